```python
import math
import jax, jax.numpy as jnp
from jax import lax
import numpy as np

D_MODEL = 1024
BATCH = 8
SEQ = 2048
DEPTH = 2
DEC_BATCH = 1
DEC_SEQ = 16384
PAST_LEN = 128

D_MIX = D_MODEL
SSM_WIDTH = D_MIX // 4
SSM_GROUP = 16
SSM_GROUPS = SSM_WIDTH // SSM_GROUP
SSM_STATE = 64
ATTN_WIDTH = D_MIX // 2
HEAD_DIM = 64
N_Q_HEADS = ATTN_WIDTH // HEAD_DIM
N_KV_HEADS = 2
Q_PER_KV = N_Q_HEADS // N_KV_HEADS
KV_WIDTH = N_KV_HEADS * HEAD_DIM
WINDOW = 128
BLOCK = 128
N_BUCKETS = 32
MAX_DISTANCE = 128
SGU_WIDTH = D_MIX - SSM_WIDTH - ATTN_WIDTH
SGU_GROUPS = 4
SGU_GROUP_WIDTH = SGU_WIDTH // SGU_GROUPS
CHUNK = 128
IN_SPLITS = (SSM_WIDTH, SSM_WIDTH, ATTN_WIDTH, KV_WIDTH, KV_WIDTH, SGU_WIDTH, SGU_WIDTH)
IN_COLS = 2 * SSM_WIDTH + ATTN_WIDTH + 2 * KV_WIDTH + 2 * SGU_WIDTH
N_EXPERTS = 32
TOP_K = 4
D_EXPERT = D_MODEL
SWIGLU_LIMIT = 7.0
SWIGLU_ALPHA = 1.702
EPS = 1e-6
NEG_INF = -1e30

kernel_name = 'hybrid_s5_window_gqa_sgu_moe_encoder'


def _rmsnorm(x, g):
    xf = x.astype(jnp.float32)
    y = xf * lax.rsqrt(jnp.mean(xf * xf, axis=-1, keepdims=True) + EPS)
    return (y * g.astype(jnp.float32)).astype(x.dtype)


def _split_points():
    pts, acc = [], 0
    for w in IN_SPLITS[:-1]:
        acc += w
        pts.append(acc)
    return pts


def _lin_combine(e1, e2):
    a1, b1 = e1
    a2, b2 = e2
    return a1 * a2, a2 * b1 + b2


def _s5_mixer(u, gate, lam_re, lam_im, log_step, b_re, b_im, c_re, c_im, d_skip):
    bsz, L, _ = u.shape
    uf = u.astype(jnp.float32).reshape(bsz, L, SSM_GROUPS, SSM_GROUP)
    states = []
    for direction in range(2):
        lam = lax.complex(lam_re[direction].astype(jnp.float32), lam_im[direction].astype(jnp.float32))
        step = jnp.exp(log_step[direction].astype(jnp.float32))[:, None]
        lam_bar = jnp.exp(lam * step)
        b = lax.complex(b_re[direction].astype(jnp.float32), b_im[direction].astype(jnp.float32))
        b_bar = ((lam_bar - 1.0) / lam)[..., None] * b
        bu = lax.complex(jnp.einsum('blgh,gph->blgp', uf, jnp.real(b_bar)),
                         jnp.einsum('blgh,gph->blgp', uf, jnp.imag(b_bar)))
        a = jnp.broadcast_to(lam_bar, bu.shape)
        _, xs = lax.associative_scan(_lin_combine, (a, bu), axis=1, reverse=(direction == 1))
        states.append(xs)
    state = states[0] + states[1]
    y = (jnp.einsum('blgp,ghp->blgh', jnp.real(state), c_re.astype(jnp.float32))
         - jnp.einsum('blgp,ghp->blgh', jnp.imag(state), c_im.astype(jnp.float32)))
    y = y.reshape(bsz, L, SSM_WIDTH) + d_skip.astype(jnp.float32) * u.astype(jnp.float32)
    out = jax.nn.gelu(y) * jax.nn.sigmoid(gate.astype(jnp.float32))
    return out.astype(u.dtype)


def _t5_bucket(rel):
    half = N_BUCKETS // 2
    max_exact = half // 2
    ret = jnp.where(rel > 0, half, 0)
    n = jnp.abs(rel)
    nf = jnp.maximum(n, 1).astype(jnp.float32)
    large = max_exact + (jnp.log(nf / max_exact) / math.log(MAX_DISTANCE / max_exact)
                         * (half - max_exact)).astype(jnp.int32)
    large = jnp.minimum(large, half - 1)
    return ret + jnp.where(n < max_exact, n, large)


def _windowed_gqa(q, k, v, sink, rel_table):
    bsz, L = q.shape[:2]
    nb = L // BLOCK
    qb = q.astype(jnp.float32).reshape(bsz, nb, BLOCK, N_KV_HEADS, Q_PER_KV, HEAD_DIM)

    def band(t):
        tp = jnp.pad(t.astype(jnp.float32), ((0, 0), (BLOCK, BLOCK), (0, 0), (0, 0)))
        tp = tp.reshape(bsz, nb + 2, BLOCK, N_KV_HEADS, HEAD_DIM)
        return jnp.concatenate([tp[:, :-2], tp[:, 1:-1], tp[:, 2:]], axis=2)

    kw, vw = band(k), band(v)
    s = jnp.einsum('bnqkgd,bnskd->bnkgqs', qb, kw) * (HEAD_DIM ** -0.5)
    q_pos = jnp.arange(BLOCK)[:, None]
    k_pos = jnp.arange(3 * BLOCK)[None, :] - BLOCK
    rel = k_pos - q_pos
    bias = rel_table.astype(jnp.float32)[_t5_bucket(rel)]
    bias = bias.transpose(2, 0, 1).reshape(N_KV_HEADS, Q_PER_KV, BLOCK, 3 * BLOCK)
    key_glob = jnp.arange(nb)[:, None] * BLOCK + k_pos
    valid = (jnp.abs(rel) <= WINDOW)[None] & ((key_glob >= 0) & (key_glob < L))[:, None, :]
    s = jnp.where(valid[None, :, None, None], s + bias, NEG_INF)
    sink_l = sink.astype(jnp.float32).reshape(N_KV_HEADS, Q_PER_KV)[None, None, :, :, None, None]
    m = jnp.maximum(jnp.max(s, axis=-1, keepdims=True), sink_l)
    p = jnp.exp(s - m)
    denom = jnp.sum(p, axis=-1, keepdims=True) + jnp.exp(sink_l - m)
    o = jnp.einsum('bnkgqs,bnskd->bnqkgd', p / denom, vw)
    return o.reshape(bsz, L, ATTN_WIDTH).astype(q.dtype)


def _spatial_gating(u, v, g_norm, w_s, b_s):
    bsz, L, _ = u.shape
    nc = L // CHUNK
    u = jax.nn.gelu(u)
    v = _rmsnorm(jax.nn.gelu(v), g_norm)
    vb = v.reshape(bsz, nc, CHUNK, SGU_GROUPS, SGU_GROUP_WIDTH)
    mixed = jnp.einsum('gts,bnsgc->bntgc', w_s, vb) + b_s.T[:, :, None]
    return u * mixed.reshape(bsz, L, SGU_WIDTH)


def _moe(h, w_router, b_router, w_gate_up, b_gate_up, w_down, b_down):
    n_tok = h.shape[0]
    logits = (h @ w_router + b_router).astype(jnp.float32)
    top_logit, top_idx = lax.top_k(logits, TOP_K)
    weights = jax.nn.softmax(top_logit, axis=-1)
    flat_expert = top_idx.reshape(-1)
    order = jnp.argsort(flat_expert)
    sorted_expert = flat_expert[order]
    token_of_row = order // TOP_K
    group_sizes = jnp.bincount(flat_expert, length=N_EXPERTS).astype(jnp.int32)
    rows = h[token_of_row]
    gu = lax.ragged_dot(rows, w_gate_up, group_sizes) + b_gate_up[sorted_expert]
    gate = jnp.minimum(gu[:, :D_EXPERT], SWIGLU_LIMIT)
    up = jnp.clip(gu[:, D_EXPERT:], -SWIGLU_LIMIT, SWIGLU_LIMIT)
    act = (up + 1.0) * gate * jax.nn.sigmoid(SWIGLU_ALPHA * gate)
    out = lax.ragged_dot(act, w_down, group_sizes) + b_down[sorted_expert]
    out = out * weights.reshape(-1)[order][:, None].astype(out.dtype)
    return jax.ops.segment_sum(out, token_of_row, num_segments=n_tok)


def _trunk(x, norm_mix, w_in, ssm_lam_re, ssm_lam_im, ssm_log_step, ssm_b_re, ssm_b_im,
           ssm_c_re, ssm_c_im, ssm_d, attn_sink, rel_bias, sgu_norm, sgu_w, sgu_b,
           out_norm, w_out, norm_ffn, w_router, b_router, w_gate_up, b_gate_up,
           w_down, b_down, final_norm):
    bsz, L, _ = x.shape
    for layer in range(DEPTH):
        h = _rmsnorm(x, norm_mix[layer])
        proj = h @ w_in[layer]
        s_u, s_gate, q, k, v, g_u, g_v = jnp.split(proj, _split_points(), axis=-1)
        y_a = _s5_mixer(s_u, s_gate, ssm_lam_re[layer], ssm_lam_im[layer], ssm_log_step[layer],
                        ssm_b_re[layer], ssm_b_im[layer], ssm_c_re[layer], ssm_c_im[layer], ssm_d[layer])
        y_b = _windowed_gqa(q.reshape(bsz, L, N_Q_HEADS, HEAD_DIM),
                            k.reshape(bsz, L, N_KV_HEADS, HEAD_DIM),
                            v.reshape(bsz, L, N_KV_HEADS, HEAD_DIM),
                            attn_sink[layer], rel_bias)
        y_c = _spatial_gating(g_u, g_v, sgu_norm[layer], sgu_w[layer], sgu_b[layer])
        g = out_norm[layer]
        mixed = jnp.concatenate([
            _rmsnorm(y_a, g[:SSM_WIDTH]),
            _rmsnorm(y_b, g[SSM_WIDTH:SSM_WIDTH + ATTN_WIDTH]),
            _rmsnorm(y_c, g[SSM_WIDTH + ATTN_WIDTH:]),
        ], axis=-1)
        x = x + mixed @ w_out[layer]
        h = _rmsnorm(x, norm_ffn[layer])
        ff = _moe(h.reshape(bsz * L, D_MODEL), w_router[layer], b_router[layer], w_gate_up[layer],
                  b_gate_up[layer], w_down[layer], b_down[layer])
        x = x + ff.reshape(bsz, L, D_MODEL)
    return _rmsnorm(x, final_norm)


def setup_inputs(seed: int = 0) -> dict:
    key = jax.random.key(seed)
    ks = jax.random.split(key, 32)
    f32 = jnp.float32

    def nrm(k, shape, scale):
        return jax.random.normal(k, shape, f32) * scale

    G, P, H = SSM_GROUPS, SSM_STATE, SSM_GROUP
    n_idx = jnp.arange(P, dtype=f32)
    return {
        'x_prompt': nrm(ks[0], (BATCH, SEQ, D_MODEL), 1.0),
        'x_sample': nrm(ks[1], (DEC_BATCH, DEC_SEQ, D_MODEL), 1.0),
        'norm_mix': 1.0 + nrm(ks[2], (DEPTH, D_MODEL), 0.02),
        'w_in': nrm(ks[3], (DEPTH, D_MODEL, IN_COLS), D_MODEL ** -0.5),
        'ssm_lam_re': -0.5 + nrm(ks[4], (DEPTH, 2, G, P), 0.01),
        'ssm_lam_im': math.pi * n_idx + nrm(ks[5], (DEPTH, 2, G, P), 0.01),
        'ssm_log_step': jax.random.uniform(ks[6], (DEPTH, 2, G), f32, math.log(1e-3), math.log(1e-1)),
        'ssm_b_re': nrm(ks[7], (DEPTH, 2, G, P, H), (2 * H) ** -0.5),
        'ssm_b_im': nrm(ks[8], (DEPTH, 2, G, P, H), (2 * H) ** -0.5),
        'ssm_c_re': nrm(ks[9], (DEPTH, G, H, P), (2 * P) ** -0.5),
        'ssm_c_im': nrm(ks[10], (DEPTH, G, H, P), (2 * P) ** -0.5),
        'ssm_d': 1.0 + nrm(ks[11], (DEPTH, SSM_WIDTH), 0.1),
        'attn_sink': nrm(ks[12], (DEPTH, N_Q_HEADS), 0.5),
        'rel_bias': nrm(ks[13], (N_BUCKETS, N_Q_HEADS), 0.5),
        'sgu_norm': 1.0 + nrm(ks[14], (DEPTH, SGU_WIDTH), 0.02),
        'sgu_w': nrm(ks[15], (DEPTH, SGU_GROUPS, CHUNK, CHUNK), CHUNK ** -0.5),
        'sgu_b': 1.0 + nrm(ks[16], (DEPTH, SGU_GROUPS, CHUNK), 0.02),
        'out_norm': 1.0 + nrm(ks[17], (DEPTH, D_MIX), 0.02),
        'w_out': nrm(ks[18], (DEPTH, D_MIX, D_MODEL), D_MIX ** -0.5),
        'norm_ffn': 1.0 + nrm(ks[19], (DEPTH, D_MODEL), 0.02),
        'w_router': nrm(ks[20], (DEPTH, D_MODEL, N_EXPERTS), D_MODEL ** -0.5),
        'b_router': nrm(ks[21], (DEPTH, N_EXPERTS), 0.01),
        'w_gate_up': nrm(ks[22], (DEPTH, N_EXPERTS, D_MODEL, 2 * D_EXPERT), D_MODEL ** -0.5),
        'b_gate_up': nrm(ks[23], (DEPTH, N_EXPERTS, 2 * D_EXPERT), 0.02),
        'w_down': nrm(ks[24], (DEPTH, N_EXPERTS, D_EXPERT, D_MODEL), D_EXPERT ** -0.5),
        'b_down': nrm(ks[25], (DEPTH, N_EXPERTS, D_MODEL), 0.02),
        'final_norm': 1.0 + nrm(ks[26], (D_MODEL,), 0.02),
    }


def reference(x_prompt, x_sample, norm_mix, w_in, ssm_lam_re, ssm_lam_im, ssm_log_step, ssm_b_re,
              ssm_b_im, ssm_c_re, ssm_c_im, ssm_d, attn_sink, rel_bias, sgu_norm, sgu_w, sgu_b,
              out_norm, w_out, norm_ffn, w_router, b_router, w_gate_up, b_gate_up, w_down, b_down,
              final_norm):
    y_prompt = _trunk(x_prompt, norm_mix, w_in, ssm_lam_re, ssm_lam_im, ssm_log_step, ssm_b_re, ssm_b_im,
                      ssm_c_re, ssm_c_im, ssm_d, attn_sink, rel_bias, sgu_norm, sgu_w, sgu_b,
                      out_norm, w_out, norm_ffn, w_router, b_router, w_gate_up, b_gate_up,
                      w_down, b_down, final_norm)
    y_sample = _trunk(x_sample, norm_mix, w_in, ssm_lam_re, ssm_lam_im, ssm_log_step, ssm_b_re, ssm_b_im,
                      ssm_c_re, ssm_c_im, ssm_d, attn_sink, rel_bias, sgu_norm, sgu_w, sgu_b,
                      out_norm, w_out, norm_ffn, w_router, b_router, w_gate_up, b_gate_up,
                      w_down, b_down, final_norm)
    return (y_prompt, y_sample)
```

```python
import functools
import math

import jax
import jax.numpy as jnp
from jax import lax
from jax.experimental import pallas as pl
from jax.experimental.pallas import tpu as pltpu

D_MODEL = 1024
SSM_WIDTH = 256
SSM_GROUP = 16
SSM_GROUPS = 16
SSM_STATE = 64
SSM_COMPLEX = SSM_GROUPS * SSM_STATE
SSM_REAL = 2 * SSM_COMPLEX
ATTN_WIDTH = 512
HEAD_DIM = 64
N_Q_HEADS = 8
N_KV_HEADS = 2
Q_PER_KV = 4
KV_WIDTH = 128
BLOCK = 128
N_BUCKETS = 32
MAX_DISTANCE = 128
SGU_WIDTH = 256
SGU_GROUPS = 4
SGU_GROUP_WIDTH = 64
IN_COLS = 1792
QKV_COLS = ATTN_WIDTH + 2 * KV_WIDTH
N_EXPERTS = 32
TOP_K = 4
D_EXPERT = 1024
SWIGLU_LIMIT = 7.0
SWIGLU_ALPHA = 1.702
EPS = 1e-6
NEG_INF = -1e30

N_SEG = 8
V7X_VMEM_LIMIT = 56 * 1024 * 1024

F32 = jnp.float32
BF16 = jnp.bfloat16


def _rms(xf, g):
    return xf * lax.rsqrt(jnp.mean(xf * xf, axis=-1, keepdims=True) + EPS) * g


def _cparams(*sem):
    return pltpu.CompilerParams(dimension_semantics=sem, vmem_limit_bytes=V7X_VMEM_LIMIT)


def _inproj_kernel(x_ref, g_ref, w_ref, ssm_ref, qkv_ref, sgu_ref):
    h = _rms(x_ref[...], g_ref[...]).astype(BF16)
    p = jnp.dot(h, w_ref[...], preferred_element_type=F32)
    ssm_ref[...] = p[:, :2 * SSM_WIDTH].astype(BF16)
    qkv_ref[...] = p[:, 2 * SSM_WIDTH:2 * SSM_WIDTH + QKV_COLS].astype(BF16)
    sgu_ref[...] = p[:, 2 * SSM_WIDTH + QKV_COLS:].astype(BF16)


def _inproj(x, g, w, tm):
    t = x.shape[0]
    return pl.pallas_call(
        _inproj_kernel,
        grid=(t // tm,),
        in_specs=[
            pl.BlockSpec((tm, D_MODEL), lambda i: (i, 0)),
            pl.BlockSpec((1, D_MODEL), lambda i: (0, 0)),
            pl.BlockSpec((D_MODEL, IN_COLS), lambda i: (0, 0)),
        ],
        out_specs=[
            pl.BlockSpec((tm, 2 * SSM_WIDTH), lambda i: (i, 0)),
            pl.BlockSpec((tm, QKV_COLS), lambda i: (i, 0)),
            pl.BlockSpec((tm, 2 * SGU_WIDTH), lambda i: (i, 0)),
        ],
        out_shape=[
            jax.ShapeDtypeStruct((t, 2 * SSM_WIDTH), BF16),
            jax.ShapeDtypeStruct((t, QKV_COLS), BF16),
            jax.ShapeDtypeStruct((t, 2 * SGU_WIDTH), BF16),
        ],
        compiler_params=_cparams("arbitrary"),
        name="inproj",
    )(x, g, w)


SCAN_COL_BLOCKS = 2


def _ssm_kernel(u_ref, b_ref, a_ref, c_ref, init_ref, *rest, tc, emit_y):
    if emit_y:
        y_ref, fin_ref, bu_scr, st_scr = rest
    else:
        fin_ref, bu_scr, st_scr = rest
    d = pl.program_id(0)
    c = pl.program_id(1)

    @pl.when(c == 0)
    def _():
        st_scr[...] = init_ref[0]

    bu_scr[...] = jnp.dot(u_ref[...], b_ref[0], preferred_element_type=F32)
    w = SSM_COMPLEX // SCAN_COL_BLOCKS
    for cb in range(SCAN_COL_BLOCKS):
        re_cols = slice(cb * w, (cb + 1) * w)
        im_cols = slice(SSM_COMPLEX + cb * w, SSM_COMPLEX + (cb + 1) * w)
        a_re = jnp.broadcast_to(a_ref[0, 0:1, re_cols], (N_SEG, w))
        a_im = jnp.broadcast_to(a_ref[0, 1:2, re_cols], (N_SEG, w))

        def body(tt, carry, re_cols=re_cols, im_cols=im_cols, a_re=a_re, a_im=a_im):
            x_re, x_im = carry
            t = jnp.where(d == 0, tt, tc - 1 - tt)
            rows = pl.ds(pl.multiple_of(t * N_SEG, N_SEG), N_SEG)
            n_re = a_re * x_re - a_im * x_im + bu_scr[rows, re_cols]
            n_im = a_re * x_im + a_im * x_re + bu_scr[rows, im_cols]
            bu_scr[rows, re_cols] = n_re
            bu_scr[rows, im_cols] = n_im
            return n_re, n_im

        x_re, x_im = lax.fori_loop(0, tc, body, (st_scr[:, re_cols], st_scr[:, im_cols]), unroll=4)
        st_scr[:, re_cols] = x_re
        st_scr[:, im_cols] = x_im

    if emit_y:
        y_ref[0] = jnp.dot(bu_scr[...].astype(BF16), c_ref[...], preferred_element_type=F32)

    @pl.when(c == pl.num_programs(1) - 1)
    def _():
        fin_ref[0] = st_scr[...]


def _ssm_scan(u_tm, b_blk, a_rows, c_blk, init, tc, emit_y):
    rows = u_tm.shape[0]
    rb = tc * N_SEG
    nc = rows // rb

    def chunk(d, c):
        return jnp.where(d == 0, c, nc - 1 - c)

    out_specs = [pl.BlockSpec((1, N_SEG, SSM_REAL), lambda d, c: (d, 0, 0))]
    out_shape = [jax.ShapeDtypeStruct((2, N_SEG, SSM_REAL), F32)]
    if emit_y:
        out_specs = [pl.BlockSpec((1, rb, SSM_WIDTH), lambda d, c: (d, chunk(d, c), 0))] + out_specs
        out_shape = [jax.ShapeDtypeStruct((2, rows, SSM_WIDTH), F32)] + out_shape
    res = pl.pallas_call(
        functools.partial(_ssm_kernel, tc=tc, emit_y=emit_y),
        grid=(2, nc),
        in_specs=[
            pl.BlockSpec((rb, SSM_WIDTH), lambda d, c: (chunk(d, c), 0)),
            pl.BlockSpec((1, SSM_WIDTH, SSM_REAL), lambda d, c: (d, 0, 0)),
            pl.BlockSpec((1, 2, SSM_COMPLEX), lambda d, c: (d, 0, 0)),
            pl.BlockSpec((SSM_REAL, SSM_WIDTH), lambda d, c: (0, 0)),
            pl.BlockSpec((1, N_SEG, SSM_REAL), lambda d, c: (d, 0, 0)),
        ],
        out_specs=out_specs,
        out_shape=out_shape,
        scratch_shapes=[pltpu.VMEM((rb, SSM_REAL), F32), pltpu.VMEM((N_SEG, SSM_REAL), F32)],
        compiler_params=_cparams("arbitrary", "arbitrary"),
        name="ssm_scan_y" if emit_y else "ssm_scan_state",
    )(u_tm, b_blk, a_rows, c_blk, init)
    return (res[0], res[1]) if emit_y else (None, res[0])


def _ssm_params(lam_re, lam_im, log_step, b_re, b_im, c_re, c_im):
    lam = lax.complex(lam_re.astype(F32), lam_im.astype(F32))
    step = jnp.exp(log_step.astype(F32))[..., None]
    lam_bar = jnp.exp(lam * step)
    b = lax.complex(b_re.astype(F32), b_im.astype(F32))
    b_bar = ((lam_bar - 1.0) / lam)[..., None] * b
    eye = jnp.eye(SSM_GROUPS, dtype=F32)
    b_blk_re = jnp.einsum('dgph,gk->dghkp', jnp.real(b_bar), eye).reshape(2, SSM_WIDTH, SSM_COMPLEX)
    b_blk_im = jnp.einsum('dgph,gk->dghkp', jnp.imag(b_bar), eye).reshape(2, SSM_WIDTH, SSM_COMPLEX)
    b_blk = jnp.concatenate([b_blk_re, b_blk_im], axis=-1).astype(BF16)
    c_blk_re = jnp.einsum('ghp,gk->gpkh', c_re.astype(F32), eye).reshape(SSM_COMPLEX, SSM_WIDTH)
    c_blk_im = jnp.einsum('ghp,gk->gpkh', c_im.astype(F32), eye).reshape(SSM_COMPLEX, SSM_WIDTH)
    c_blk = jnp.concatenate([c_blk_re, -c_blk_im], axis=0).astype(BF16)
    a_rows = jnp.stack([jnp.real(lam_bar).reshape(2, SSM_COMPLEX),
                        jnp.imag(lam_bar).reshape(2, SSM_COMPLEX)], axis=1)
    return lam_bar.reshape(2, SSM_COMPLEX), b_blk, a_rows, c_blk


def _to_time_major(a, seg_len):
    n = a.shape[0] // seg_len
    return a.reshape(n // N_SEG, N_SEG, seg_len, a.shape[-1]).transpose(0, 2, 1, 3).reshape(a.shape)


def _from_time_major(a, seg_len):
    n = a.shape[0] // seg_len
    return a.reshape(n // N_SEG, seg_len, N_SEG, a.shape[-1]).transpose(0, 2, 1, 3).reshape(a.shape)


def _chain_init(fin, lam_pow):
    f = lax.complex(fin[..., :SSM_COMPLEX], fin[..., SSM_COMPLEX:])
    zero = jnp.zeros_like(f[0, 0])
    fwd = [zero]
    for s in range(1, N_SEG):
        fwd.append(lam_pow[0] * fwd[-1] + f[0, s - 1])
    bwd = [zero]
    for s in range(N_SEG - 2, -1, -1):
        bwd.append(lam_pow[1] * bwd[-1] + f[1, s + 1])
    init = jnp.stack([jnp.stack(fwd), jnp.stack(bwd[::-1])])
    return jnp.concatenate([jnp.real(init), jnp.imag(init)], axis=-1).astype(F32)


def _attn_sgu_kernel(flags_ref, q_ref, kp_ref, kc_ref, kn_ref, vp_ref, vc_ref, vn_ref, bias_ref, sink_ref,
                     sgu_ref, sgn_ref, sgw_ref, sgb_ref, attn_ref, sguo_ref):
    n = pl.program_id(0)
    has_prev = flags_ref[0, n] > 0
    has_next = flags_ref[1, n] > 0
    q = q_ref[...]
    ks = (kp_ref[...], kc_ref[...], kn_ref[...])
    vs = (vp_ref[...], vc_ref[...], vn_ref[...])
    outs = []
    for j in range(N_KV_HEADS):
        qs = jnp.concatenate(
            [q[:, (j * Q_PER_KV + i) * HEAD_DIM:(j * Q_PER_KV + i + 1) * HEAD_DIM] for i in range(Q_PER_KV)], axis=0)
        kv_cols = slice(j * HEAD_DIM, (j + 1) * HEAD_DIM)
        s = []
        for b in range(3):
            sb = lax.dot_general(qs, ks[b][:, kv_cols], (((1,), (1,)), ((), ())), preferred_element_type=F32)
            sb = sb * (HEAD_DIM ** -0.5) + bias_ref[j, :, b * BLOCK:(b + 1) * BLOCK]
            if b == 0:
                sb = jnp.where(has_prev, sb, NEG_INF)
            if b == 2:
                sb = jnp.where(has_next, sb, NEG_INF)
            s.append(sb)
        sink = sink_ref[j]
        m = jnp.max(jnp.maximum(jnp.maximum(s[0], s[1]), s[2]), axis=-1, keepdims=True)
        m = jnp.maximum(m, sink)
        p = [jnp.exp(sb - m) for sb in s]
        denom = jnp.sum(p[0] + p[1] + p[2], axis=-1, keepdims=True) + jnp.exp(sink - m)
        o = sum(jnp.dot(p[b].astype(BF16), vs[b][:, kv_cols], preferred_element_type=F32) for b in range(3))
        o = o / denom
        outs += [o[i * BLOCK:(i + 1) * BLOCK] for i in range(Q_PER_KV)]
    attn_ref[...] = jnp.concatenate(outs, axis=1).astype(BF16)

    gu = jax.nn.gelu(sgu_ref[:, :SGU_WIDTH].astype(F32))
    gv = _rms(jax.nn.gelu(sgu_ref[:, SGU_WIDTH:].astype(F32)), sgn_ref[...]).astype(BF16)
    mixed = jnp.concatenate(
        [jnp.dot(sgw_ref[g], gv[:, g * SGU_GROUP_WIDTH:(g + 1) * SGU_GROUP_WIDTH], preferred_element_type=F32)
         for g in range(SGU_GROUPS)], axis=1) + sgb_ref[...]
    sguo_ref[...] = (gu * mixed).astype(BF16)


def _attn_sgu(flags, qkv, sgu, bias, sink, sgn, sgw, sgb):
    t = qkv.shape[0]
    nb = t // BLOCK
    kcol = ATTN_WIDTH // KV_WIDTH
    vcol = kcol + 1

    def blk(col, off):
        return pl.BlockSpec((BLOCK, KV_WIDTH), lambda n, f: (jnp.clip(n + off, 0, nb - 1), col))

    grid_spec = pltpu.PrefetchScalarGridSpec(
        num_scalar_prefetch=1,
        grid=(nb,),
        in_specs=[
            pl.BlockSpec((BLOCK, ATTN_WIDTH), lambda n, f: (n, 0)),
            blk(kcol, -1), blk(kcol, 0), blk(kcol, 1),
            blk(vcol, -1), blk(vcol, 0), blk(vcol, 1),
            pl.BlockSpec((N_KV_HEADS, Q_PER_KV * BLOCK, 3 * BLOCK), lambda n, f: (0, 0, 0)),
            pl.BlockSpec((N_KV_HEADS, Q_PER_KV * BLOCK, 1), lambda n, f: (0, 0, 0)),
            pl.BlockSpec((BLOCK, 2 * SGU_WIDTH), lambda n, f: (n, 0)),
            pl.BlockSpec((1, SGU_WIDTH), lambda n, f: (0, 0)),
            pl.BlockSpec((SGU_GROUPS, BLOCK, BLOCK), lambda n, f: (0, 0, 0)),
            pl.BlockSpec((BLOCK, SGU_WIDTH), lambda n, f: (0, 0)),
        ],
        out_specs=[
            pl.BlockSpec((BLOCK, ATTN_WIDTH), lambda n, f: (n, 0)),
            pl.BlockSpec((BLOCK, SGU_WIDTH), lambda n, f: (n, 0)),
        ],
    )
    return pl.pallas_call(
        _attn_sgu_kernel,
        grid_spec=grid_spec,
        out_shape=[jax.ShapeDtypeStruct((t, ATTN_WIDTH), BF16), jax.ShapeDtypeStruct((t, SGU_WIDTH), BF16)],
        compiler_params=_cparams("arbitrary"),
        name="attn_sgu",
    )(flags, qkv, qkv, qkv, qkv, qkv, qkv, qkv, bias, sink, sgu, sgn, sgw, sgb)


def _t5_bucket(rel):
    half = N_BUCKETS // 2
    max_exact = half // 2
    ret = jnp.where(rel > 0, half, 0)
    n = jnp.abs(rel)
    nf = jnp.maximum(n, 1).astype(F32)
    large = max_exact + (jnp.log(nf / max_exact) / math.log(MAX_DISTANCE / max_exact)
                         * (half - max_exact)).astype(jnp.int32)
    large = jnp.minimum(large, half - 1)
    return ret + jnp.where(n < max_exact, n, large)


def _band_bias(rel_table):
    q_pos = jnp.arange(BLOCK)[:, None]
    k_pos = jnp.arange(3 * BLOCK)[None, :] - BLOCK
    rel = k_pos - q_pos
    bias = rel_table.astype(F32)[_t5_bucket(rel)]
    bias = jnp.where((jnp.abs(rel) <= BLOCK)[..., None], bias, NEG_INF)
    return bias.transpose(2, 0, 1).reshape(N_KV_HEADS, Q_PER_KV * BLOCK, 3 * BLOCK)


def _outproj_router_kernel(x_ref, ssm_ref, yf_ref, yb_ref, attn_ref, sguo_ref, d_ref, gout_ref, wout_ref,
                           gffn_ref, wr_ref, br_ref, x1_ref, h_ref, idx_ref, rank_ref, wgt_ref, cnt_ref,
                           run_scr, *, tm):
    i = pl.program_id(0)

    @pl.when(i == 0)
    def _():
        run_scr[...] = jnp.zeros_like(run_scr)

    u = ssm_ref[:, :SSM_WIDTH].astype(F32)
    gate = ssm_ref[:, SSM_WIDTH:].astype(F32)
    ya = yf_ref[...] + yb_ref[...] + d_ref[...] * u
    ya = jax.nn.gelu(ya) * jax.nn.sigmoid(gate)
    g = gout_ref[...]
    mixed = jnp.concatenate([
        _rms(ya, g[:, :SSM_WIDTH]),
        _rms(attn_ref[...].astype(F32), g[:, SSM_WIDTH:SSM_WIDTH + ATTN_WIDTH]),
        _rms(sguo_ref[...].astype(F32), g[:, SSM_WIDTH + ATTN_WIDTH:]),
    ], axis=1).astype(BF16)
    x1 = x_ref[...] + jnp.dot(mixed, wout_ref[...], preferred_element_type=F32)
    x1_ref[...] = x1
    h = _rms(x1, gffn_ref[...])
    h_ref[...] = h

    logits = jnp.dot(h, wr_ref[...], preferred_element_type=F32, precision=lax.Precision.HIGHEST) + br_ref[...]
    lane = lax.broadcasted_iota(jnp.int32, (tm, N_EXPERTS), 1)
    vals, idxs, sels = [], [], []
    lg = logits
    for _ in range(TOP_K):
        m = jnp.max(lg, axis=-1, keepdims=True)
        idx = jnp.min(jnp.where(lg == m, lane, N_EXPERTS), axis=-1, keepdims=True)
        sel = lane == idx
        lg = jnp.where(sel, -jnp.inf, lg)
        vals.append(m)
        idxs.append(idx)
        sels.append(sel)
    e = [jnp.exp(v - vals[0]) for v in vals]
    tot = e[0] + e[1] + e[2] + e[3]
    wgt_ref[...] = jnp.concatenate([ek / tot for ek in e], axis=1)
    idx_ref[...] = jnp.concatenate(idxs, axis=1)

    picked = (sels[0] | sels[1] | sels[2] | sels[3])
    pm = jnp.where(picked, 1.0, 0.0)
    r_io = lax.broadcasted_iota(jnp.int32, (tm, tm), 0)
    c_io = lax.broadcasted_iota(jnp.int32, (tm, tm), 1)
    tri = jnp.where(c_io < r_io, 1.0, 0.0).astype(BF16)
    before = jnp.dot(tri, pm.astype(BF16), preferred_element_type=F32) + run_scr[...]
    rank_ref[...] = jnp.concatenate(
        [jnp.sum(jnp.where(s, before, 0.0), axis=-1, keepdims=True) for s in sels], axis=1).astype(jnp.int32)
    run_scr[...] = run_scr[...] + jnp.sum(pm, axis=0, keepdims=True)

    @pl.when(i == pl.num_programs(0) - 1)
    def _():
        cnt_ref[...] = run_scr[...].astype(jnp.int32)


def _outproj_router(x, ssm, yf, yb, attn, sguo, d_skip, g_out, w_out, g_ffn, w_r, b_r, tm):
    t = x.shape[0]
    row = lambda w: pl.BlockSpec((tm, w), lambda i: (i, 0))
    full = lambda a, b: pl.BlockSpec((a, b), lambda i: (0, 0))
    return pl.pallas_call(
        functools.partial(_outproj_router_kernel, tm=tm),
        grid=(t // tm,),
        in_specs=[
            row(D_MODEL), row(2 * SSM_WIDTH), row(SSM_WIDTH), row(SSM_WIDTH), row(ATTN_WIDTH), row(SGU_WIDTH),
            full(1, SSM_WIDTH), full(1, D_MODEL), full(D_MODEL, D_MODEL), full(1, D_MODEL),
            full(D_MODEL, N_EXPERTS), full(1, N_EXPERTS),
        ],
        out_specs=[row(D_MODEL), row(D_MODEL), row(TOP_K), row(TOP_K), row(TOP_K), full(1, N_EXPERTS)],
        out_shape=[
            jax.ShapeDtypeStruct((t, D_MODEL), F32),
            jax.ShapeDtypeStruct((t, D_MODEL), F32),
            jax.ShapeDtypeStruct((t, TOP_K), jnp.int32),
            jax.ShapeDtypeStruct((t, TOP_K), jnp.int32),
            jax.ShapeDtypeStruct((t, TOP_K), F32),
            jax.ShapeDtypeStruct((1, N_EXPERTS), jnp.int32),
        ],
        scratch_shapes=[pltpu.VMEM((1, N_EXPERTS), F32)],
        compiler_params=_cparams("arbitrary"),
        name="outproj_router",
    )(x, ssm, yf, yb, attn, sguo, d_skip, g_out, w_out, g_ffn, w_r, b_r)


def _row_copy(src_hbm, dst_vmem, sem, src_row, dst_row):
    return pltpu.make_async_copy(src_hbm.at[pl.ds(src_row, 1)], dst_vmem.at[pl.ds(dst_row, 1)], sem)


def _experts_kernel(te_ref, tv_ref, tok_ref, h_hbm, wgu_ref, bgu_ref, wd_ref, bd_ref, out_ref, x_buf, sem, *, tm):
    i = pl.program_id(0)

    @pl.when(tv_ref[i] > 0)
    def _():
        def start(r, carry):
            _row_copy(h_hbm, x_buf, sem, tok_ref[0, 0, r], r).start()
            return carry

        lax.fori_loop(0, tm, start, 0)

        def wait(r, carry):
            _row_copy(h_hbm, x_buf, sem, 0, r).wait()
            return carry

        lax.fori_loop(0, tm, wait, 0)
        x = x_buf[...].astype(BF16)
        gu = jnp.dot(x, wgu_ref[0], preferred_element_type=F32) + bgu_ref[0]
        gate = jnp.minimum(gu[:, :D_EXPERT], SWIGLU_LIMIT)
        up = jnp.clip(gu[:, D_EXPERT:], -SWIGLU_LIMIT, SWIGLU_LIMIT)
        act = (up + 1.0) * gate * jax.nn.sigmoid(SWIGLU_ALPHA * gate)
        out_ref[...] = jnp.dot(act.astype(BF16), wd_ref[0], preferred_element_type=F32) + bd_ref[0]

    @pl.when(tv_ref[i] == 0)
    def _():
        out_ref[...] = jnp.zeros_like(out_ref)


def _experts(tile_expert, tile_valid, row_token, h, wgu, bgu, wd, bd, tm):
    nt = tile_expert.shape[0]
    grid_spec = pltpu.PrefetchScalarGridSpec(
        num_scalar_prefetch=2,
        grid=(nt,),
        in_specs=[
            pl.BlockSpec((1, 1, tm), lambda i, te, tv: (i, 0, 0), memory_space=pltpu.SMEM),
            pl.BlockSpec(memory_space=pl.ANY),
            pl.BlockSpec((1, D_MODEL, 2 * D_EXPERT), lambda i, te, tv: (te[i], 0, 0)),
            pl.BlockSpec((1, 1, 2 * D_EXPERT), lambda i, te, tv: (te[i], 0, 0)),
            pl.BlockSpec((1, D_EXPERT, D_MODEL), lambda i, te, tv: (te[i], 0, 0)),
            pl.BlockSpec((1, 1, D_MODEL), lambda i, te, tv: (te[i], 0, 0)),
        ],
        out_specs=pl.BlockSpec((tm, D_MODEL), lambda i, te, tv: (i, 0)),
        scratch_shapes=[pltpu.VMEM((tm, D_MODEL), F32), pltpu.SemaphoreType.DMA(())],
    )
    return pl.pallas_call(
        functools.partial(_experts_kernel, tm=tm),
        grid_spec=grid_spec,
        out_shape=jax.ShapeDtypeStruct((nt * tm, D_MODEL), F32),
        compiler_params=_cparams("arbitrary"),
        name="experts",
    )(tile_expert, tile_valid, row_token.reshape(nt, 1, tm), h, wgu, bgu, wd, bd)


def _combine_kernel(pos_ref, ys_hbm, x1_ref, wgt_ref, g_ref, out_ref, y_buf, sem, *, tm, final):
    def start(r, carry):
        for k in range(TOP_K):
            _row_copy(ys_hbm, y_buf.at[k], sem, pos_ref[0, 0, r * TOP_K + k], r).start()
        return carry

    lax.fori_loop(0, tm, start, 0)

    def wait(r, carry):
        for k in range(TOP_K):
            _row_copy(ys_hbm, y_buf.at[k], sem, 0, r).wait()
        return carry

    lax.fori_loop(0, tm, wait, 0)
    w = wgt_ref[...]
    x2 = x1_ref[...]
    for k in range(TOP_K):
        x2 = x2 + y_buf[k] * w[:, k:k + 1]
    if final:
        x2 = _rms(x2, g_ref[...])
    out_ref[...] = x2


def _combine(pos, ys, x1, wgt, g_final, tm, final):
    t = x1.shape[0]
    return pl.pallas_call(
        functools.partial(_combine_kernel, tm=tm, final=final),
        grid=(t // tm,),
        in_specs=[
            pl.BlockSpec((1, 1, tm * TOP_K), lambda i: (i, 0, 0), memory_space=pltpu.SMEM),
            pl.BlockSpec(memory_space=pl.ANY),
            pl.BlockSpec((tm, D_MODEL), lambda i: (i, 0)),
            pl.BlockSpec((tm, TOP_K), lambda i: (i, 0)),
            pl.BlockSpec((1, D_MODEL), lambda i: (0, 0)),
        ],
        out_specs=pl.BlockSpec((tm, D_MODEL), lambda i: (i, 0)),
        out_shape=jax.ShapeDtypeStruct((t, D_MODEL), F32),
        scratch_shapes=[pltpu.VMEM((TOP_K, tm, D_MODEL), F32), pltpu.SemaphoreType.DMA(())],
        compiler_params=_cparams("arbitrary"),
        name="combine",
    )(pos.reshape(t // tm, 1, tm * TOP_K), ys, x1, wgt, g_final)


def _block_flags(n_prompt_seq, seg_len, n_tokens):
    nb = n_tokens // BLOCK
    per_seg = seg_len // BLOCK
    b = jnp.arange(nb)
    prompt_blocks = n_prompt_seq * per_seg
    in_prompt = b < prompt_blocks
    first = jnp.where(in_prompt, b % per_seg == 0, b == prompt_blocks)
    last = jnp.where(in_prompt, b % per_seg == per_seg - 1, b == nb - 1)
    return jnp.stack([~first, ~last]).astype(jnp.int32)


def _trunk(x, n_prompt_seq, seg_len, p, tm, tm_moe, tc):
    t = x.shape[0]
    t_prompt = n_prompt_seq * seg_len
    depth = p['w_in'].shape[0]
    flags = _block_flags(n_prompt_seq, seg_len, t)
    bias = _band_bias(p['rel_bias'])
    nt = (t * TOP_K) // tm_moe + N_EXPERTS
    zero_init = jnp.zeros((2, N_SEG, SSM_REAL), F32)
    for layer in range(depth):
        ssm, qkv, sgu = _inproj(x, p['norm_mix'][layer][None], p['w_in'][layer].astype(BF16), tm)

        lam_bar, b_blk, a_rows, c_blk = _ssm_params(
            p['ssm_lam_re'][layer], p['ssm_lam_im'][layer], p['ssm_log_step'][layer], p['ssm_b_re'][layer],
            p['ssm_b_im'][layer], p['ssm_c_re'][layer], p['ssm_c_im'][layer])
        u_tm = _to_time_major(ssm[:, :SSM_WIDTH], seg_len)
        y_p, _ = _ssm_scan(u_tm[:t_prompt], b_blk, a_rows, c_blk, zero_init, tc, True)
        _, fin = _ssm_scan(u_tm[t_prompt:], b_blk, a_rows, c_blk, zero_init, tc, False)
        init = _chain_init(fin, lam_bar ** seg_len)
        y_s, _ = _ssm_scan(u_tm[t_prompt:], b_blk, a_rows, c_blk, init, tc, True)
        y_tm = jnp.concatenate([y_p, y_s], axis=1)
        yf = _from_time_major(y_tm[0], seg_len)
        yb = _from_time_major(y_tm[1], seg_len)

        sink = jnp.repeat(p['attn_sink'][layer].astype(F32), BLOCK).reshape(N_KV_HEADS, Q_PER_KV * BLOCK, 1)
        sgb = jnp.repeat(p['sgu_b'][layer].astype(F32).T, SGU_GROUP_WIDTH, axis=1)
        attn, sguo = _attn_sgu(flags, qkv, sgu, bias, sink, p['sgu_norm'][layer][None].astype(F32),
                               p['sgu_w'][layer].astype(BF16), sgb)

        x1, h, idx, rank, wgt, cnt = _outproj_router(
            x, ssm, yf, yb, attn, sguo, p['ssm_d'][layer][None].astype(F32), p['out_norm'][layer][None],
            p['w_out'][layer].astype(BF16), p['norm_ffn'][layer][None], p['w_router'][layer].astype(F32),
            p['b_router'][layer][None].astype(F32), tm)

        tiles = (cnt[0] + tm_moe - 1) // tm_moe
        tile_end = jnp.cumsum(tiles)
        group_start = (tile_end - tiles) * tm_moe
        pos = group_start[idx] + rank
        tok = jnp.broadcast_to(jnp.arange(t, dtype=jnp.int32)[:, None], (t, TOP_K))
        row_token = jnp.zeros((nt * tm_moe,), jnp.int32).at[pos.reshape(-1)].set(
            tok.reshape(-1), unique_indices=True)
        tile_ids = jnp.arange(nt, dtype=jnp.int32)
        tile_expert = jnp.minimum(jnp.searchsorted(tile_end, tile_ids, side='right'), N_EXPERTS - 1).astype(jnp.int32)
        tile_valid = (tile_ids < tile_end[-1]).astype(jnp.int32)

        ys = _experts(tile_expert, tile_valid, row_token, h, p['w_gate_up'][layer].astype(BF16),
                      p['b_gate_up'][layer][:, None].astype(F32), p['w_down'][layer].astype(BF16),
                      p['b_down'][layer][:, None].astype(F32), tm_moe)
        x = _combine(pos.astype(jnp.int32), ys, x1, wgt, p['final_norm'][None].astype(F32), min(tm, 128),
                     layer == depth - 1)
    return x


def _run(x_prompt, x_sample, p, tm=512, tm_moe=512, tc=64):
    bsz, seg_len, _ = x_prompt.shape
    assert bsz == N_SEG and x_sample.shape[0] == 1 and x_sample.shape[1] == N_SEG * seg_len
    assert seg_len % BLOCK == 0 and seg_len % tc == 0
    x = jnp.concatenate([x_prompt.reshape(-1, D_MODEL), x_sample.reshape(-1, D_MODEL)], axis=0).astype(F32)
    y = _trunk(x, bsz, seg_len, p, tm, tm_moe, tc)
    t_prompt = bsz * seg_len
    return y[:t_prompt].reshape(x_prompt.shape), y[t_prompt:].reshape(x_sample.shape)


def kernel(x_prompt, x_sample, norm_mix, w_in, ssm_lam_re, ssm_lam_im, ssm_log_step, ssm_b_re, ssm_b_im, ssm_c_re, ssm_c_im, ssm_d, attn_sink, rel_bias, sgu_norm, sgu_w, sgu_b, out_norm, w_out, norm_ffn, w_router, b_router, w_gate_up, b_gate_up, w_down, b_down, final_norm):
    p = dict(norm_mix=norm_mix, w_in=w_in, ssm_lam_re=ssm_lam_re, ssm_lam_im=ssm_lam_im, ssm_log_step=ssm_log_step,
             ssm_b_re=ssm_b_re, ssm_b_im=ssm_b_im, ssm_c_re=ssm_c_re, ssm_c_im=ssm_c_im, ssm_d=ssm_d,
             attn_sink=attn_sink, rel_bias=rel_bias, sgu_norm=sgu_norm, sgu_w=sgu_w, sgu_b=sgu_b, out_norm=out_norm,
             w_out=w_out, norm_ffn=norm_ffn, w_router=w_router, b_router=b_router, w_gate_up=w_gate_up,
             b_gate_up=b_gate_up, w_down=w_down, b_down=b_down, final_norm=final_norm)
    return _run(x_prompt, x_sample, p)
```

```python
import functools
import math

import jax
import jax.numpy as jnp
from jax import lax
from jax.experimental import pallas as pl
from jax.experimental.pallas import tpu as pltpu

D_MODEL = 1024
SSM_WIDTH = 256
SSM_GROUP = 16
SSM_GROUPS = 16
SSM_STATE = 64
SSM_COMPLEX = SSM_GROUPS * SSM_STATE
SSM_REAL = 2 * SSM_COMPLEX
ATTN_WIDTH = 512
HEAD_DIM = 64
N_Q_HEADS = 8
N_KV_HEADS = 2
Q_PER_KV = 4
KV_WIDTH = 128
BLOCK = 128
N_BUCKETS = 32
MAX_DISTANCE = 128
SGU_WIDTH = 256
SGU_GROUPS = 4
SGU_GROUP_WIDTH = 64
IN_COLS = 1792
QKV_COLS = ATTN_WIDTH + 2 * KV_WIDTH
N_EXPERTS = 32
TOP_K = 4
D_EXPERT = 1024
SWIGLU_LIMIT = 7.0
SWIGLU_ALPHA = 1.702
EPS = 1e-6
NEG_INF = -1e30

N_SEG = 8
V7X_VMEM_LIMIT = 56 * 1024 * 1024

F32 = jnp.float32
BF16 = jnp.bfloat16


def _rms(xf, g):
    return xf * lax.rsqrt(jnp.mean(xf * xf, axis=-1, keepdims=True) + EPS) * g


LANES = 128
TOKEN_TILE_ROWS = D_MODEL // LANES


def _load_token_tiles(ref, base, n):
    return jnp.concatenate(
        [ref[pl.ds(base + j, n, stride=TOKEN_TILE_ROWS), :] for j in range(TOKEN_TILE_ROWS)], axis=1)


def _store_token_tiles(ref, base, val, n):
    for j in range(TOKEN_TILE_ROWS):
        ref[pl.ds(base + j, n, stride=TOKEN_TILE_ROWS), :] = val[:, j * LANES:(j + 1) * LANES]


def _cparams(*sem):
    return pltpu.CompilerParams(dimension_semantics=sem, vmem_limit_bytes=V7X_VMEM_LIMIT)


def _inproj_kernel(x_ref, g_ref, w_ref, ssm_ref, qkv_ref, sgu_ref):
    h = _rms(x_ref[...], g_ref[...]).astype(BF16)
    p = jnp.dot(h, w_ref[...], preferred_element_type=F32)
    ssm_ref[...] = p[:, :2 * SSM_WIDTH].astype(BF16)
    qkv_ref[...] = p[:, 2 * SSM_WIDTH:2 * SSM_WIDTH + QKV_COLS].astype(BF16)
    sgu_ref[...] = p[:, 2 * SSM_WIDTH + QKV_COLS:].astype(BF16)


def _inproj(x, g, w, tm):
    t = x.shape[0]
    return pl.pallas_call(
        _inproj_kernel,
        grid=(t // tm,),
        in_specs=[
            pl.BlockSpec((tm, D_MODEL), lambda i: (i, 0)),
            pl.BlockSpec((1, D_MODEL), lambda i: (0, 0)),
            pl.BlockSpec((D_MODEL, IN_COLS), lambda i: (0, 0)),
        ],
        out_specs=[
            pl.BlockSpec((tm, 2 * SSM_WIDTH), lambda i: (i, 0)),
            pl.BlockSpec((tm, QKV_COLS), lambda i: (i, 0)),
            pl.BlockSpec((tm, 2 * SGU_WIDTH), lambda i: (i, 0)),
        ],
        out_shape=[
            jax.ShapeDtypeStruct((t, 2 * SSM_WIDTH), BF16),
            jax.ShapeDtypeStruct((t, QKV_COLS), BF16),
            jax.ShapeDtypeStruct((t, 2 * SGU_WIDTH), BF16),
        ],
        compiler_params=_cparams("arbitrary"),
        name="inproj",
    )(x, g, w)


SCAN_COL_BLOCKS = 2


def _ssm_kernel(u_ref, b_ref, a_ref, c_ref, init_ref, *rest, tc, emit_y):
    if emit_y:
        y_ref, fin_ref, bu_scr, st_scr = rest
    else:
        fin_ref, bu_scr, st_scr = rest
    d = pl.program_id(0)
    c = pl.program_id(1)

    @pl.when(c == 0)
    def _():
        st_scr[...] = init_ref[0]

    bu_scr[...] = jnp.dot(u_ref[...], b_ref[0], preferred_element_type=F32)
    w = SSM_COMPLEX // SCAN_COL_BLOCKS
    for cb in range(SCAN_COL_BLOCKS):
        re_cols = slice(cb * w, (cb + 1) * w)
        im_cols = slice(SSM_COMPLEX + cb * w, SSM_COMPLEX + (cb + 1) * w)
        a_re = jnp.broadcast_to(a_ref[0, 0:1, re_cols], (N_SEG, w))
        a_im = jnp.broadcast_to(a_ref[0, 1:2, re_cols], (N_SEG, w))

        def body(tt, carry, re_cols=re_cols, im_cols=im_cols, a_re=a_re, a_im=a_im):
            x_re, x_im = carry
            t = jnp.where(d == 0, tt, tc - 1 - tt)
            rows = pl.ds(pl.multiple_of(t * N_SEG, N_SEG), N_SEG)
            n_re = a_re * x_re - a_im * x_im + bu_scr[rows, re_cols]
            n_im = a_re * x_im + a_im * x_re + bu_scr[rows, im_cols]
            bu_scr[rows, re_cols] = n_re
            bu_scr[rows, im_cols] = n_im
            return n_re, n_im

        x_re, x_im = lax.fori_loop(0, tc, body, (st_scr[:, re_cols], st_scr[:, im_cols]), unroll=4)
        st_scr[:, re_cols] = x_re
        st_scr[:, im_cols] = x_im

    if emit_y:
        y_ref[0] = jnp.dot(bu_scr[...].astype(BF16), c_ref[...], preferred_element_type=F32)

    @pl.when(c == pl.num_programs(1) - 1)
    def _():
        fin_ref[0] = st_scr[...]


def _ssm_scan(u_tm, b_blk, a_rows, c_blk, init, tc, emit_y):
    rows = u_tm.shape[0]
    rb = tc * N_SEG
    nc = rows // rb

    def chunk(d, c):
        return jnp.where(d == 0, c, nc - 1 - c)

    out_specs = [pl.BlockSpec((1, N_SEG, SSM_REAL), lambda d, c: (d, 0, 0))]
    out_shape = [jax.ShapeDtypeStruct((2, N_SEG, SSM_REAL), F32)]
    if emit_y:
        out_specs = [pl.BlockSpec((1, rb, SSM_WIDTH), lambda d, c: (d, chunk(d, c), 0))] + out_specs
        out_shape = [jax.ShapeDtypeStruct((2, rows, SSM_WIDTH), F32)] + out_shape
    res = pl.pallas_call(
        functools.partial(_ssm_kernel, tc=tc, emit_y=emit_y),
        grid=(2, nc),
        in_specs=[
            pl.BlockSpec((rb, SSM_WIDTH), lambda d, c: (chunk(d, c), 0)),
            pl.BlockSpec((1, SSM_WIDTH, SSM_REAL), lambda d, c: (d, 0, 0)),
            pl.BlockSpec((1, 2, SSM_COMPLEX), lambda d, c: (d, 0, 0)),
            pl.BlockSpec((SSM_REAL, SSM_WIDTH), lambda d, c: (0, 0)),
            pl.BlockSpec((1, N_SEG, SSM_REAL), lambda d, c: (d, 0, 0)),
        ],
        out_specs=out_specs,
        out_shape=out_shape,
        scratch_shapes=[pltpu.VMEM((rb, SSM_REAL), F32), pltpu.VMEM((N_SEG, SSM_REAL), F32)],
        compiler_params=_cparams("arbitrary", "arbitrary"),
        name="ssm_scan_y" if emit_y else "ssm_scan_state",
    )(u_tm, b_blk, a_rows, c_blk, init)
    return (res[0], res[1]) if emit_y else (None, res[0])


def _ssm_params(lam_re, lam_im, log_step, b_re, b_im, c_re, c_im):
    lam = lax.complex(lam_re.astype(F32), lam_im.astype(F32))
    step = jnp.exp(log_step.astype(F32))[..., None]
    lam_bar = jnp.exp(lam * step)
    b = lax.complex(b_re.astype(F32), b_im.astype(F32))
    b_bar = ((lam_bar - 1.0) / lam)[..., None] * b
    eye = jnp.eye(SSM_GROUPS, dtype=F32)
    b_blk_re = jnp.einsum('dgph,gk->dghkp', jnp.real(b_bar), eye).reshape(2, SSM_WIDTH, SSM_COMPLEX)
    b_blk_im = jnp.einsum('dgph,gk->dghkp', jnp.imag(b_bar), eye).reshape(2, SSM_WIDTH, SSM_COMPLEX)
    b_blk = jnp.concatenate([b_blk_re, b_blk_im], axis=-1).astype(BF16)
    c_blk_re = jnp.einsum('ghp,gk->gpkh', c_re.astype(F32), eye).reshape(SSM_COMPLEX, SSM_WIDTH)
    c_blk_im = jnp.einsum('ghp,gk->gpkh', c_im.astype(F32), eye).reshape(SSM_COMPLEX, SSM_WIDTH)
    c_blk = jnp.concatenate([c_blk_re, -c_blk_im], axis=0).astype(BF16)
    a_rows = jnp.stack([jnp.real(lam_bar).reshape(2, SSM_COMPLEX),
                        jnp.imag(lam_bar).reshape(2, SSM_COMPLEX)], axis=1)
    return lam_bar.reshape(2, SSM_COMPLEX), b_blk, a_rows, c_blk


def _to_time_major(a, seg_len):
    n = a.shape[0] // seg_len
    return a.reshape(n // N_SEG, N_SEG, seg_len, a.shape[-1]).transpose(0, 2, 1, 3).reshape(a.shape)


def _from_time_major(a, seg_len):
    n = a.shape[0] // seg_len
    return a.reshape(n // N_SEG, seg_len, N_SEG, a.shape[-1]).transpose(0, 2, 1, 3).reshape(a.shape)


def _chain_init(fin, lam_pow):
    f = lax.complex(fin[..., :SSM_COMPLEX], fin[..., SSM_COMPLEX:])
    zero = jnp.zeros_like(f[0, 0])
    fwd = [zero]
    for s in range(1, N_SEG):
        fwd.append(lam_pow[0] * fwd[-1] + f[0, s - 1])
    bwd = [zero]
    for s in range(N_SEG - 2, -1, -1):
        bwd.append(lam_pow[1] * bwd[-1] + f[1, s + 1])
    init = jnp.stack([jnp.stack(fwd), jnp.stack(bwd[::-1])])
    return jnp.concatenate([jnp.real(init), jnp.imag(init)], axis=-1).astype(F32)


def _attn_sgu_kernel(flags_ref, q_ref, kp_ref, kc_ref, kn_ref, vp_ref, vc_ref, vn_ref, bias_ref, sink_ref,
                     sgu_ref, sgn_ref, sgw_ref, sgb_ref, attn_ref, sguo_ref):
    n = pl.program_id(0)
    has_prev = flags_ref[0, n] > 0
    has_next = flags_ref[1, n] > 0
    q = q_ref[...]
    ks = (kp_ref[...], kc_ref[...], kn_ref[...])
    vs = (vp_ref[...], vc_ref[...], vn_ref[...])
    outs = []
    for j in range(N_KV_HEADS):
        qs = jnp.concatenate(
            [q[:, (j * Q_PER_KV + i) * HEAD_DIM:(j * Q_PER_KV + i + 1) * HEAD_DIM] for i in range(Q_PER_KV)], axis=0)
        kv_cols = slice(j * HEAD_DIM, (j + 1) * HEAD_DIM)
        s = []
        for b in range(3):
            sb = lax.dot_general(qs, ks[b][:, kv_cols], (((1,), (1,)), ((), ())), preferred_element_type=F32)
            sb = sb * (HEAD_DIM ** -0.5) + bias_ref[j, :, b * BLOCK:(b + 1) * BLOCK]
            if b == 0:
                sb = jnp.where(has_prev, sb, NEG_INF)
            if b == 2:
                sb = jnp.where(has_next, sb, NEG_INF)
            s.append(sb)
        sink = sink_ref[j]
        m = jnp.max(jnp.maximum(jnp.maximum(s[0], s[1]), s[2]), axis=-1, keepdims=True)
        m = jnp.maximum(m, sink)
        p = [jnp.exp(sb - m) for sb in s]
        denom = jnp.sum(p[0] + p[1] + p[2], axis=-1, keepdims=True) + jnp.exp(sink - m)
        o = sum(jnp.dot(p[b].astype(BF16), vs[b][:, kv_cols], preferred_element_type=F32) for b in range(3))
        o = o / denom
        outs += [o[i * BLOCK:(i + 1) * BLOCK] for i in range(Q_PER_KV)]
    attn_ref[...] = jnp.concatenate(outs, axis=1).astype(BF16)

    gu = jax.nn.gelu(sgu_ref[:, :SGU_WIDTH].astype(F32))
    gv = _rms(jax.nn.gelu(sgu_ref[:, SGU_WIDTH:].astype(F32)), sgn_ref[...]).astype(BF16)
    mixed = jnp.concatenate(
        [jnp.dot(sgw_ref[g], gv[:, g * SGU_GROUP_WIDTH:(g + 1) * SGU_GROUP_WIDTH], preferred_element_type=F32)
         for g in range(SGU_GROUPS)], axis=1) + sgb_ref[...]
    sguo_ref[...] = (gu * mixed).astype(BF16)


def _attn_sgu(flags, qkv, sgu, bias, sink, sgn, sgw, sgb):
    t = qkv.shape[0]
    nb = t // BLOCK
    kcol = ATTN_WIDTH // KV_WIDTH
    vcol = kcol + 1

    def blk(col, off):
        return pl.BlockSpec((BLOCK, KV_WIDTH), lambda n, f: (jnp.clip(n + off, 0, nb - 1), col))

    grid_spec = pltpu.PrefetchScalarGridSpec(
        num_scalar_prefetch=1,
        grid=(nb,),
        in_specs=[
            pl.BlockSpec((BLOCK, ATTN_WIDTH), lambda n, f: (n, 0)),
            blk(kcol, -1), blk(kcol, 0), blk(kcol, 1),
            blk(vcol, -1), blk(vcol, 0), blk(vcol, 1),
            pl.BlockSpec((N_KV_HEADS, Q_PER_KV * BLOCK, 3 * BLOCK), lambda n, f: (0, 0, 0)),
            pl.BlockSpec((N_KV_HEADS, Q_PER_KV * BLOCK, 1), lambda n, f: (0, 0, 0)),
            pl.BlockSpec((BLOCK, 2 * SGU_WIDTH), lambda n, f: (n, 0)),
            pl.BlockSpec((1, SGU_WIDTH), lambda n, f: (0, 0)),
            pl.BlockSpec((SGU_GROUPS, BLOCK, BLOCK), lambda n, f: (0, 0, 0)),
            pl.BlockSpec((BLOCK, SGU_WIDTH), lambda n, f: (0, 0)),
        ],
        out_specs=[
            pl.BlockSpec((BLOCK, ATTN_WIDTH), lambda n, f: (n, 0)),
            pl.BlockSpec((BLOCK, SGU_WIDTH), lambda n, f: (n, 0)),
        ],
    )
    return pl.pallas_call(
        _attn_sgu_kernel,
        grid_spec=grid_spec,
        out_shape=[jax.ShapeDtypeStruct((t, ATTN_WIDTH), BF16), jax.ShapeDtypeStruct((t, SGU_WIDTH), BF16)],
        compiler_params=_cparams("arbitrary"),
        name="attn_sgu",
    )(flags, qkv, qkv, qkv, qkv, qkv, qkv, qkv, bias, sink, sgu, sgn, sgw, sgb)


def _t5_bucket(rel):
    half = N_BUCKETS // 2
    max_exact = half // 2
    ret = jnp.where(rel > 0, half, 0)
    n = jnp.abs(rel)
    nf = jnp.maximum(n, 1).astype(F32)
    large = max_exact + (jnp.log(nf / max_exact) / math.log(MAX_DISTANCE / max_exact)
                         * (half - max_exact)).astype(jnp.int32)
    large = jnp.minimum(large, half - 1)
    return ret + jnp.where(n < max_exact, n, large)


def _band_bias(rel_table):
    q_pos = jnp.arange(BLOCK)[:, None]
    k_pos = jnp.arange(3 * BLOCK)[None, :] - BLOCK
    rel = k_pos - q_pos
    bias = rel_table.astype(F32)[_t5_bucket(rel)]
    bias = jnp.where((jnp.abs(rel) <= BLOCK)[..., None], bias, NEG_INF)
    return bias.transpose(2, 0, 1).reshape(N_KV_HEADS, Q_PER_KV * BLOCK, 3 * BLOCK)


def _outproj_router_kernel(x_ref, ssm_ref, yf_ref, yb_ref, attn_ref, sguo_ref, d_ref, gout_ref, wout_ref,
                           gffn_ref, wr_ref, br_ref, x1_ref, h_ref, idx_ref, rank_ref, wgt_ref, cnt_ref,
                           run_scr, *, tm):
    i = pl.program_id(0)

    @pl.when(i == 0)
    def _():
        run_scr[...] = jnp.zeros_like(run_scr)

    u = ssm_ref[:, :SSM_WIDTH].astype(F32)
    gate = ssm_ref[:, SSM_WIDTH:].astype(F32)
    ya = yf_ref[...] + yb_ref[...] + d_ref[...] * u
    ya = jax.nn.gelu(ya) * jax.nn.sigmoid(gate)
    g = gout_ref[...]
    mixed = jnp.concatenate([
        _rms(ya, g[:, :SSM_WIDTH]),
        _rms(attn_ref[...].astype(F32), g[:, SSM_WIDTH:SSM_WIDTH + ATTN_WIDTH]),
        _rms(sguo_ref[...].astype(F32), g[:, SSM_WIDTH + ATTN_WIDTH:]),
    ], axis=1).astype(BF16)
    x1 = x_ref[...] + jnp.dot(mixed, wout_ref[...], preferred_element_type=F32)
    x1_ref[...] = x1
    h = _rms(x1, gffn_ref[...])
    _store_token_tiles(h_ref, 0, h, tm)

    logits = jnp.dot(h, wr_ref[...], preferred_element_type=F32, precision=lax.Precision.HIGHEST) + br_ref[...]
    lane = lax.broadcasted_iota(jnp.int32, (tm, N_EXPERTS), 1)
    vals, idxs, sels = [], [], []
    lg = logits
    for _ in range(TOP_K):
        m = jnp.max(lg, axis=-1, keepdims=True)
        idx = jnp.min(jnp.where(lg == m, lane, N_EXPERTS), axis=-1, keepdims=True)
        sel = lane == idx
        lg = jnp.where(sel, -jnp.inf, lg)
        vals.append(m)
        idxs.append(idx)
        sels.append(sel)
    e = [jnp.exp(v - vals[0]) for v in vals]
    tot = e[0] + e[1] + e[2] + e[3]
    wgt_ref[...] = jnp.concatenate([ek / tot for ek in e], axis=1)
    idx_ref[...] = jnp.concatenate(idxs, axis=1)

    picked = (sels[0] | sels[1] | sels[2] | sels[3])
    pm = jnp.where(picked, 1.0, 0.0)
    r_io = lax.broadcasted_iota(jnp.int32, (tm, tm), 0)
    c_io = lax.broadcasted_iota(jnp.int32, (tm, tm), 1)
    tri = jnp.where(c_io < r_io, 1.0, 0.0).astype(BF16)
    before = jnp.dot(tri, pm.astype(BF16), preferred_element_type=F32) + run_scr[...]
    rank_ref[...] = jnp.concatenate(
        [jnp.sum(jnp.where(s, before, 0.0), axis=-1, keepdims=True) for s in sels], axis=1).astype(jnp.int32)
    run_scr[...] = run_scr[...] + jnp.sum(pm, axis=0, keepdims=True)

    @pl.when(i == pl.num_programs(0) - 1)
    def _():
        cnt_ref[...] = run_scr[...].astype(jnp.int32)


def _outproj_router(x, ssm, yf, yb, attn, sguo, d_skip, g_out, w_out, g_ffn, w_r, b_r, tm):
    t = x.shape[0]
    row = lambda w: pl.BlockSpec((tm, w), lambda i: (i, 0))
    full = lambda a, b: pl.BlockSpec((a, b), lambda i: (0, 0))
    return pl.pallas_call(
        functools.partial(_outproj_router_kernel, tm=tm),
        grid=(t // tm,),
        in_specs=[
            row(D_MODEL), row(2 * SSM_WIDTH), row(SSM_WIDTH), row(SSM_WIDTH), row(ATTN_WIDTH), row(SGU_WIDTH),
            full(1, SSM_WIDTH), full(1, D_MODEL), full(D_MODEL, D_MODEL), full(1, D_MODEL),
            full(D_MODEL, N_EXPERTS), full(1, N_EXPERTS),
        ],
        out_specs=[row(D_MODEL), pl.BlockSpec((tm * TOKEN_TILE_ROWS, LANES), lambda i: (i, 0)),
                   row(TOP_K), row(TOP_K), row(TOP_K), full(1, N_EXPERTS)],
        out_shape=[
            jax.ShapeDtypeStruct((t, D_MODEL), F32),
            jax.ShapeDtypeStruct((t * TOKEN_TILE_ROWS, LANES), F32),
            jax.ShapeDtypeStruct((t, TOP_K), jnp.int32),
            jax.ShapeDtypeStruct((t, TOP_K), jnp.int32),
            jax.ShapeDtypeStruct((t, TOP_K), F32),
            jax.ShapeDtypeStruct((1, N_EXPERTS), jnp.int32),
        ],
        scratch_shapes=[pltpu.VMEM((1, N_EXPERTS), F32)],
        compiler_params=_cparams("arbitrary"),
        name="outproj_router",
    )(x, ssm, yf, yb, attn, sguo, d_skip, g_out, w_out, g_ffn, w_r, b_r)


ISSUE_UNROLL = 8


def _for_rows(n, fn):
    def body(c, carry):
        for u in range(ISSUE_UNROLL):
            fn(c * ISSUE_UNROLL + u)
        return carry

    lax.fori_loop(0, n // ISSUE_UNROLL, body, 0)


def _tile_rows(ref, first_tile, n_tiles=1):
    start = first_tile * TOKEN_TILE_ROWS
    if not isinstance(start, int):
        start = pl.multiple_of(start, TOKEN_TILE_ROWS)
    return ref.at[pl.ds(start, n_tiles * TOKEN_TILE_ROWS)]


def _experts_kernel(tile_ref, exp_ref, lo_ref, hi_ref, tok_ref, tokn_ref, dst_ref, h_hbm, wgu_ref, bgu_ref,
                    wd_ref, bd_ref, out_hbm, x_buf, y_buf, gsem, ssem, *, tm, spare_row):
    w = pl.program_id(0)
    nw = pl.num_programs(0)
    slot = lax.rem(w, 2)
    lo = lo_ref[w]
    hi = hi_ref[w]

    def gather(list_ref, to_slot):
        def one(r):
            pltpu.make_async_copy(_tile_rows(h_hbm, list_ref[0, 0, r]), _tile_rows(x_buf, to_slot * tm + r),
                                  gsem.at[to_slot]).start()
        _for_rows(tm, one)

    def wait_gather(of_slot):
        pltpu.make_async_copy(_tile_rows(h_hbm, 0, tm), _tile_rows(x_buf, of_slot * tm, tm), gsem.at[of_slot]).wait()

    def wait_scatter(of_slot):
        pltpu.make_async_copy(_tile_rows(y_buf, of_slot * tm, tm), _tile_rows(out_hbm, 0, tm), ssem.at[of_slot]).wait()

    @pl.when(w == 0)
    def _():
        gather(tok_ref, 0)

    @pl.when(w + 1 < nw)
    def _():
        gather(tokn_ref, 1 - slot)

    wait_gather(slot)

    @pl.when(w >= 2)
    def _():
        wait_scatter(slot)

    base = pl.multiple_of(slot * (tm * TOKEN_TILE_ROWS), tm * TOKEN_TILE_ROWS)
    x = _load_token_tiles(x_buf, base, tm).astype(BF16)
    gu = jnp.dot(x, wgu_ref[0], preferred_element_type=F32) + bgu_ref[0]
    gate = jnp.minimum(gu[:, :D_EXPERT], SWIGLU_LIMIT)
    up = jnp.clip(gu[:, D_EXPERT:], -SWIGLU_LIMIT, SWIGLU_LIMIT)
    act = (up + 1.0) * gate * jax.nn.sigmoid(SWIGLU_ALPHA * gate)
    y = jnp.dot(act.astype(BF16), wd_ref[0], preferred_element_type=F32) + bd_ref[0]
    _store_token_tiles(y_buf, base, y, tm)

    def scatter_one(r):
        dst = jnp.where((r >= lo) & (r < hi), dst_ref[0, 0, r], spare_row + r)
        pltpu.make_async_copy(_tile_rows(y_buf, slot * tm + r), _tile_rows(out_hbm, dst), ssem.at[slot]).start()

    _for_rows(tm, scatter_one)

    @pl.when(w == nw - 1)
    def _():
        wait_scatter(1 - slot)
        wait_scatter(slot)


def _experts(item_tile, item_expert, item_lo, item_hi, tok_sorted, dst_sorted, h_tiles, wgu, bgu, wd, bd, tm):
    nw = item_tile.shape[0]
    nt = tok_sorted.shape[0] // tm
    n_rows = tok_sorted.shape[0]
    lists = lambda a: a.reshape(nt, 1, tm)
    cur = lambda i, ti, ex, lo, hi: (ti[i], 0, 0)
    nxt = lambda i, ti, ex, lo, hi: (ti[jnp.minimum(i + 1, nw - 1)], 0, 0)
    by_expert = lambda i, ti, ex, lo, hi: (ex[i], 0, 0)
    grid_spec = pltpu.PrefetchScalarGridSpec(
        num_scalar_prefetch=4,
        grid=(nw,),
        in_specs=[
            pl.BlockSpec((1, 1, tm), cur, memory_space=pltpu.SMEM),
            pl.BlockSpec((1, 1, tm), nxt, memory_space=pltpu.SMEM),
            pl.BlockSpec((1, 1, tm), cur, memory_space=pltpu.SMEM),
            pl.BlockSpec(memory_space=pl.ANY),
            pl.BlockSpec((1, D_MODEL, 2 * D_EXPERT), by_expert),
            pl.BlockSpec((1, 1, 2 * D_EXPERT), by_expert),
            pl.BlockSpec((1, D_EXPERT, D_MODEL), by_expert),
            pl.BlockSpec((1, 1, D_MODEL), by_expert),
        ],
        out_specs=pl.BlockSpec(memory_space=pl.ANY),
        scratch_shapes=[
            pltpu.VMEM((2 * tm * TOKEN_TILE_ROWS, LANES), F32),
            pltpu.VMEM((2 * tm * TOKEN_TILE_ROWS, LANES), F32),
            pltpu.SemaphoreType.DMA((2,)),
            pltpu.SemaphoreType.DMA((2,)),
        ],
    )
    return pl.pallas_call(
        functools.partial(_experts_kernel, tm=tm, spare_row=n_rows),
        grid_spec=grid_spec,
        out_shape=jax.ShapeDtypeStruct(((n_rows + tm) * TOKEN_TILE_ROWS, LANES), F32),
        compiler_params=_cparams("arbitrary"),
        name="experts",
    )(item_tile, item_expert, item_lo, item_hi, lists(tok_sorted), lists(tok_sorted), lists(dst_sorted),
      h_tiles, wgu, bgu, wd, bd)


def _combine_kernel(y0_ref, y1_ref, y2_ref, y3_ref, x1_ref, wgt_ref, g_ref, out_ref, *, tm, final):
    w = wgt_ref[...]
    x2 = x1_ref[...]
    for k, y_ref in enumerate((y0_ref, y1_ref, y2_ref, y3_ref)):
        x2 = x2 + _load_token_tiles(y_ref, 0, tm) * w[:, k:k + 1]
    if final:
        x2 = _rms(x2, g_ref[...])
    out_ref[...] = x2


def _combine(y_tiles, x1, wgt, g_final, tm, final):
    t = x1.shape[0]
    nb = t // tm
    choice = lambda k: pl.BlockSpec((tm * TOKEN_TILE_ROWS, LANES), lambda i: (k * nb + i, 0))
    return pl.pallas_call(
        functools.partial(_combine_kernel, tm=tm, final=final),
        grid=(nb,),
        in_specs=[
            choice(0), choice(1), choice(2), choice(3),
            pl.BlockSpec((tm, D_MODEL), lambda i: (i, 0)),
            pl.BlockSpec((tm, TOP_K), lambda i: (i, 0)),
            pl.BlockSpec((1, D_MODEL), lambda i: (0, 0)),
        ],
        out_specs=pl.BlockSpec((tm, D_MODEL), lambda i: (i, 0)),
        out_shape=jax.ShapeDtypeStruct((t, D_MODEL), F32),
        compiler_params=_cparams("arbitrary"),
        name="combine",
    )(y_tiles, y_tiles, y_tiles, y_tiles, x1, wgt, g_final)


def _routing_lists(idx, rank, cnt, tm):
    t = idx.shape[0]
    n_rows = t * TOP_K
    nt = n_rows // tm
    group_end = jnp.cumsum(cnt)
    group_start = group_end - cnt
    experts = jnp.arange(N_EXPERTS, dtype=jnp.int32)
    pos = rank + jnp.sum(jnp.where(idx[..., None] == experts, group_start, 0), axis=-1)
    flat = jnp.arange(n_rows, dtype=jnp.int32)
    _, flat_sorted = lax.sort_key_val(pos.reshape(-1).astype(jnp.int32), flat)
    tok_sorted = flat_sorted // TOP_K
    dst_sorted = (flat_sorted % TOP_K) * t + tok_sorted
    bounds = jnp.sort(jnp.concatenate([jnp.arange(nt, dtype=jnp.int32) * tm, group_start.astype(jnp.int32)]))
    nxt = jnp.concatenate([bounds[1:], jnp.array([n_rows], jnp.int32)])
    tile = jnp.minimum(bounds // tm, nt - 1)
    lo = bounds - tile * tm
    hi = jnp.where(nxt > bounds, nxt - tile * tm, lo)
    expert = jnp.minimum(jnp.sum(group_end[None, :] <= bounds[:, None], axis=-1), N_EXPERTS - 1)
    items = tuple(a.astype(jnp.int32) for a in (tile, expert, lo, hi))
    return items, tok_sorted, dst_sorted


def _block_flags(n_prompt_seq, seg_len, n_tokens):
    nb = n_tokens // BLOCK
    per_seg = seg_len // BLOCK
    b = jnp.arange(nb)
    prompt_blocks = n_prompt_seq * per_seg
    in_prompt = b < prompt_blocks
    first = jnp.where(in_prompt, b % per_seg == 0, b == prompt_blocks)
    last = jnp.where(in_prompt, b % per_seg == per_seg - 1, b == nb - 1)
    return jnp.stack([~first, ~last]).astype(jnp.int32)


def _trunk(x, n_prompt_seq, seg_len, p, tm, tm_moe, tc):
    t = x.shape[0]
    t_prompt = n_prompt_seq * seg_len
    depth = p['w_in'].shape[0]
    flags = _block_flags(n_prompt_seq, seg_len, t)
    bias = _band_bias(p['rel_bias'])
    nt = (t * TOP_K) // tm_moe + N_EXPERTS
    zero_init = jnp.zeros((2, N_SEG, SSM_REAL), F32)
    for layer in range(depth):
        ssm, qkv, sgu = _inproj(x, p['norm_mix'][layer][None], p['w_in'][layer].astype(BF16), tm)

        lam_bar, b_blk, a_rows, c_blk = _ssm_params(
            p['ssm_lam_re'][layer], p['ssm_lam_im'][layer], p['ssm_log_step'][layer], p['ssm_b_re'][layer],
            p['ssm_b_im'][layer], p['ssm_c_re'][layer], p['ssm_c_im'][layer])
        u_tm = _to_time_major(ssm[:, :SSM_WIDTH], seg_len)
        y_p, _ = _ssm_scan(u_tm[:t_prompt], b_blk, a_rows, c_blk, zero_init, tc, True)
        _, fin = _ssm_scan(u_tm[t_prompt:], b_blk, a_rows, c_blk, zero_init, tc, False)
        init = _chain_init(fin, lam_bar ** seg_len)
        y_s, _ = _ssm_scan(u_tm[t_prompt:], b_blk, a_rows, c_blk, init, tc, True)
        y_tm = jnp.concatenate([y_p, y_s], axis=1)
        yf = _from_time_major(y_tm[0], seg_len)
        yb = _from_time_major(y_tm[1], seg_len)

        sink = jnp.repeat(p['attn_sink'][layer].astype(F32), BLOCK).reshape(N_KV_HEADS, Q_PER_KV * BLOCK, 1)
        sgb = jnp.repeat(p['sgu_b'][layer].astype(F32).T, SGU_GROUP_WIDTH, axis=1)
        attn, sguo = _attn_sgu(flags, qkv, sgu, bias, sink, p['sgu_norm'][layer][None].astype(F32),
                               p['sgu_w'][layer].astype(BF16), sgb)

        x1, h, idx, rank, wgt, cnt = _outproj_router(
            x, ssm, yf, yb, attn, sguo, p['ssm_d'][layer][None].astype(F32), p['out_norm'][layer][None],
            p['w_out'][layer].astype(BF16), p['norm_ffn'][layer][None], p['w_router'][layer].astype(F32),
            p['b_router'][layer][None].astype(F32), tm)

        items, tok_sorted, dst_sorted = _routing_lists(idx, rank, cnt[0], tm_moe)
        y_tiles = _experts(*items, tok_sorted, dst_sorted, h, p['w_gate_up'][layer].astype(BF16),
                           p['b_gate_up'][layer][:, None].astype(F32), p['w_down'][layer].astype(BF16),
                           p['b_down'][layer][:, None].astype(F32), tm_moe)
        x = _combine(y_tiles, x1, wgt, p['final_norm'][None].astype(F32), min(tm, 256), layer == depth - 1)
    return x


def _run(x_prompt, x_sample, p, tm=512, tm_moe=512, tc=64):
    bsz, seg_len, _ = x_prompt.shape
    assert bsz == N_SEG and x_sample.shape[0] == 1 and x_sample.shape[1] == N_SEG * seg_len
    assert seg_len % BLOCK == 0 and seg_len % tc == 0
    x = jnp.concatenate([x_prompt.reshape(-1, D_MODEL), x_sample.reshape(-1, D_MODEL)], axis=0).astype(F32)
    y = _trunk(x, bsz, seg_len, p, tm, tm_moe, tc)
    t_prompt = bsz * seg_len
    return y[:t_prompt].reshape(x_prompt.shape), y[t_prompt:].reshape(x_sample.shape)


def kernel(x_prompt, x_sample, norm_mix, w_in, ssm_lam_re, ssm_lam_im, ssm_log_step, ssm_b_re, ssm_b_im, ssm_c_re, ssm_c_im, ssm_d, attn_sink, rel_bias, sgu_norm, sgu_w, sgu_b, out_norm, w_out, norm_ffn, w_router, b_router, w_gate_up, b_gate_up, w_down, b_down, final_norm):
    p = dict(norm_mix=norm_mix, w_in=w_in, ssm_lam_re=ssm_lam_re, ssm_lam_im=ssm_lam_im, ssm_log_step=ssm_log_step,
             ssm_b_re=ssm_b_re, ssm_b_im=ssm_b_im, ssm_c_re=ssm_c_re, ssm_c_im=ssm_c_im, ssm_d=ssm_d,
             attn_sink=attn_sink, rel_bias=rel_bias, sgu_norm=sgu_norm, sgu_w=sgu_w, sgu_b=sgu_b, out_norm=out_norm,
             w_out=w_out, norm_ffn=norm_ffn, w_router=w_router, b_router=b_router, w_gate_up=w_gate_up,
             b_gate_up=b_gate_up, w_down=w_down, b_down=b_down, final_norm=final_norm)
    return _run(x_prompt, x_sample, p)
```

```python
import functools
import math

import jax
import jax.numpy as jnp
from jax import lax
from jax.experimental import pallas as pl
from jax.experimental.pallas import tpu as pltpu

D_MODEL = 1024
SSM_WIDTH = 256
SSM_GROUP = 16
SSM_GROUPS = 16
SSM_STATE = 64
SSM_COMPLEX = SSM_GROUPS * SSM_STATE
SSM_REAL = 2 * SSM_COMPLEX
ATTN_WIDTH = 512
HEAD_DIM = 64
N_Q_HEADS = 8
N_KV_HEADS = 2
Q_PER_KV = 4
KV_WIDTH = 128
BLOCK = 128
N_BUCKETS = 32
MAX_DISTANCE = 128
SGU_WIDTH = 256
SGU_GROUPS = 4
SGU_GROUP_WIDTH = 64
IN_COLS = 1792
QKV_COLS = ATTN_WIDTH + 2 * KV_WIDTH
N_EXPERTS = 32
TOP_K = 4
D_EXPERT = 1024
SWIGLU_LIMIT = 7.0
SWIGLU_ALPHA = 1.702
EPS = 1e-6
NEG_INF = -1e30

N_SEG = 8
V7X_VMEM_LIMIT = 56 * 1024 * 1024

F32 = jnp.float32
BF16 = jnp.bfloat16


def _rms(xf, g):
    return xf * lax.rsqrt(jnp.mean(xf * xf, axis=-1, keepdims=True) + EPS) * g


LANES = 128
TOKEN_TILE_ROWS = D_MODEL // LANES


def _load_token_tiles(ref, base, n):
    return jnp.concatenate(
        [ref[pl.ds(base + j, n, stride=TOKEN_TILE_ROWS), :] for j in range(TOKEN_TILE_ROWS)], axis=1)


def _store_token_tiles(ref, base, val, n):
    for j in range(TOKEN_TILE_ROWS):
        ref[pl.ds(base + j, n, stride=TOKEN_TILE_ROWS), :] = val[:, j * LANES:(j + 1) * LANES]


def _cparams(*sem):
    return pltpu.CompilerParams(dimension_semantics=sem, vmem_limit_bytes=V7X_VMEM_LIMIT)


def _inproj_kernel(x_ref, g_ref, w_ref, ssm_ref, qkv_ref, sgu_ref):
    h = _rms(x_ref[...], g_ref[...]).astype(BF16)
    p = jnp.dot(h, w_ref[...], preferred_element_type=F32)
    ssm_ref[...] = p[:, :2 * SSM_WIDTH].astype(BF16)
    qkv_ref[...] = p[:, 2 * SSM_WIDTH:2 * SSM_WIDTH + QKV_COLS].astype(BF16)
    sgu_ref[...] = p[:, 2 * SSM_WIDTH + QKV_COLS:].astype(BF16)


def _inproj(x, g, w, tm):
    t = x.shape[0]
    return pl.pallas_call(
        _inproj_kernel,
        grid=(t // tm,),
        in_specs=[
            pl.BlockSpec((tm, D_MODEL), lambda i: (i, 0)),
            pl.BlockSpec((1, D_MODEL), lambda i: (0, 0)),
            pl.BlockSpec((D_MODEL, IN_COLS), lambda i: (0, 0)),
        ],
        out_specs=[
            pl.BlockSpec((tm, 2 * SSM_WIDTH), lambda i: (i, 0)),
            pl.BlockSpec((tm, QKV_COLS), lambda i: (i, 0)),
            pl.BlockSpec((tm, 2 * SGU_WIDTH), lambda i: (i, 0)),
        ],
        out_shape=[
            jax.ShapeDtypeStruct((t, 2 * SSM_WIDTH), BF16),
            jax.ShapeDtypeStruct((t, QKV_COLS), BF16),
            jax.ShapeDtypeStruct((t, 2 * SGU_WIDTH), BF16),
        ],
        compiler_params=_cparams("arbitrary"),
        name="inproj",
    )(x, g, w)


SCAN_COL_BLOCKS = 2


def _ssm_kernel(u_ref, b_ref, a_ref, c_ref, init_ref, *rest, tc, emit_y):
    if emit_y:
        y_ref, fin_ref, bu_scr, st_scr, tm_lo, tm_hi = rest
    else:
        fin_ref, bu_scr, st_scr, tm_lo, tm_hi = rest
    d = pl.program_id(0)
    c = pl.program_id(2)
    halves = ((tm_lo, slice(0, LANES)), (tm_hi, slice(LANES, 2 * LANES)))

    @pl.when(c == 0)
    def _():
        st_scr[...] = init_ref[0, 0]

    for s in range(N_SEG):
        u_s = u_ref[s].astype(F32)
        for scr, cols in halves:
            scr[pl.ds(s, tc, stride=N_SEG), :] = u_s[:, cols]
    u_tm = jnp.concatenate([tm_lo[...], tm_hi[...]], axis=1).astype(BF16)
    bu_scr[...] = jnp.dot(u_tm, b_ref[0], preferred_element_type=F32)
    w = SSM_COMPLEX // SCAN_COL_BLOCKS
    for cb in range(SCAN_COL_BLOCKS):
        re_cols = slice(cb * w, (cb + 1) * w)
        im_cols = slice(SSM_COMPLEX + cb * w, SSM_COMPLEX + (cb + 1) * w)
        a_re = jnp.broadcast_to(a_ref[0, 0:1, re_cols], (N_SEG, w))
        a_im = jnp.broadcast_to(a_ref[0, 1:2, re_cols], (N_SEG, w))

        def body(tt, carry, re_cols=re_cols, im_cols=im_cols, a_re=a_re, a_im=a_im):
            x_re, x_im = carry
            t = jnp.where(d == 0, tt, tc - 1 - tt)
            rows = pl.ds(pl.multiple_of(t * N_SEG, N_SEG), N_SEG)
            n_re = a_re * x_re - a_im * x_im + bu_scr[rows, re_cols]
            n_im = a_re * x_im + a_im * x_re + bu_scr[rows, im_cols]
            bu_scr[rows, re_cols] = n_re
            bu_scr[rows, im_cols] = n_im
            return n_re, n_im

        x_re, x_im = lax.fori_loop(0, tc, body, (st_scr[:, re_cols], st_scr[:, im_cols]), unroll=4)
        st_scr[:, re_cols] = x_re
        st_scr[:, im_cols] = x_im

    if emit_y:
        y = jnp.dot(bu_scr[...].astype(BF16), c_ref[...], preferred_element_type=F32)
        for scr, cols in halves:
            scr[...] = y[:, cols]
        for s in range(N_SEG):
            for scr, cols in halves:
                y_ref[0, s, :, cols] = scr[pl.ds(s, tc, stride=N_SEG), :]

    @pl.when(c == pl.num_programs(2) - 1)
    def _():
        fin_ref[0, 0] = st_scr[...]


def _ssm_scan(ssm_proj, seg_len, first_group, n_groups, b_blk, a_rows, c_blk, init, tc, emit_y):
    u = ssm_proj.reshape(-1, seg_len, 2 * SSM_WIDTH)
    nc = seg_len // tc

    def chunk(d, c):
        return jnp.where(d == 0, c, nc - 1 - c)

    out_specs = [pl.BlockSpec((1, 1, N_SEG, SSM_REAL), lambda d, g, c: (d, g, 0, 0))]
    out_shape = [jax.ShapeDtypeStruct((2, n_groups, N_SEG, SSM_REAL), F32)]
    if emit_y:
        out_specs = [pl.BlockSpec((1, N_SEG, tc, SSM_WIDTH), lambda d, g, c: (d, g, chunk(d, c), 0))] + out_specs
        out_shape = [jax.ShapeDtypeStruct((2, n_groups * N_SEG, seg_len, SSM_WIDTH), F32)] + out_shape
    res = pl.pallas_call(
        functools.partial(_ssm_kernel, tc=tc, emit_y=emit_y),
        grid=(2, n_groups, nc),
        in_specs=[
            pl.BlockSpec((N_SEG, tc, SSM_WIDTH), lambda d, g, c: (first_group + g, chunk(d, c), 0)),
            pl.BlockSpec((1, SSM_WIDTH, SSM_REAL), lambda d, g, c: (d, 0, 0)),
            pl.BlockSpec((1, 2, SSM_COMPLEX), lambda d, g, c: (d, 0, 0)),
            pl.BlockSpec((SSM_REAL, SSM_WIDTH), lambda d, g, c: (0, 0)),
            pl.BlockSpec((1, 1, N_SEG, SSM_REAL), lambda d, g, c: (d, g, 0, 0)),
        ],
        out_specs=out_specs,
        out_shape=out_shape,
        scratch_shapes=[pltpu.VMEM((tc * N_SEG, SSM_REAL), F32), pltpu.VMEM((N_SEG, SSM_REAL), F32),
                        pltpu.VMEM((tc * N_SEG, LANES), F32), pltpu.VMEM((tc * N_SEG, LANES), F32)],
        compiler_params=_cparams("arbitrary", "arbitrary", "arbitrary"),
        name="ssm_scan_y" if emit_y else "ssm_scan_state",
    )(u, b_blk, a_rows, c_blk, init)
    if emit_y:
        return res[0].reshape(2, -1, SSM_WIDTH), res[1]
    return None, res[0]


def _ssm_params(lam_re, lam_im, log_step, b_re, b_im, c_re, c_im):
    lam = lax.complex(lam_re.astype(F32), lam_im.astype(F32))
    step = jnp.exp(log_step.astype(F32))[..., None]
    lam_bar = jnp.exp(lam * step)
    b = lax.complex(b_re.astype(F32), b_im.astype(F32))
    b_bar = ((lam_bar - 1.0) / lam)[..., None] * b
    eye = jnp.eye(SSM_GROUPS, dtype=F32)
    b_blk_re = jnp.einsum('dgph,gk->dghkp', jnp.real(b_bar), eye).reshape(2, SSM_WIDTH, SSM_COMPLEX)
    b_blk_im = jnp.einsum('dgph,gk->dghkp', jnp.imag(b_bar), eye).reshape(2, SSM_WIDTH, SSM_COMPLEX)
    b_blk = jnp.concatenate([b_blk_re, b_blk_im], axis=-1).astype(BF16)
    c_blk_re = jnp.einsum('ghp,gk->gpkh', c_re.astype(F32), eye).reshape(SSM_COMPLEX, SSM_WIDTH)
    c_blk_im = jnp.einsum('ghp,gk->gpkh', c_im.astype(F32), eye).reshape(SSM_COMPLEX, SSM_WIDTH)
    c_blk = jnp.concatenate([c_blk_re, -c_blk_im], axis=0).astype(BF16)
    a_rows = jnp.stack([jnp.real(lam_bar).reshape(2, SSM_COMPLEX),
                        jnp.imag(lam_bar).reshape(2, SSM_COMPLEX)], axis=1)
    return lam_bar.reshape(2, SSM_COMPLEX), b_blk, a_rows, c_blk


def _chain_init(fin, lam_pow):
    f = lax.complex(fin[..., :SSM_COMPLEX], fin[..., SSM_COMPLEX:])
    zero = jnp.zeros_like(f[0, 0])
    fwd = [zero]
    for s in range(1, N_SEG):
        fwd.append(lam_pow[0] * fwd[-1] + f[0, s - 1])
    bwd = [zero]
    for s in range(N_SEG - 2, -1, -1):
        bwd.append(lam_pow[1] * bwd[-1] + f[1, s + 1])
    init = jnp.stack([jnp.stack(fwd), jnp.stack(bwd[::-1])])
    return jnp.concatenate([jnp.real(init), jnp.imag(init)], axis=-1).astype(F32)


def _attn_sgu_kernel(flags_ref, q_ref, kp_ref, kc_ref, kn_ref, vp_ref, vc_ref, vn_ref, bias_ref, sink_ref,
                     sgu_ref, sgn_ref, sgw_ref, sgb_ref, attn_ref, sguo_ref, *, nbq):
    n = pl.program_id(0)
    no_prev = flags_ref[0, n * nbq] == 0
    no_next = flags_ref[1, n * nbq + nbq - 1] == 0
    def key_block(prev_ref, cur_ref, next_ref, i):
        if i == 0:
            return prev_ref[...]
        if i == nbq + 1:
            return next_ref[...]
        return cur_ref[(i - 1) * BLOCK:i * BLOCK, :]

    for b in range(nbq):
        rows = slice(b * BLOCK, (b + 1) * BLOCK)
        q = q_ref[rows, :]
        ks = [key_block(kp_ref, kc_ref, kn_ref, b + i) for i in range(3)]
        vs = [key_block(vp_ref, vc_ref, vn_ref, b + i) for i in range(3)]
        outs = []
        for j in range(N_KV_HEADS):
            qs = jnp.concatenate(
                [q[:, (j * Q_PER_KV + i) * HEAD_DIM:(j * Q_PER_KV + i + 1) * HEAD_DIM] for i in range(Q_PER_KV)],
                axis=0)
            kv_cols = slice(j * HEAD_DIM, (j + 1) * HEAD_DIM)
            s = []
            for i in range(3):
                si = lax.dot_general(qs, ks[i][:, kv_cols], (((1,), (1,)), ((), ())), preferred_element_type=F32)
                si = si * (HEAD_DIM ** -0.5) + bias_ref[j, :, i * BLOCK:(i + 1) * BLOCK]
                if b == 0 and i == 0:
                    si = jnp.where(no_prev, NEG_INF, si)
                if b == nbq - 1 and i == 2:
                    si = jnp.where(no_next, NEG_INF, si)
                s.append(si)
            sink = sink_ref[j]
            m = jnp.max(jnp.maximum(jnp.maximum(s[0], s[1]), s[2]), axis=-1, keepdims=True)
            m = jnp.maximum(m, sink)
            p = [jnp.exp(si - m) for si in s]
            denom = jnp.sum(p[0] + p[1] + p[2], axis=-1, keepdims=True) + jnp.exp(sink - m)
            o = sum(jnp.dot(p[i].astype(BF16), vs[i][:, kv_cols], preferred_element_type=F32) for i in range(3))
            o = o / denom
            outs += [o[i * BLOCK:(i + 1) * BLOCK] for i in range(Q_PER_KV)]
        attn_ref[rows, :] = jnp.concatenate(outs, axis=1).astype(BF16)

        gu = jax.nn.gelu(sgu_ref[rows, :SGU_WIDTH].astype(F32))
        gv = _rms(jax.nn.gelu(sgu_ref[rows, SGU_WIDTH:].astype(F32)), sgn_ref[...]).astype(BF16)
        mixed = jnp.concatenate(
            [jnp.dot(sgw_ref[g], gv[:, g * SGU_GROUP_WIDTH:(g + 1) * SGU_GROUP_WIDTH], preferred_element_type=F32)
             for g in range(SGU_GROUPS)], axis=1) + sgb_ref[...]
        sguo_ref[rows, :] = (gu * mixed).astype(BF16)


def _attn_sgu(flags, qkv, sgu, bias, sink, sgn, sgw, sgb, nbq):
    t = qkv.shape[0]
    nb = t // BLOCK
    tile = nbq * BLOCK
    kcol = ATTN_WIDTH // KV_WIDTH
    vcol = kcol + 1
    prev = lambda col: pl.BlockSpec((BLOCK, KV_WIDTH), lambda n, f: (jnp.maximum(n * nbq - 1, 0), col))
    cur = lambda col: pl.BlockSpec((tile, KV_WIDTH), lambda n, f: (n, col))
    nxt = lambda col: pl.BlockSpec((BLOCK, KV_WIDTH), lambda n, f: (jnp.minimum((n + 1) * nbq, nb - 1), col))
    grid_spec = pltpu.PrefetchScalarGridSpec(
        num_scalar_prefetch=1,
        grid=(nb // nbq,),
        in_specs=[
            pl.BlockSpec((tile, ATTN_WIDTH), lambda n, f: (n, 0)),
            prev(kcol), cur(kcol), nxt(kcol),
            prev(vcol), cur(vcol), nxt(vcol),
            pl.BlockSpec((N_KV_HEADS, Q_PER_KV * BLOCK, 3 * BLOCK), lambda n, f: (0, 0, 0)),
            pl.BlockSpec((N_KV_HEADS, Q_PER_KV * BLOCK, 1), lambda n, f: (0, 0, 0)),
            pl.BlockSpec((tile, 2 * SGU_WIDTH), lambda n, f: (n, 0)),
            pl.BlockSpec((1, SGU_WIDTH), lambda n, f: (0, 0)),
            pl.BlockSpec((SGU_GROUPS, BLOCK, BLOCK), lambda n, f: (0, 0, 0)),
            pl.BlockSpec((BLOCK, SGU_WIDTH), lambda n, f: (0, 0)),
        ],
        out_specs=[
            pl.BlockSpec((tile, ATTN_WIDTH), lambda n, f: (n, 0)),
            pl.BlockSpec((tile, SGU_WIDTH), lambda n, f: (n, 0)),
        ],
    )
    return pl.pallas_call(
        functools.partial(_attn_sgu_kernel, nbq=nbq),
        grid_spec=grid_spec,
        out_shape=[jax.ShapeDtypeStruct((t, ATTN_WIDTH), BF16), jax.ShapeDtypeStruct((t, SGU_WIDTH), BF16)],
        compiler_params=_cparams("arbitrary"),
        name="attn_sgu",
    )(flags, qkv, qkv, qkv, qkv, qkv, qkv, qkv, bias, sink, sgu, sgn, sgw, sgb)


def _t5_bucket(rel):
    half = N_BUCKETS // 2
    max_exact = half // 2
    ret = jnp.where(rel > 0, half, 0)
    n = jnp.abs(rel)
    nf = jnp.maximum(n, 1).astype(F32)
    large = max_exact + (jnp.log(nf / max_exact) / math.log(MAX_DISTANCE / max_exact)
                         * (half - max_exact)).astype(jnp.int32)
    large = jnp.minimum(large, half - 1)
    return ret + jnp.where(n < max_exact, n, large)


def _band_bias(rel_table):
    q_pos = jnp.arange(BLOCK)[:, None]
    k_pos = jnp.arange(3 * BLOCK)[None, :] - BLOCK
    rel = k_pos - q_pos
    hit = _t5_bucket(rel)[..., None] == jnp.arange(N_BUCKETS)
    bias = jnp.sum(jnp.where(hit[..., None], rel_table.astype(F32), 0.0), axis=2)
    bias = jnp.where((jnp.abs(rel) <= BLOCK)[..., None], bias, NEG_INF)
    return bias.transpose(2, 0, 1).reshape(N_KV_HEADS, Q_PER_KV * BLOCK, 3 * BLOCK)


def _outproj_router_kernel(x_ref, ssm_ref, yf_ref, yb_ref, attn_ref, sguo_ref, d_ref, gout_ref, wout_ref,
                           gffn_ref, wrh_ref, wrl_ref, br_ref, x1_ref, h_ref, idx_ref, rank_ref, wgt_ref, cnt_ref,
                           run_scr, *, tm):
    i = pl.program_id(0)

    @pl.when(i == 0)
    def _():
        run_scr[...] = jnp.zeros_like(run_scr)

    u = ssm_ref[:, :SSM_WIDTH].astype(F32)
    gate = ssm_ref[:, SSM_WIDTH:].astype(F32)
    ya = yf_ref[...] + yb_ref[...] + d_ref[...] * u
    ya = jax.nn.gelu(ya) * jax.nn.sigmoid(gate)
    g = gout_ref[...]
    mixed = jnp.concatenate([
        _rms(ya, g[:, :SSM_WIDTH]),
        _rms(attn_ref[...].astype(F32), g[:, SSM_WIDTH:SSM_WIDTH + ATTN_WIDTH]),
        _rms(sguo_ref[...].astype(F32), g[:, SSM_WIDTH + ATTN_WIDTH:]),
    ], axis=1).astype(BF16)
    x1 = x_ref[...] + jnp.dot(mixed, wout_ref[...], preferred_element_type=F32)
    x1_ref[...] = x1
    h = _rms(x1, gffn_ref[...])
    _store_token_tiles(h_ref, 0, h, tm)

    h_hi = h.astype(BF16)
    h_lo = (h - h_hi.astype(F32)).astype(BF16)
    nt_dot = lambda a, b: lax.dot_general(a, b, (((1,), (1,)), ((), ())), preferred_element_type=F32)
    lg = nt_dot(wrh_ref[...], h_hi) + nt_dot(wrl_ref[...], h_hi) + nt_dot(wrh_ref[...], h_lo) + br_ref[...]
    row = lax.broadcasted_iota(jnp.int32, (N_EXPERTS, tm), 0)
    vals, idxs, sels = [], [], []
    for _ in range(TOP_K):
        m = jnp.max(lg, axis=0, keepdims=True)
        idx = jnp.min(jnp.where(lg == m, row, N_EXPERTS), axis=0, keepdims=True)
        sel = row == idx
        lg = jnp.where(sel, -jnp.inf, lg)
        vals.append(m)
        idxs.append(idx)
        sels.append(sel)
    e = [jnp.exp(v - vals[0]) for v in vals]
    tot = e[0] + e[1] + e[2] + e[3]
    wgt_ref[...] = jnp.concatenate([ek / tot for ek in e], axis=0)
    idx_ref[...] = jnp.concatenate(idxs, axis=0)

    picked = (sels[0] | sels[1] | sels[2] | sels[3])
    pm = jnp.where(picked, 1.0, 0.0)
    r_io = lax.broadcasted_iota(jnp.int32, (tm, tm), 0)
    c_io = lax.broadcasted_iota(jnp.int32, (tm, tm), 1)
    earlier = jnp.where(r_io < c_io, 1.0, 0.0).astype(BF16)
    before = jnp.dot(pm.astype(BF16), earlier, preferred_element_type=F32) + run_scr[...]
    rank_ref[...] = jnp.concatenate(
        [jnp.sum(jnp.where(s, before, 0.0), axis=0, keepdims=True) for s in sels], axis=0).astype(jnp.int32)
    run_scr[...] = run_scr[...] + jnp.sum(pm, axis=1, keepdims=True)

    @pl.when(i == pl.num_programs(0) - 1)
    def _():
        cnt_ref[...] = run_scr[...].astype(jnp.int32)


def _outproj_router(x, ssm, yf, yb, attn, sguo, d_skip, g_out, w_out, g_ffn, w_r, b_r, tm):
    t = x.shape[0]
    row = lambda w: pl.BlockSpec((tm, w), lambda i: (i, 0))
    full = lambda a, b: pl.BlockSpec((a, b), lambda i: (0, 0))
    per_choice = pl.BlockSpec((TOP_K, tm), lambda i: (0, i))
    w_rt = w_r.astype(F32).T
    w_hi = w_rt.astype(BF16)
    w_lo = (w_rt - w_hi.astype(F32)).astype(BF16)
    return pl.pallas_call(
        functools.partial(_outproj_router_kernel, tm=tm),
        grid=(t // tm,),
        in_specs=[
            row(D_MODEL), row(2 * SSM_WIDTH), row(SSM_WIDTH), row(SSM_WIDTH), row(ATTN_WIDTH), row(SGU_WIDTH),
            full(1, SSM_WIDTH), full(1, D_MODEL), full(D_MODEL, D_MODEL), full(1, D_MODEL),
            full(N_EXPERTS, D_MODEL), full(N_EXPERTS, D_MODEL), full(N_EXPERTS, 1),
        ],
        out_specs=[row(D_MODEL), pl.BlockSpec((tm * TOKEN_TILE_ROWS, LANES), lambda i: (i, 0)),
                   per_choice, per_choice, per_choice, full(N_EXPERTS, 1)],
        out_shape=[
            jax.ShapeDtypeStruct((t, D_MODEL), F32),
            jax.ShapeDtypeStruct((t * TOKEN_TILE_ROWS, LANES), F32),
            jax.ShapeDtypeStruct((TOP_K, t), jnp.int32),
            jax.ShapeDtypeStruct((TOP_K, t), jnp.int32),
            jax.ShapeDtypeStruct((TOP_K, t), F32),
            jax.ShapeDtypeStruct((N_EXPERTS, 1), jnp.int32),
        ],
        scratch_shapes=[pltpu.VMEM((N_EXPERTS, 1), F32)],
        compiler_params=_cparams("arbitrary"),
        name="outproj_router",
    )(x, ssm, yf, yb, attn, sguo, d_skip, g_out, w_out, g_ffn, w_hi, w_lo, b_r.astype(F32).reshape(N_EXPERTS, 1))


ISSUE_UNROLL = 8


def _for_rows(n, fn):
    def body(c, carry):
        for u in range(ISSUE_UNROLL):
            fn(c * ISSUE_UNROLL + u)
        return carry

    lax.fori_loop(0, n // ISSUE_UNROLL, body, 0)


def _tile_rows(ref, first_tile, n_tiles=1):
    start = first_tile * TOKEN_TILE_ROWS
    if not isinstance(start, int):
        start = pl.multiple_of(start, TOKEN_TILE_ROWS)
    return ref.at[pl.ds(start, n_tiles * TOKEN_TILE_ROWS)]


EXPERT_COL_CHUNK = 256


def _experts_kernel(tile_ref, exp_ref, lo_ref, hi_ref, tok_ref, tokn_ref, dstp_ref, dst_ref, h_hbm, wgu_ref, bgu_ref,
                    wd_ref, bd_ref, out_hbm, x_buf, y_buf, wgu_bf, wd_bf, gsem, ssem, *, tm, spare_row):
    w = pl.program_id(0)
    nw = pl.num_programs(0)
    slot = lax.rem(w, 2)
    other = 1 - slot
    prev = jnp.maximum(w - 1, 0)
    lo_prev = jnp.where(w > 0, lo_ref[prev], 0)
    hi_prev = jnp.where(w > 0, hi_ref[prev], 0)

    def gather_row(list_ref, to_slot, r):
        pltpu.make_async_copy(_tile_rows(h_hbm, list_ref[0, 0, r]), _tile_rows(x_buf, to_slot * tm + r),
                              gsem.at[to_slot]).start()

    def scatter_row(list_ref, from_slot, lo, hi, r):
        dst = jnp.where((r >= lo) & (r < hi), list_ref[0, 0, r], spare_row + r)
        pltpu.make_async_copy(_tile_rows(y_buf, from_slot * tm + r), _tile_rows(out_hbm, dst),
                              ssem.at[from_slot]).start()

    def wait_gather(of_slot):
        pltpu.make_async_copy(_tile_rows(h_hbm, 0, tm), _tile_rows(x_buf, of_slot * tm, tm), gsem.at[of_slot]).wait()

    def wait_scatter(of_slot):
        pltpu.make_async_copy(_tile_rows(y_buf, of_slot * tm, tm), _tile_rows(out_hbm, 0, tm), ssem.at[of_slot]).wait()

    @pl.when(w == 0)
    def _():
        y_buf[...] = jnp.zeros_like(y_buf)
        _for_rows(tm, lambda r: gather_row(tok_ref, 0, r))

    @pl.when((w == 0) | (exp_ref[w] != exp_ref[prev]))
    def _():
        wgu_bf[...] = wgu_ref[0].astype(BF16)
        wd_bf[...] = wd_ref[0].astype(BF16)

    wait_gather(slot)

    @pl.when(w >= 1)
    def _():
        wait_scatter(slot)

    n_pieces = (D_EXPERT + D_MODEL) // EXPERT_COL_CHUNK
    per_piece = tm // n_pieces
    piece = 0

    def issue_neighbours():
        nonlocal piece
        for r in range(piece * per_piece, (piece + 1) * per_piece):
            gather_row(tokn_ref, other, r)
            scatter_row(dstp_ref, other, lo_prev, hi_prev, r)
        piece += 1

    base = pl.multiple_of(slot * (tm * TOKEN_TILE_ROWS), tm * TOKEN_TILE_ROWS)
    x = _load_token_tiles(x_buf, base, tm).astype(BF16)
    acts = []
    for c in range(D_EXPERT // EXPERT_COL_CHUNK):
        issue_neighbours()
        g_cols = slice(c * EXPERT_COL_CHUNK, (c + 1) * EXPERT_COL_CHUNK)
        u_cols = slice(D_EXPERT + c * EXPERT_COL_CHUNK, D_EXPERT + (c + 1) * EXPERT_COL_CHUNK)
        gate = jnp.dot(x, wgu_bf[:, g_cols], preferred_element_type=F32) + bgu_ref[0, :, g_cols]
        up = jnp.dot(x, wgu_bf[:, u_cols], preferred_element_type=F32) + bgu_ref[0, :, u_cols]
        gate = jnp.minimum(gate, SWIGLU_LIMIT)
        up = jnp.clip(up, -SWIGLU_LIMIT, SWIGLU_LIMIT)
        acts.append(((up + 1.0) * gate * jax.nn.sigmoid(SWIGLU_ALPHA * gate)).astype(BF16))
    act = jnp.concatenate(acts, axis=1)
    tiles_per_chunk = EXPERT_COL_CHUNK // LANES
    for c in range(D_MODEL // EXPERT_COL_CHUNK):
        issue_neighbours()
        cols = slice(c * EXPERT_COL_CHUNK, (c + 1) * EXPERT_COL_CHUNK)
        y = jnp.dot(act, wd_bf[:, cols], preferred_element_type=F32) + bd_ref[0, :, cols]
        for j in range(tiles_per_chunk):
            y_buf[pl.ds(base + c * tiles_per_chunk + j, tm, stride=TOKEN_TILE_ROWS), :] = y[:, j * LANES:(j + 1) * LANES]

    @pl.when(w == nw - 1)
    def _():
        _for_rows(tm, lambda r: scatter_row(dst_ref, slot, lo_ref[w], hi_ref[w], r))
        wait_gather(other)
        wait_scatter(other)
        wait_scatter(slot)


def _experts(item_tile, item_expert, item_lo, item_hi, tok_sorted, dst_sorted, h_tiles, wgu, bgu, wd, bd, layer, tm):
    nw = item_tile.shape[0]
    nt = tok_sorted.shape[0] // tm
    n_rows = tok_sorted.shape[0]
    lists = lambda a: a.reshape(nt, 1, tm)
    cur = lambda i, ti, ex, lo, hi: (ti[i], 0, 0)
    nxt = lambda i, ti, ex, lo, hi: (ti[jnp.minimum(i + 1, nw - 1)], 0, 0)
    prv = lambda i, ti, ex, lo, hi: (ti[jnp.maximum(i - 1, 0)], 0, 0)
    by_expert = lambda i, ti, ex, lo, hi: (layer, ex[i], 0, 0)
    grid_spec = pltpu.PrefetchScalarGridSpec(
        num_scalar_prefetch=4,
        grid=(nw,),
        in_specs=[
            pl.BlockSpec((1, 1, tm), cur, memory_space=pltpu.SMEM),
            pl.BlockSpec((1, 1, tm), nxt, memory_space=pltpu.SMEM),
            pl.BlockSpec((1, 1, tm), prv, memory_space=pltpu.SMEM),
            pl.BlockSpec((1, 1, tm), cur, memory_space=pltpu.SMEM),
            pl.BlockSpec(memory_space=pl.ANY),
            pl.BlockSpec((None, 1, D_MODEL, 2 * D_EXPERT), by_expert),
            pl.BlockSpec((None, 1, 1, 2 * D_EXPERT), by_expert),
            pl.BlockSpec((None, 1, D_EXPERT, D_MODEL), by_expert),
            pl.BlockSpec((None, 1, 1, D_MODEL), by_expert),
        ],
        out_specs=pl.BlockSpec(memory_space=pl.ANY),
        scratch_shapes=[
            pltpu.VMEM((2 * tm * TOKEN_TILE_ROWS, LANES), F32),
            pltpu.VMEM((2 * tm * TOKEN_TILE_ROWS, LANES), F32),
            pltpu.VMEM((D_MODEL, 2 * D_EXPERT), BF16),
            pltpu.VMEM((D_EXPERT, D_MODEL), BF16),
            pltpu.SemaphoreType.DMA((2,)),
            pltpu.SemaphoreType.DMA((2,)),
        ],
    )
    return pl.pallas_call(
        functools.partial(_experts_kernel, tm=tm, spare_row=n_rows),
        grid_spec=grid_spec,
        out_shape=jax.ShapeDtypeStruct(((n_rows + tm) * TOKEN_TILE_ROWS, LANES), F32),
        compiler_params=_cparams("arbitrary"),
        name="experts",
    )(item_tile, item_expert, item_lo, item_hi, lists(tok_sorted), lists(tok_sorted), lists(dst_sorted),
      lists(dst_sorted), h_tiles, wgu, bgu, wd, bd)


def _combine_kernel(y0_ref, y1_ref, y2_ref, y3_ref, x1_ref, wgt_ref, g_ref, out_ref, *, tm, final):
    w = wgt_ref[...]
    x2 = x1_ref[...]
    for k, y_ref in enumerate((y0_ref, y1_ref, y2_ref, y3_ref)):
        x2 = x2 + _load_token_tiles(y_ref, 0, tm) * w[:, k:k + 1]
    if final:
        x2 = _rms(x2, g_ref[...])
    out_ref[...] = x2


def _combine(y_tiles, x1, wgt, g_final, tm, final):
    t = x1.shape[0]
    nb = t // tm
    choice = lambda k: pl.BlockSpec((tm * TOKEN_TILE_ROWS, LANES), lambda i: (k * nb + i, 0))
    return pl.pallas_call(
        functools.partial(_combine_kernel, tm=tm, final=final),
        grid=(nb,),
        in_specs=[
            choice(0), choice(1), choice(2), choice(3),
            pl.BlockSpec((tm, D_MODEL), lambda i: (i, 0)),
            pl.BlockSpec((tm, TOP_K), lambda i: (i, 0)),
            pl.BlockSpec((1, D_MODEL), lambda i: (0, 0)),
        ],
        out_specs=pl.BlockSpec((tm, D_MODEL), lambda i: (i, 0)),
        out_shape=jax.ShapeDtypeStruct((t, D_MODEL), F32),
        compiler_params=_cparams("arbitrary"),
        name="combine",
    )(y_tiles, y_tiles, y_tiles, y_tiles, x1, wgt, g_final)


def _routing_lists(idx, rank, cnt, tm):
    t = idx.shape[1]
    n_rows = t * TOP_K
    nt = n_rows // tm
    group_end = jnp.cumsum(cnt)
    group_start = group_end - cnt
    experts = jnp.arange(N_EXPERTS, dtype=jnp.int32)
    pos = rank + jnp.sum(jnp.where(idx[..., None] == experts, group_start, 0), axis=-1)
    dst = jnp.arange(n_rows, dtype=jnp.int32)
    _, dst_sorted = lax.sort_key_val(pos.reshape(-1).astype(jnp.int32), dst)
    tok_sorted = dst_sorted % t
    bounds = jnp.sort(jnp.concatenate([jnp.arange(nt, dtype=jnp.int32) * tm, group_start.astype(jnp.int32)]))
    nxt = jnp.concatenate([bounds[1:], jnp.array([n_rows], jnp.int32)])
    tile = jnp.minimum(bounds // tm, nt - 1)
    lo = bounds - tile * tm
    hi = jnp.where(nxt > bounds, nxt - tile * tm, lo)
    expert = jnp.minimum(jnp.sum(group_end[None, :] <= bounds[:, None], axis=-1), N_EXPERTS - 1)
    items = tuple(a.astype(jnp.int32) for a in (tile, expert, lo, hi))
    return items, tok_sorted, dst_sorted


def _block_flags(n_prompt_seq, seg_len, n_tokens):
    nb = n_tokens // BLOCK
    per_seg = seg_len // BLOCK
    b = jnp.arange(nb)
    prompt_blocks = n_prompt_seq * per_seg
    in_prompt = b < prompt_blocks
    first = jnp.where(in_prompt, b % per_seg == 0, b == prompt_blocks)
    last = jnp.where(in_prompt, b % per_seg == per_seg - 1, b == nb - 1)
    return jnp.stack([~first, ~last]).astype(jnp.int32)


def _trunk(x, n_prompt_seq, seg_len, p, tm, tm_moe, tc, nbq):
    t = x.shape[0]
    depth = p['w_in'].shape[0]
    flags = _block_flags(n_prompt_seq, seg_len, t)
    bias = _band_bias(p['rel_bias'])
    zero_init = jnp.zeros((2, 1, N_SEG, SSM_REAL), F32)
    for layer in range(depth):
        ssm, qkv, sgu = _inproj(x, p['norm_mix'][layer][None], p['w_in'][layer].astype(BF16), tm)

        lam_bar, b_blk, a_rows, c_blk = _ssm_params(
            p['ssm_lam_re'][layer], p['ssm_lam_im'][layer], p['ssm_log_step'][layer], p['ssm_b_re'][layer],
            p['ssm_b_im'][layer], p['ssm_c_re'][layer], p['ssm_c_im'][layer])
        _, fin = _ssm_scan(ssm, seg_len, 1, 1, b_blk, a_rows, c_blk, zero_init, tc, False)
        init = jnp.concatenate([zero_init, _chain_init(fin[:, 0], lam_bar ** seg_len)[:, None]], axis=1)
        y, _ = _ssm_scan(ssm, seg_len, 0, 2, b_blk, a_rows, c_blk, init, tc, True)

        sink = jnp.repeat(p['attn_sink'][layer].astype(F32), BLOCK).reshape(N_KV_HEADS, Q_PER_KV * BLOCK, 1)
        sgb = jnp.repeat(p['sgu_b'][layer].astype(F32).T, SGU_GROUP_WIDTH, axis=1)
        attn, sguo = _attn_sgu(flags, qkv, sgu, bias, sink, p['sgu_norm'][layer][None].astype(F32),
                               p['sgu_w'][layer].astype(BF16), sgb, nbq)

        x1, h, idx, rank, wgt, cnt = _outproj_router(
            x, ssm, y[0], y[1], attn, sguo, p['ssm_d'][layer][None].astype(F32), p['out_norm'][layer][None],
            p['w_out'][layer].astype(BF16), p['norm_ffn'][layer][None], p['w_router'][layer].astype(F32),
            p['b_router'][layer][None].astype(F32), tm)

        items, tok_sorted, dst_sorted = _routing_lists(idx, rank, cnt[:, 0], tm_moe)
        wgt = wgt.T
        y_tiles = _experts(*items, tok_sorted, dst_sorted, h, p['w_gate_up'].astype(F32),
                           p['b_gate_up'][:, :, None].astype(F32), p['w_down'].astype(F32),
                           p['b_down'][:, :, None].astype(F32), layer, tm_moe)
        x = _combine(y_tiles, x1, wgt, p['final_norm'][None].astype(F32), min(tm, 256), layer == depth - 1)
    return x


def _run(x_prompt, x_sample, p, tm=512, tm_moe=512, tc=64, nbq=2):
    bsz, seg_len, _ = x_prompt.shape
    assert bsz == N_SEG and x_sample.shape[0] == 1 and x_sample.shape[1] == N_SEG * seg_len
    assert seg_len % (nbq * BLOCK) == 0 and seg_len % tc == 0
    x = jnp.concatenate([x_prompt.reshape(-1, D_MODEL), x_sample.reshape(-1, D_MODEL)], axis=0).astype(F32)
    y = _trunk(x, bsz, seg_len, p, tm, tm_moe, tc, nbq)
    t_prompt = bsz * seg_len
    return y[:t_prompt].reshape(x_prompt.shape), y[t_prompt:].reshape(x_sample.shape)


def kernel(x_prompt, x_sample, norm_mix, w_in, ssm_lam_re, ssm_lam_im, ssm_log_step, ssm_b_re, ssm_b_im, ssm_c_re, ssm_c_im, ssm_d, attn_sink, rel_bias, sgu_norm, sgu_w, sgu_b, out_norm, w_out, norm_ffn, w_router, b_router, w_gate_up, b_gate_up, w_down, b_down, final_norm):
    p = dict(norm_mix=norm_mix, w_in=w_in, ssm_lam_re=ssm_lam_re, ssm_lam_im=ssm_lam_im, ssm_log_step=ssm_log_step,
             ssm_b_re=ssm_b_re, ssm_b_im=ssm_b_im, ssm_c_re=ssm_c_re, ssm_c_im=ssm_c_im, ssm_d=ssm_d,
             attn_sink=attn_sink, rel_bias=rel_bias, sgu_norm=sgu_norm, sgu_w=sgu_w, sgu_b=sgu_b, out_norm=out_norm,
             w_out=w_out, norm_ffn=norm_ffn, w_router=w_router, b_router=b_router, w_gate_up=w_gate_up,
             b_gate_up=b_gate_up, w_down=w_down, b_down=b_down, final_norm=final_norm)
    return _run(x_prompt, x_sample, p)
```

```python
import functools
import math

import jax
import jax.numpy as jnp
from jax import lax
from jax.experimental import pallas as pl
from jax.experimental.pallas import tpu as pltpu

D_MODEL = 1024
SSM_WIDTH = 256
SSM_GROUP = 16
SSM_GROUPS = 16
SSM_STATE = 64
SSM_COMPLEX = SSM_GROUPS * SSM_STATE
SSM_REAL = 2 * SSM_COMPLEX
ATTN_WIDTH = 512
HEAD_DIM = 64
N_Q_HEADS = 8
N_KV_HEADS = 2
Q_PER_KV = 4
KV_WIDTH = 128
BLOCK = 128
N_BUCKETS = 32
MAX_DISTANCE = 128
SGU_WIDTH = 256
SGU_GROUPS = 4
SGU_GROUP_WIDTH = 64
IN_COLS = 1792
QKV_COLS = ATTN_WIDTH + 2 * KV_WIDTH
N_EXPERTS = 32
TOP_K = 4
D_EXPERT = 1024
SWIGLU_LIMIT = 7.0
SWIGLU_ALPHA = 1.702
EPS = 1e-6
NEG_INF = -1e30

N_SEG = 8
V7X_VMEM_LIMIT = 56 * 1024 * 1024

F32 = jnp.float32
BF16 = jnp.bfloat16


def _rms(xf, g):
    return xf * lax.rsqrt(jnp.mean(xf * xf, axis=-1, keepdims=True) + EPS) * g


LANES = 128
TOKEN_TILE_ROWS = D_MODEL // LANES


def _load_token_tiles(ref, base, n):
    return jnp.concatenate(
        [ref[pl.ds(base + j, n, stride=TOKEN_TILE_ROWS), :] for j in range(TOKEN_TILE_ROWS)], axis=1)


def _store_token_tiles(ref, base, val, n):
    for j in range(TOKEN_TILE_ROWS):
        ref[pl.ds(base + j, n, stride=TOKEN_TILE_ROWS), :] = val[:, j * LANES:(j + 1) * LANES]


def _cparams(*sem):
    return pltpu.CompilerParams(dimension_semantics=sem, vmem_limit_bytes=V7X_VMEM_LIMIT)


def _inproj_kernel(x_ref, g_ref, w_ref, ssm_ref, qkv_ref, sgu_ref):
    h = _rms(x_ref[...], g_ref[...]).astype(BF16)
    p = jnp.dot(h, w_ref[...], preferred_element_type=F32)
    ssm_ref[...] = p[:, :2 * SSM_WIDTH].astype(BF16)
    qkv_ref[...] = p[:, 2 * SSM_WIDTH:2 * SSM_WIDTH + QKV_COLS].astype(BF16)
    sgu_ref[...] = p[:, 2 * SSM_WIDTH + QKV_COLS:].astype(BF16)


def _inproj(x, g, w, tm):
    t = x.shape[0]
    return pl.pallas_call(
        _inproj_kernel,
        grid=(t // tm,),
        in_specs=[
            pl.BlockSpec((tm, D_MODEL), lambda i: (i, 0)),
            pl.BlockSpec((1, D_MODEL), lambda i: (0, 0)),
            pl.BlockSpec((D_MODEL, IN_COLS), lambda i: (0, 0)),
        ],
        out_specs=[
            pl.BlockSpec((tm, 2 * SSM_WIDTH), lambda i: (i, 0)),
            pl.BlockSpec((tm, QKV_COLS), lambda i: (i, 0)),
            pl.BlockSpec((tm, 2 * SGU_WIDTH), lambda i: (i, 0)),
        ],
        out_shape=[
            jax.ShapeDtypeStruct((t, 2 * SSM_WIDTH), BF16),
            jax.ShapeDtypeStruct((t, QKV_COLS), BF16),
            jax.ShapeDtypeStruct((t, 2 * SGU_WIDTH), BF16),
        ],
        compiler_params=_cparams("arbitrary"),
        name="inproj",
    )(x, g, w)


SCAN_COL_BLOCKS = 2


def _ssm_kernel(u_ref, b_ref, a_ref, c_ref, init_ref, *rest, tc, emit_y):
    if emit_y:
        y_ref, fin_ref, bu_scr, st_scr, tm_lo, tm_hi = rest
    else:
        fin_ref, bu_scr, st_scr, tm_lo, tm_hi = rest
    d = pl.program_id(0)
    c = pl.program_id(2)
    halves = ((tm_lo, slice(0, LANES)), (tm_hi, slice(LANES, 2 * LANES)))

    @pl.when(c == 0)
    def _():
        st_scr[...] = init_ref[0, 0]

    for s in range(N_SEG):
        u_s = u_ref[s].astype(F32)
        for scr, cols in halves:
            scr[pl.ds(s, tc, stride=N_SEG), :] = u_s[:, cols]
    u_tm = jnp.concatenate([tm_lo[...], tm_hi[...]], axis=1).astype(BF16)
    bu_scr[...] = jnp.dot(u_tm, b_ref[0], preferred_element_type=F32)
    w = SSM_COMPLEX // SCAN_COL_BLOCKS
    for cb in range(SCAN_COL_BLOCKS):
        re_cols = slice(cb * w, (cb + 1) * w)
        im_cols = slice(SSM_COMPLEX + cb * w, SSM_COMPLEX + (cb + 1) * w)
        a_re = jnp.broadcast_to(a_ref[0, 0:1, re_cols], (N_SEG, w))
        a_im = jnp.broadcast_to(a_ref[0, 1:2, re_cols], (N_SEG, w))

        def body(tt, carry, re_cols=re_cols, im_cols=im_cols, a_re=a_re, a_im=a_im):
            x_re, x_im = carry
            t = jnp.where(d == 0, tt, tc - 1 - tt)
            rows = pl.ds(pl.multiple_of(t * N_SEG, N_SEG), N_SEG)
            n_re = a_re * x_re - a_im * x_im + bu_scr[rows, re_cols]
            n_im = a_re * x_im + a_im * x_re + bu_scr[rows, im_cols]
            bu_scr[rows, re_cols] = n_re
            bu_scr[rows, im_cols] = n_im
            return n_re, n_im

        x_re, x_im = lax.fori_loop(0, tc, body, (st_scr[:, re_cols], st_scr[:, im_cols]), unroll=4)
        st_scr[:, re_cols] = x_re
        st_scr[:, im_cols] = x_im

    if emit_y:
        y = jnp.dot(bu_scr[...].astype(BF16), c_ref[...], preferred_element_type=F32)
        for scr, cols in halves:
            scr[...] = y[:, cols]
        for s in range(N_SEG):
            for scr, cols in halves:
                y_ref[0, s, :, cols] = scr[pl.ds(s, tc, stride=N_SEG), :]

    @pl.when(c == pl.num_programs(2) - 1)
    def _():
        fin_ref[0, 0] = st_scr[...]


def _ssm_scan(ssm_proj, seg_len, first_group, n_groups, b_blk, a_rows, c_blk, init, tc, emit_y):
    u = ssm_proj.reshape(-1, seg_len, 2 * SSM_WIDTH)
    nc = seg_len // tc

    def chunk(d, c):
        return jnp.where(d == 0, c, nc - 1 - c)

    out_specs = [pl.BlockSpec((1, 1, N_SEG, SSM_REAL), lambda d, g, c: (d, g, 0, 0))]
    out_shape = [jax.ShapeDtypeStruct((2, n_groups, N_SEG, SSM_REAL), F32)]
    if emit_y:
        out_specs = [pl.BlockSpec((1, N_SEG, tc, SSM_WIDTH), lambda d, g, c: (d, g, chunk(d, c), 0))] + out_specs
        out_shape = [jax.ShapeDtypeStruct((2, n_groups * N_SEG, seg_len, SSM_WIDTH), F32)] + out_shape
    res = pl.pallas_call(
        functools.partial(_ssm_kernel, tc=tc, emit_y=emit_y),
        grid=(2, n_groups, nc),
        in_specs=[
            pl.BlockSpec((N_SEG, tc, SSM_WIDTH), lambda d, g, c: (first_group + g, chunk(d, c), 0)),
            pl.BlockSpec((1, SSM_WIDTH, SSM_REAL), lambda d, g, c: (d, 0, 0)),
            pl.BlockSpec((1, 2, SSM_COMPLEX), lambda d, g, c: (d, 0, 0)),
            pl.BlockSpec((SSM_REAL, SSM_WIDTH), lambda d, g, c: (0, 0)),
            pl.BlockSpec((1, 1, N_SEG, SSM_REAL), lambda d, g, c: (d, g, 0, 0)),
        ],
        out_specs=out_specs,
        out_shape=out_shape,
        scratch_shapes=[pltpu.VMEM((tc * N_SEG, SSM_REAL), F32), pltpu.VMEM((N_SEG, SSM_REAL), F32),
                        pltpu.VMEM((tc * N_SEG, LANES), F32), pltpu.VMEM((tc * N_SEG, LANES), F32)],
        compiler_params=_cparams("arbitrary", "arbitrary", "arbitrary"),
        name="ssm_scan_y" if emit_y else "ssm_scan_state",
    )(u, b_blk, a_rows, c_blk, init)
    if emit_y:
        return res[0].reshape(2, -1, SSM_WIDTH), res[1]
    return None, res[0]


def _ssm_params(lam_re, lam_im, log_step, b_re, b_im, c_re, c_im):
    lam = lax.complex(lam_re.astype(F32), lam_im.astype(F32))
    step = jnp.exp(log_step.astype(F32))[..., None]
    lam_bar = jnp.exp(lam * step)
    b = lax.complex(b_re.astype(F32), b_im.astype(F32))
    b_bar = ((lam_bar - 1.0) / lam)[..., None] * b
    eye = jnp.eye(SSM_GROUPS, dtype=F32)
    b_blk_re = jnp.einsum('dgph,gk->dghkp', jnp.real(b_bar), eye).reshape(2, SSM_WIDTH, SSM_COMPLEX)
    b_blk_im = jnp.einsum('dgph,gk->dghkp', jnp.imag(b_bar), eye).reshape(2, SSM_WIDTH, SSM_COMPLEX)
    b_blk = jnp.concatenate([b_blk_re, b_blk_im], axis=-1).astype(BF16)
    c_blk_re = jnp.einsum('ghp,gk->gpkh', c_re.astype(F32), eye).reshape(SSM_COMPLEX, SSM_WIDTH)
    c_blk_im = jnp.einsum('ghp,gk->gpkh', c_im.astype(F32), eye).reshape(SSM_COMPLEX, SSM_WIDTH)
    c_blk = jnp.concatenate([c_blk_re, -c_blk_im], axis=0).astype(BF16)
    a_rows = jnp.stack([jnp.real(lam_bar).reshape(2, SSM_COMPLEX),
                        jnp.imag(lam_bar).reshape(2, SSM_COMPLEX)], axis=1)
    return lam_bar.reshape(2, SSM_COMPLEX), b_blk, a_rows, c_blk


def _chain_init(fin, lam_pow):
    f = lax.complex(fin[..., :SSM_COMPLEX], fin[..., SSM_COMPLEX:])
    zero = jnp.zeros_like(f[0, 0])
    fwd = [zero]
    for s in range(1, N_SEG):
        fwd.append(lam_pow[0] * fwd[-1] + f[0, s - 1])
    bwd = [zero]
    for s in range(N_SEG - 2, -1, -1):
        bwd.append(lam_pow[1] * bwd[-1] + f[1, s + 1])
    init = jnp.stack([jnp.stack(fwd), jnp.stack(bwd[::-1])])
    return jnp.concatenate([jnp.real(init), jnp.imag(init)], axis=-1).astype(F32)


def _attn_sgu_kernel(flags_ref, q_ref, kp_ref, kc_ref, kn_ref, vp_ref, vc_ref, vn_ref, bias_ref, sink_ref,
                     sgu_ref, sgn_ref, sgw_ref, sgb_ref, attn_ref, sguo_ref, *, nbq):
    n = pl.program_id(0)
    no_prev = flags_ref[0, n * nbq] == 0
    no_next = flags_ref[1, n * nbq + nbq - 1] == 0
    def key_block(prev_ref, cur_ref, next_ref, i):
        if i == 0:
            return prev_ref[...]
        if i == nbq + 1:
            return next_ref[...]
        return cur_ref[(i - 1) * BLOCK:i * BLOCK, :]

    for b in range(nbq):
        rows = slice(b * BLOCK, (b + 1) * BLOCK)
        q = q_ref[rows, :]
        ks = [key_block(kp_ref, kc_ref, kn_ref, b + i) for i in range(3)]
        vs = [key_block(vp_ref, vc_ref, vn_ref, b + i) for i in range(3)]
        outs = []
        for j in range(N_KV_HEADS):
            qs = jnp.concatenate(
                [q[:, (j * Q_PER_KV + i) * HEAD_DIM:(j * Q_PER_KV + i + 1) * HEAD_DIM] for i in range(Q_PER_KV)],
                axis=0)
            kv_cols = slice(j * HEAD_DIM, (j + 1) * HEAD_DIM)
            s = []
            for i in range(3):
                si = lax.dot_general(qs, ks[i][:, kv_cols], (((1,), (1,)), ((), ())), preferred_element_type=F32)
                si = si * (HEAD_DIM ** -0.5) + bias_ref[j, :, i * BLOCK:(i + 1) * BLOCK]
                if b == 0 and i == 0:
                    si = jnp.where(no_prev, NEG_INF, si)
                if b == nbq - 1 and i == 2:
                    si = jnp.where(no_next, NEG_INF, si)
                s.append(si)
            sink = sink_ref[j]
            m = jnp.max(jnp.maximum(jnp.maximum(s[0], s[1]), s[2]), axis=-1, keepdims=True)
            m = jnp.maximum(m, sink)
            p = [jnp.exp(si - m) for si in s]
            denom = jnp.sum(p[0] + p[1] + p[2], axis=-1, keepdims=True) + jnp.exp(sink - m)
            o = sum(jnp.dot(p[i].astype(BF16), vs[i][:, kv_cols], preferred_element_type=F32) for i in range(3))
            o = o / denom
            outs += [o[i * BLOCK:(i + 1) * BLOCK] for i in range(Q_PER_KV)]
        attn_ref[rows, :] = jnp.concatenate(outs, axis=1).astype(BF16)

        gu = jax.nn.gelu(sgu_ref[rows, :SGU_WIDTH].astype(F32))
        gv = _rms(jax.nn.gelu(sgu_ref[rows, SGU_WIDTH:].astype(F32)), sgn_ref[...]).astype(BF16)
        mixed = jnp.concatenate(
            [jnp.dot(sgw_ref[g], gv[:, g * SGU_GROUP_WIDTH:(g + 1) * SGU_GROUP_WIDTH], preferred_element_type=F32)
             for g in range(SGU_GROUPS)], axis=1) + sgb_ref[...]
        sguo_ref[rows, :] = (gu * mixed).astype(BF16)


def _attn_sgu(flags, qkv, sgu, bias, sink, sgn, sgw, sgb, nbq):
    t = qkv.shape[0]
    nb = t // BLOCK
    tile = nbq * BLOCK
    kcol = ATTN_WIDTH // KV_WIDTH
    vcol = kcol + 1
    prev = lambda col: pl.BlockSpec((BLOCK, KV_WIDTH), lambda n, f: (jnp.maximum(n * nbq - 1, 0), col))
    cur = lambda col: pl.BlockSpec((tile, KV_WIDTH), lambda n, f: (n, col))
    nxt = lambda col: pl.BlockSpec((BLOCK, KV_WIDTH), lambda n, f: (jnp.minimum((n + 1) * nbq, nb - 1), col))
    grid_spec = pltpu.PrefetchScalarGridSpec(
        num_scalar_prefetch=1,
        grid=(nb // nbq,),
        in_specs=[
            pl.BlockSpec((tile, ATTN_WIDTH), lambda n, f: (n, 0)),
            prev(kcol), cur(kcol), nxt(kcol),
            prev(vcol), cur(vcol), nxt(vcol),
            pl.BlockSpec((N_KV_HEADS, Q_PER_KV * BLOCK, 3 * BLOCK), lambda n, f: (0, 0, 0)),
            pl.BlockSpec((N_KV_HEADS, Q_PER_KV * BLOCK, 1), lambda n, f: (0, 0, 0)),
            pl.BlockSpec((tile, 2 * SGU_WIDTH), lambda n, f: (n, 0)),
            pl.BlockSpec((1, SGU_WIDTH), lambda n, f: (0, 0)),
            pl.BlockSpec((SGU_GROUPS, BLOCK, BLOCK), lambda n, f: (0, 0, 0)),
            pl.BlockSpec((BLOCK, SGU_WIDTH), lambda n, f: (0, 0)),
        ],
        out_specs=[
            pl.BlockSpec((tile, ATTN_WIDTH), lambda n, f: (n, 0)),
            pl.BlockSpec((tile, SGU_WIDTH), lambda n, f: (n, 0)),
        ],
    )
    return pl.pallas_call(
        functools.partial(_attn_sgu_kernel, nbq=nbq),
        grid_spec=grid_spec,
        out_shape=[jax.ShapeDtypeStruct((t, ATTN_WIDTH), BF16), jax.ShapeDtypeStruct((t, SGU_WIDTH), BF16)],
        compiler_params=_cparams("arbitrary"),
        name="attn_sgu",
    )(flags, qkv, qkv, qkv, qkv, qkv, qkv, qkv, bias, sink, sgu, sgn, sgw, sgb)


def _t5_bucket(rel):
    half = N_BUCKETS // 2
    max_exact = half // 2
    ret = jnp.where(rel > 0, half, 0)
    n = jnp.abs(rel)
    nf = jnp.maximum(n, 1).astype(F32)
    large = max_exact + (jnp.log(nf / max_exact) / math.log(MAX_DISTANCE / max_exact)
                         * (half - max_exact)).astype(jnp.int32)
    large = jnp.minimum(large, half - 1)
    return ret + jnp.where(n < max_exact, n, large)


def _band_bias(rel_table):
    q_pos = jnp.arange(BLOCK)[:, None]
    k_pos = jnp.arange(3 * BLOCK)[None, :] - BLOCK
    rel = k_pos - q_pos
    hit = _t5_bucket(rel)[..., None] == jnp.arange(N_BUCKETS)
    bias = jnp.sum(jnp.where(hit[..., None], rel_table.astype(F32), 0.0), axis=2)
    bias = jnp.where((jnp.abs(rel) <= BLOCK)[..., None], bias, NEG_INF)
    return bias.transpose(2, 0, 1).reshape(N_KV_HEADS, Q_PER_KV * BLOCK, 3 * BLOCK)


def _outproj_router_kernel(x_ref, ssm_ref, yf_ref, yb_ref, attn_ref, sguo_ref, d_ref, gout_ref, wout_ref,
                           gffn_ref, wrh_ref, wrl_ref, br_ref, x1_ref, h_ref, idx_ref, rank_ref, wgt_ref, cnt_ref,
                           *, tm):
    u = ssm_ref[:, :SSM_WIDTH].astype(F32)
    gate = ssm_ref[:, SSM_WIDTH:].astype(F32)
    ya = yf_ref[...] + yb_ref[...] + d_ref[...] * u
    ya = jax.nn.gelu(ya) * jax.nn.sigmoid(gate)
    g = gout_ref[...]
    mixed = jnp.concatenate([
        _rms(ya, g[:, :SSM_WIDTH]),
        _rms(attn_ref[...].astype(F32), g[:, SSM_WIDTH:SSM_WIDTH + ATTN_WIDTH]),
        _rms(sguo_ref[...].astype(F32), g[:, SSM_WIDTH + ATTN_WIDTH:]),
    ], axis=1).astype(BF16)
    x1 = x_ref[...] + jnp.dot(mixed, wout_ref[...], preferred_element_type=F32)
    x1_ref[...] = x1
    h = _rms(x1, gffn_ref[...])
    _store_token_tiles(h_ref, 0, h, tm)

    h_hi = h.astype(BF16)
    h_lo = (h - h_hi.astype(F32)).astype(BF16)
    nt_dot = lambda a, b: lax.dot_general(a, b, (((1,), (1,)), ((), ())), preferred_element_type=F32)
    lg = nt_dot(wrh_ref[...], h_hi) + nt_dot(wrl_ref[...], h_hi) + nt_dot(wrh_ref[...], h_lo) + br_ref[...]
    row = lax.broadcasted_iota(jnp.int32, (N_EXPERTS, tm), 0)
    vals, idxs, sels = [], [], []
    for _ in range(TOP_K):
        m = jnp.max(lg, axis=0, keepdims=True)
        idx = jnp.min(jnp.where(lg == m, row, N_EXPERTS), axis=0, keepdims=True)
        sel = row == idx
        lg = jnp.where(sel, -jnp.inf, lg)
        vals.append(m)
        idxs.append(idx)
        sels.append(sel)
    e = [jnp.exp(v - vals[0]) for v in vals]
    tot = e[0] + e[1] + e[2] + e[3]
    wgt_ref[...] = jnp.concatenate([ek / tot for ek in e], axis=0)
    idx_ref[...] = jnp.concatenate(idxs, axis=0)

    picked = (sels[0] | sels[1] | sels[2] | sels[3])
    pm = jnp.where(picked, 1.0, 0.0)
    r_io = lax.broadcasted_iota(jnp.int32, (tm, tm), 0)
    c_io = lax.broadcasted_iota(jnp.int32, (tm, tm), 1)
    earlier = jnp.where(r_io < c_io, 1.0, 0.0).astype(BF16)
    before = jnp.dot(pm.astype(BF16), earlier, preferred_element_type=F32)
    rank_ref[...] = jnp.concatenate(
        [jnp.sum(jnp.where(s, before, 0.0), axis=0, keepdims=True) for s in sels], axis=0).astype(jnp.int32)
    cnt_ref[0] = jnp.sum(pm, axis=1, keepdims=True).astype(jnp.int32)


def _outproj_router(x, ssm, yf, yb, attn, sguo, d_skip, g_out, w_out, g_ffn, w_r, b_r, tm):
    t = x.shape[0]
    row = lambda w: pl.BlockSpec((tm, w), lambda i: (i, 0))
    full = lambda a, b: pl.BlockSpec((a, b), lambda i: (0, 0))
    per_choice = pl.BlockSpec((TOP_K, tm), lambda i: (0, i))
    w_rt = w_r.astype(F32).T
    w_hi = w_rt.astype(BF16)
    w_lo = (w_rt - w_hi.astype(F32)).astype(BF16)
    return pl.pallas_call(
        functools.partial(_outproj_router_kernel, tm=tm),
        grid=(t // tm,),
        in_specs=[
            row(D_MODEL), row(2 * SSM_WIDTH), row(SSM_WIDTH), row(SSM_WIDTH), row(ATTN_WIDTH), row(SGU_WIDTH),
            full(1, SSM_WIDTH), full(1, D_MODEL), full(D_MODEL, D_MODEL), full(1, D_MODEL),
            full(N_EXPERTS, D_MODEL), full(N_EXPERTS, D_MODEL), full(N_EXPERTS, 1),
        ],
        out_specs=[row(D_MODEL), pl.BlockSpec((tm * TOKEN_TILE_ROWS, LANES), lambda i: (i, 0)),
                   per_choice, per_choice, per_choice, pl.BlockSpec((1, N_EXPERTS, 1), lambda i: (i, 0, 0))],
        out_shape=[
            jax.ShapeDtypeStruct((t, D_MODEL), F32),
            jax.ShapeDtypeStruct((t * TOKEN_TILE_ROWS, LANES), F32),
            jax.ShapeDtypeStruct((TOP_K, t), jnp.int32),
            jax.ShapeDtypeStruct((TOP_K, t), jnp.int32),
            jax.ShapeDtypeStruct((TOP_K, t), F32),
            jax.ShapeDtypeStruct((t // tm, N_EXPERTS, 1), jnp.int32),
        ],
        compiler_params=_cparams("arbitrary"),
        name="outproj_router",
    )(x, ssm, yf, yb, attn, sguo, d_skip, g_out, w_out, g_ffn, w_hi, w_lo, b_r.astype(F32).reshape(N_EXPERTS, 1))


ISSUE_UNROLL = 8
FETCH_ROWS = 16


def _for_rows(n, fn):
    def body(c, carry):
        for u in range(ISSUE_UNROLL):
            fn(c * ISSUE_UNROLL + u, u % 2)
        return carry

    lax.fori_loop(0, n // ISSUE_UNROLL, body, 0)


def _tile_rows(ref, first_tile, n_tiles=1):
    start = first_tile * TOKEN_TILE_ROWS
    if not isinstance(start, int):
        start = pl.multiple_of(start, TOKEN_TILE_ROWS)
    return ref.at[pl.ds(start, n_tiles * TOKEN_TILE_ROWS)]


EXPERT_COL_CHUNK = 256


def _experts_kernel(ltile_ref, otile_ref, exp_ref, lo_ref, hi_ref, tok_ref, tokn_ref, wrow_ref, h_hbm, wgu_ref,
                    bgu_ref, wd_ref, bd_ref, out_ref, x_buf, y_buf, wgu_bf, wd_bf, gsem, *, tm):
    w = pl.program_id(0)
    nw = pl.num_programs(0)
    slot = lax.rem(w, 2)
    other = 1 - slot
    prev = jnp.maximum(w - 1, 0)

    def gather_row(list_ref, to_slot, r, queue):
        pltpu.make_async_copy(_tile_rows(h_hbm, list_ref[0, 0, r]), _tile_rows(x_buf, to_slot * tm + r),
                              gsem.at[to_slot]).start(priority=queue)

    def wait_gather(of_slot):
        pltpu.make_async_copy(_tile_rows(h_hbm, 0, tm), _tile_rows(x_buf, of_slot * tm, tm), gsem.at[of_slot]).wait()

    @pl.when(w == 0)
    def _():
        _for_rows(tm, lambda r, queue: gather_row(tok_ref, 0, r, queue))

    @pl.when((w == 0) | (exp_ref[w] != exp_ref[prev]))
    def _():
        wgu_bf[...] = wgu_ref[0].astype(BF16)
        wd_bf[...] = wd_ref[0].astype(BF16)

    wait_gather(slot)

    n_pieces = (D_EXPERT + D_MODEL) // EXPERT_COL_CHUNK
    per_piece = tm // n_pieces
    piece = 0

    def issue_next_gather():
        nonlocal piece
        for r in range(piece * per_piece, (piece + 1) * per_piece):
            gather_row(tokn_ref, other, r, r % 2)
        piece += 1

    w_col = wrow_ref[:, :1]

    base = pl.multiple_of(slot * (tm * TOKEN_TILE_ROWS), tm * TOKEN_TILE_ROWS)
    x = _load_token_tiles(x_buf, base, tm).astype(BF16)
    acts = []
    for c in range(D_EXPERT // EXPERT_COL_CHUNK):
        issue_next_gather()
        g_cols = slice(c * EXPERT_COL_CHUNK, (c + 1) * EXPERT_COL_CHUNK)
        u_cols = slice(D_EXPERT + c * EXPERT_COL_CHUNK, D_EXPERT + (c + 1) * EXPERT_COL_CHUNK)
        gate = jnp.dot(x, wgu_bf[:, g_cols], preferred_element_type=F32) + bgu_ref[0, :, g_cols]
        up = jnp.dot(x, wgu_bf[:, u_cols], preferred_element_type=F32) + bgu_ref[0, :, u_cols]
        gate = jnp.minimum(gate, SWIGLU_LIMIT)
        up = jnp.clip(up, -SWIGLU_LIMIT, SWIGLU_LIMIT)
        acts.append(((up + 1.0) * gate * jax.nn.sigmoid(SWIGLU_ALPHA * gate)).astype(BF16))
    act = jnp.concatenate(acts, axis=1)
    tiles_per_chunk = EXPERT_COL_CHUNK // LANES
    for c in range(D_MODEL // EXPERT_COL_CHUNK):
        issue_next_gather()
        cols = slice(c * EXPERT_COL_CHUNK, (c + 1) * EXPERT_COL_CHUNK)
        y = (jnp.dot(act, wd_bf[:, cols], preferred_element_type=F32) + bd_ref[0, :, cols]) * w_col
        for j in range(tiles_per_chunk):
            y_buf[pl.ds(c * tiles_per_chunk + j, tm, stride=TOKEN_TILE_ROWS), :] = y[:, j * LANES:(j + 1) * LANES]

    first = (w == 0) | (otile_ref[w] != otile_ref[prev])

    @pl.when(first)
    def _():
        out_ref[...] = y_buf[...]

    @pl.when(jnp.logical_not(first))
    def _():
        row = lax.shift_right_logical(lax.broadcasted_iota(jnp.int32, y_buf.shape, 0),
                                      TOKEN_TILE_ROWS.bit_length() - 1)
        mine = (row >= lo_ref[w]) & (row < hi_ref[w])
        out_ref[...] = jnp.where(mine, y_buf[...], out_ref[...])

    @pl.when(w == nw - 1)
    def _():
        wait_gather(other)


def _experts(items, tok_sorted, w_sorted, h_tiles, wgu, bgu, wd, bd, layer, tm):
    nw = items[0].shape[0]
    n_rows = tok_sorted.shape[0]
    nt = n_rows // tm
    lists = lambda a: a.reshape(nt, 1, tm)
    cur = lambda i, lt, ot, ex, lo, hi: (lt[i], 0, 0)
    nxt = lambda i, lt, ot, ex, lo, hi: (lt[jnp.minimum(i + 1, nw - 1)], 0, 0)
    by_expert = lambda i, lt, ot, ex, lo, hi: (layer, ex[i], 0, 0)
    grid_spec = pltpu.PrefetchScalarGridSpec(
        num_scalar_prefetch=5,
        grid=(nw,),
        in_specs=[
            pl.BlockSpec((1, 1, tm), cur, memory_space=pltpu.SMEM),
            pl.BlockSpec((1, 1, tm), nxt, memory_space=pltpu.SMEM),
            pl.BlockSpec((tm, TOKEN_TILE_ROWS), lambda i, lt, ot, ex, lo, hi: (lt[i], 0)),
            pl.BlockSpec(memory_space=pl.ANY),
            pl.BlockSpec((None, 1, D_MODEL, 2 * D_EXPERT), by_expert),
            pl.BlockSpec((None, 1, 1, 2 * D_EXPERT), by_expert),
            pl.BlockSpec((None, 1, D_EXPERT, D_MODEL), by_expert),
            pl.BlockSpec((None, 1, 1, D_MODEL), by_expert),
        ],
        out_specs=pl.BlockSpec((tm * TOKEN_TILE_ROWS, LANES), lambda i, lt, ot, ex, lo, hi: (ot[i], 0)),
        scratch_shapes=[
            pltpu.VMEM((2 * tm * TOKEN_TILE_ROWS, LANES), F32),
            pltpu.VMEM((tm * TOKEN_TILE_ROWS, LANES), F32),
            pltpu.VMEM((D_MODEL, 2 * D_EXPERT), BF16),
            pltpu.VMEM((D_EXPERT, D_MODEL), BF16),
            pltpu.SemaphoreType.DMA((2,)),
        ],
    )
    return pl.pallas_call(
        functools.partial(_experts_kernel, tm=tm),
        grid_spec=grid_spec,
        out_shape=jax.ShapeDtypeStruct(((n_rows + FETCH_ROWS) * TOKEN_TILE_ROWS, LANES), F32),
        compiler_params=_cparams("arbitrary"),
        name="experts",
    )(*items, lists(tok_sorted), lists(tok_sorted),
      jnp.broadcast_to(w_sorted[:, None], (n_rows, TOKEN_TILE_ROWS)), h_tiles, wgu, bgu, wd, bd)


def _combine_kernel(src_ref, cst_ref, nch_ref, y_hbm, x1_ref, qpos_ref, g_ref, out_ref, yc_buf, sem, *, tm, nq, final):
    i = pl.program_id(0)

    @pl.when(i == 0)
    def _():
        yc_buf[...] = jnp.zeros_like(yc_buf)

    def chunk_copy(src_row, dst_row):
        return pltpu.make_async_copy(_tile_rows(y_hbm, src_row, FETCH_ROWS), _tile_rows(yc_buf, dst_row, FETCH_ROWS), sem)

    total = 0
    for e in range(N_EXPERTS):
        n = nch_ref[i * N_EXPERTS + e]
        src0 = src_ref[i * N_EXPERTS + e]
        dst0 = cst_ref[i * N_EXPERTS + e]

        def fetch(j, carry, src0=src0, dst0=dst0):
            chunk_copy(src0 + j * FETCH_ROWS, dst0 + j * FETCH_ROWS).start()
            return carry

        lax.fori_loop(0, n, fetch, 0)
        total = total + n

    def wait(j, carry):
        chunk_copy(0, 0).wait()
        return carry

    lax.fori_loop(0, total, wait, 0)

    q = lax.broadcasted_iota(jnp.int32, (tm, nq), 1)
    qpos = qpos_ref[...]
    hit = q == qpos[:, 0:1]
    for k in range(1, TOP_K):
        hit = hit | (q == qpos[:, k:k + 1])
    p = jnp.where(hit, 1.0, 0.0).astype(BF16)
    yc = _load_token_tiles(yc_buf, 0, nq).astype(BF16)
    x2 = x1_ref[...] + jnp.dot(p, yc, preferred_element_type=F32)
    if final:
        x2 = _rms(x2, g_ref[...])
    out_ref[...] = x2


def _combine(src0, cstart, nchunk, y_sorted, x1, qpos, g_final, tm, final):
    t = x1.shape[0]
    nq = -(-(tm * TOP_K + N_EXPERTS * (FETCH_ROWS - 1)) // LANES) * LANES
    grid_spec = pltpu.PrefetchScalarGridSpec(
        num_scalar_prefetch=3,
        grid=(t // tm,),
        in_specs=[
            pl.BlockSpec(memory_space=pl.ANY),
            pl.BlockSpec((tm, D_MODEL), lambda i, a, b, c: (i, 0)),
            pl.BlockSpec((tm, TOP_K), lambda i, a, b, c: (i, 0)),
            pl.BlockSpec((1, D_MODEL), lambda i, a, b, c: (0, 0)),
        ],
        out_specs=pl.BlockSpec((tm, D_MODEL), lambda i, a, b, c: (i, 0)),
        scratch_shapes=[pltpu.VMEM((nq * TOKEN_TILE_ROWS, LANES), F32), pltpu.SemaphoreType.DMA(())],
    )
    return pl.pallas_call(
        functools.partial(_combine_kernel, tm=tm, nq=nq, final=final),
        grid_spec=grid_spec,
        out_shape=jax.ShapeDtypeStruct((t, D_MODEL), F32),
        compiler_params=_cparams("arbitrary"),
        name="combine",
    )(src0.reshape(-1), cstart.reshape(-1), nchunk.reshape(-1), y_sorted, x1, qpos, g_final)


def _routing_lists(idx, lrank, wgt, tcnt, tm_tok, tm):
    t = idx.shape[1]
    n_rows = t * TOP_K
    nt = n_rows // tm
    cnt = jnp.sum(tcnt, axis=0)
    group_end = jnp.cumsum(cnt)
    group_start = group_end - cnt
    src0 = group_start[None, :] + jnp.cumsum(tcnt, axis=0) - tcnt
    aligned = -(-tcnt // FETCH_ROWS) * FETCH_ROWS
    cstart = jnp.cumsum(aligned, axis=1) - aligned
    nchunk = aligned // FETCH_ROWS
    experts = jnp.arange(N_EXPERTS, dtype=jnp.int32)
    onehot = idx.reshape(TOP_K, -1, tm_tok)[..., None] == experts
    pick = lambda table: jnp.sum(jnp.where(onehot, table[None, :, None, :], 0), axis=-1).reshape(TOP_K, t)
    pos = lrank + pick(src0)
    qpos = (lrank + pick(cstart)).T
    flat = jnp.arange(n_rows, dtype=jnp.int32)
    _, flat_sorted, w_sorted = lax.sort((pos.reshape(-1).astype(jnp.int32), flat, wgt.reshape(-1)), num_keys=1)
    tok_sorted = flat_sorted % t
    bounds = jnp.sort(jnp.concatenate([jnp.arange(nt, dtype=jnp.int32) * tm, group_start.astype(jnp.int32)]))
    nxt = jnp.concatenate([bounds[1:], jnp.array([n_rows], jnp.int32)])
    tile = jnp.minimum(bounds // tm, nt - 1)
    lo = bounds - tile * tm
    hi = jnp.where(nxt > bounds, nxt - tile * tm, lo)
    expert = jnp.minimum(jnp.sum(group_end[None, :] <= bounds[:, None], axis=-1), N_EXPERTS - 1)
    last = lambda a, v: jnp.concatenate([a, jnp.array([v], a.dtype)])
    items = (last(tile, nt - 1), last(tile, nt), jnp.concatenate([expert, expert[-1:]]), last(lo, 0), last(hi, 0))
    items = tuple(a.astype(jnp.int32) for a in items)
    to_i32 = lambda a: a.astype(jnp.int32)
    return items, tok_sorted, w_sorted, to_i32(src0), to_i32(cstart), to_i32(nchunk), to_i32(qpos)


def _block_flags(n_prompt_seq, seg_len, n_tokens):
    nb = n_tokens // BLOCK
    per_seg = seg_len // BLOCK
    b = jnp.arange(nb)
    prompt_blocks = n_prompt_seq * per_seg
    in_prompt = b < prompt_blocks
    first = jnp.where(in_prompt, b % per_seg == 0, b == prompt_blocks)
    last = jnp.where(in_prompt, b % per_seg == per_seg - 1, b == nb - 1)
    return jnp.stack([~first, ~last]).astype(jnp.int32)


def _trunk(x, n_prompt_seq, seg_len, p, tm, tm_moe, tc, nbq):
    t = x.shape[0]
    depth = p['w_in'].shape[0]
    flags = _block_flags(n_prompt_seq, seg_len, t)
    bias = _band_bias(p['rel_bias'])
    zero_init = jnp.zeros((2, 1, N_SEG, SSM_REAL), F32)
    for layer in range(depth):
        ssm, qkv, sgu = _inproj(x, p['norm_mix'][layer][None], p['w_in'][layer].astype(BF16), tm)

        lam_bar, b_blk, a_rows, c_blk = _ssm_params(
            p['ssm_lam_re'][layer], p['ssm_lam_im'][layer], p['ssm_log_step'][layer], p['ssm_b_re'][layer],
            p['ssm_b_im'][layer], p['ssm_c_re'][layer], p['ssm_c_im'][layer])
        _, fin = _ssm_scan(ssm, seg_len, 1, 1, b_blk, a_rows, c_blk, zero_init, tc, False)
        init = jnp.concatenate([zero_init, _chain_init(fin[:, 0], lam_bar ** seg_len)[:, None]], axis=1)
        y, _ = _ssm_scan(ssm, seg_len, 0, 2, b_blk, a_rows, c_blk, init, tc, True)

        sink = jnp.repeat(p['attn_sink'][layer].astype(F32), BLOCK).reshape(N_KV_HEADS, Q_PER_KV * BLOCK, 1)
        sgb = jnp.repeat(p['sgu_b'][layer].astype(F32).T, SGU_GROUP_WIDTH, axis=1)
        attn, sguo = _attn_sgu(flags, qkv, sgu, bias, sink, p['sgu_norm'][layer][None].astype(F32),
                               p['sgu_w'][layer].astype(BF16), sgb, nbq)

        x1, h, idx, rank, wgt, cnt = _outproj_router(
            x, ssm, y[0], y[1], attn, sguo, p['ssm_d'][layer][None].astype(F32), p['out_norm'][layer][None],
            p['w_out'][layer].astype(BF16), p['norm_ffn'][layer][None], p['w_router'][layer].astype(F32),
            p['b_router'][layer][None].astype(F32), tm)

        items, tok_sorted, w_sorted, src0, cstart, nchunk, qpos = _routing_lists(
            idx, rank, wgt, cnt[:, :, 0], tm, tm_moe)
        y_sorted = _experts(items, tok_sorted, w_sorted, h, p['w_gate_up'].astype(F32),
                            p['b_gate_up'][:, :, None].astype(F32), p['w_down'].astype(F32),
                            p['b_down'][:, :, None].astype(F32), layer, tm_moe)
        x = _combine(src0, cstart, nchunk, y_sorted, x1, qpos, p['final_norm'][None].astype(F32), tm,
                     layer == depth - 1)
    return x


def _run(x_prompt, x_sample, p, tm=512, tm_moe=512, tc=64, nbq=2):
    bsz, seg_len, _ = x_prompt.shape
    assert bsz == N_SEG and x_sample.shape[0] == 1 and x_sample.shape[1] == N_SEG * seg_len
    assert seg_len % (nbq * BLOCK) == 0 and seg_len % tc == 0
    x = jnp.concatenate([x_prompt.reshape(-1, D_MODEL), x_sample.reshape(-1, D_MODEL)], axis=0).astype(F32)
    y = _trunk(x, bsz, seg_len, p, tm, tm_moe, tc, nbq)
    t_prompt = bsz * seg_len
    return y[:t_prompt].reshape(x_prompt.shape), y[t_prompt:].reshape(x_sample.shape)


def kernel(x_prompt, x_sample, norm_mix, w_in, ssm_lam_re, ssm_lam_im, ssm_log_step, ssm_b_re, ssm_b_im, ssm_c_re, ssm_c_im, ssm_d, attn_sink, rel_bias, sgu_norm, sgu_w, sgu_b, out_norm, w_out, norm_ffn, w_router, b_router, w_gate_up, b_gate_up, w_down, b_down, final_norm):
    p = dict(norm_mix=norm_mix, w_in=w_in, ssm_lam_re=ssm_lam_re, ssm_lam_im=ssm_lam_im, ssm_log_step=ssm_log_step,
             ssm_b_re=ssm_b_re, ssm_b_im=ssm_b_im, ssm_c_re=ssm_c_re, ssm_c_im=ssm_c_im, ssm_d=ssm_d,
             attn_sink=attn_sink, rel_bias=rel_bias, sgu_norm=sgu_norm, sgu_w=sgu_w, sgu_b=sgu_b, out_norm=out_norm,
             w_out=w_out, norm_ffn=norm_ffn, w_router=w_router, b_router=b_router, w_gate_up=w_gate_up,
             b_gate_up=b_gate_up, w_down=w_down, b_down=b_down, final_norm=final_norm)
    return _run(x_prompt, x_sample, p)
```

```python
import functools
import math

import jax
import jax.numpy as jnp
from jax import lax
from jax.experimental import pallas as pl
from jax.experimental.pallas import tpu as pltpu

D_MODEL = 1024
SSM_WIDTH = 256
SSM_GROUP = 16
SSM_GROUPS = 16
SSM_STATE = 64
SSM_COMPLEX = SSM_GROUPS * SSM_STATE
SSM_REAL = 2 * SSM_COMPLEX
ATTN_WIDTH = 512
HEAD_DIM = 64
N_Q_HEADS = 8
N_KV_HEADS = 2
Q_PER_KV = 4
KV_WIDTH = 128
BLOCK = 128
N_BUCKETS = 32
MAX_DISTANCE = 128
SGU_WIDTH = 256
SGU_GROUPS = 4
SGU_GROUP_WIDTH = 64
IN_COLS = 1792
QKV_COLS = ATTN_WIDTH + 2 * KV_WIDTH
N_EXPERTS = 32
TOP_K = 4
D_EXPERT = 1024
SWIGLU_LIMIT = 7.0
SWIGLU_ALPHA = 1.702
EPS = 1e-6
NEG_INF = -1e30

N_SEG = 8
V7X_VMEM_LIMIT = 56 * 1024 * 1024

F32 = jnp.float32
BF16 = jnp.bfloat16


def _rms(xf, g):
    return xf * lax.rsqrt(jnp.mean(xf * xf, axis=-1, keepdims=True) + EPS) * g


LANES = 128
TOKEN_TILE_ROWS = D_MODEL // LANES


def _load_token_tiles(ref, base, n):
    return jnp.concatenate(
        [ref[pl.ds(base + j, n, stride=TOKEN_TILE_ROWS), :] for j in range(TOKEN_TILE_ROWS)], axis=1)


def _store_token_tiles(ref, base, val, n):
    for j in range(TOKEN_TILE_ROWS):
        ref[pl.ds(base + j, n, stride=TOKEN_TILE_ROWS), :] = val[:, j * LANES:(j + 1) * LANES]


def _cparams(*sem):
    return pltpu.CompilerParams(dimension_semantics=sem, vmem_limit_bytes=V7X_VMEM_LIMIT)


def _part_specs(parts, tm):
    per = parts[0].shape[0] // tm
    return [pl.BlockSpec((tm, D_MODEL), (lambda j: lambda i, *_: (jnp.clip(i - j * per, 0, per - 1), 0))(j))
            for j in range(len(parts))]


def _pick_part(x_refs, steps_per_part):
    i = pl.program_id(0)
    x = x_refs[0][...]
    for j in range(1, len(x_refs)):
        x = jnp.where(i >= j * steps_per_part, x_refs[j][...], x)
    return x


def _inproj_kernel(*refs, n_parts, steps_per_part):
    x_refs, (g_ref, w_ref, ssm_ref, qkv_ref, sgu_ref) = refs[:n_parts], refs[n_parts:]
    h = _rms(_pick_part(x_refs, steps_per_part), g_ref[...]).astype(BF16)
    p = jnp.dot(h, w_ref[...], preferred_element_type=F32)
    ssm_ref[...] = p[:, :2 * SSM_WIDTH].astype(BF16)
    qkv_ref[...] = p[:, 2 * SSM_WIDTH:2 * SSM_WIDTH + QKV_COLS].astype(BF16)
    sgu_ref[...] = p[:, 2 * SSM_WIDTH + QKV_COLS:].astype(BF16)


def _inproj(x_parts, g, w, tm):
    t = sum(x.shape[0] for x in x_parts)
    return pl.pallas_call(
        functools.partial(_inproj_kernel, n_parts=len(x_parts), steps_per_part=x_parts[0].shape[0] // tm),
        grid=(t // tm,),
        in_specs=_part_specs(x_parts, tm) + [
            pl.BlockSpec((1, D_MODEL), lambda i: (0, 0)),
            pl.BlockSpec((D_MODEL, IN_COLS), lambda i: (0, 0)),
        ],
        out_specs=[
            pl.BlockSpec((tm, 2 * SSM_WIDTH), lambda i: (i, 0)),
            pl.BlockSpec((tm, QKV_COLS), lambda i: (i, 0)),
            pl.BlockSpec((tm, 2 * SGU_WIDTH), lambda i: (i, 0)),
        ],
        out_shape=[
            jax.ShapeDtypeStruct((t, 2 * SSM_WIDTH), BF16),
            jax.ShapeDtypeStruct((t, QKV_COLS), BF16),
            jax.ShapeDtypeStruct((t, 2 * SGU_WIDTH), BF16),
        ],
        compiler_params=_cparams("arbitrary"),
        name="inproj",
    )(*x_parts, g, w)


SCAN_COL_BLOCKS = 2


def _ssm_kernel(u_ref, b_ref, a_ref, c_ref, init_ref, *rest, tc, emit_y):
    if emit_y:
        y_ref, fin_ref, bu_scr, st_scr, tm_lo, tm_hi = rest
    else:
        fin_ref, bu_scr, st_scr, tm_lo, tm_hi = rest
    d = pl.program_id(0)
    c = pl.program_id(2)
    halves = ((tm_lo, slice(0, LANES)), (tm_hi, slice(LANES, 2 * LANES)))

    @pl.when(c == 0)
    def _():
        st_scr[...] = init_ref[0, 0]

    for s in range(N_SEG):
        u_s = u_ref[s].astype(F32)
        for scr, cols in halves:
            scr[pl.ds(s, tc, stride=N_SEG), :] = u_s[:, cols]
    u_tm = jnp.concatenate([tm_lo[...], tm_hi[...]], axis=1).astype(BF16)
    bu_scr[...] = jnp.dot(u_tm, b_ref[0], preferred_element_type=F32)
    w = SSM_COMPLEX // SCAN_COL_BLOCKS
    for cb in range(SCAN_COL_BLOCKS):
        re_cols = slice(cb * w, (cb + 1) * w)
        im_cols = slice(SSM_COMPLEX + cb * w, SSM_COMPLEX + (cb + 1) * w)
        a_re = jnp.broadcast_to(a_ref[0, 0:1, re_cols], (N_SEG, w))
        a_im = jnp.broadcast_to(a_ref[0, 1:2, re_cols], (N_SEG, w))

        def body(tt, carry, re_cols=re_cols, im_cols=im_cols, a_re=a_re, a_im=a_im):
            x_re, x_im = carry
            t = jnp.where(d == 0, tt, tc - 1 - tt)
            rows = pl.ds(pl.multiple_of(t * N_SEG, N_SEG), N_SEG)
            n_re = a_re * x_re - a_im * x_im + bu_scr[rows, re_cols]
            n_im = a_re * x_im + a_im * x_re + bu_scr[rows, im_cols]
            bu_scr[rows, re_cols] = n_re
            bu_scr[rows, im_cols] = n_im
            return n_re, n_im

        x_re, x_im = lax.fori_loop(0, tc, body, (st_scr[:, re_cols], st_scr[:, im_cols]), unroll=4)
        st_scr[:, re_cols] = x_re
        st_scr[:, im_cols] = x_im

    if emit_y:
        y = jnp.dot(bu_scr[...].astype(BF16), c_ref[...], preferred_element_type=F32)
        for scr, cols in halves:
            scr[...] = y[:, cols]
        for s in range(N_SEG):
            for scr, cols in halves:
                y_ref[0, s, :, cols] = scr[pl.ds(s, tc, stride=N_SEG), :]

    @pl.when(c == pl.num_programs(2) - 1)
    def _():
        fin_ref[0, 0] = st_scr[...]


def _ssm_scan(ssm_proj, seg_len, first_group, n_groups, b_blk, a_rows, c_blk, init, tc, emit_y):
    u = ssm_proj.reshape(-1, seg_len, 2 * SSM_WIDTH)
    nc = seg_len // tc

    def chunk(d, c):
        return jnp.where(d == 0, c, nc - 1 - c)

    out_specs = [pl.BlockSpec((1, 1, N_SEG, SSM_REAL), lambda d, g, c: (d, g, 0, 0))]
    out_shape = [jax.ShapeDtypeStruct((2, n_groups, N_SEG, SSM_REAL), F32)]
    if emit_y:
        out_specs = [pl.BlockSpec((1, N_SEG, tc, SSM_WIDTH), lambda d, g, c: (d, g, chunk(d, c), 0))] + out_specs
        out_shape = [jax.ShapeDtypeStruct((2, n_groups * N_SEG, seg_len, SSM_WIDTH), F32)] + out_shape
    res = pl.pallas_call(
        functools.partial(_ssm_kernel, tc=tc, emit_y=emit_y),
        grid=(2, n_groups, nc),
        in_specs=[
            pl.BlockSpec((N_SEG, tc, SSM_WIDTH), lambda d, g, c: (first_group + g, chunk(d, c), 0)),
            pl.BlockSpec((1, SSM_WIDTH, SSM_REAL), lambda d, g, c: (d, 0, 0)),
            pl.BlockSpec((1, 2, SSM_COMPLEX), lambda d, g, c: (d, 0, 0)),
            pl.BlockSpec((SSM_REAL, SSM_WIDTH), lambda d, g, c: (0, 0)),
            pl.BlockSpec((1, 1, N_SEG, SSM_REAL), lambda d, g, c: (d, g, 0, 0)),
        ],
        out_specs=out_specs,
        out_shape=out_shape,
        scratch_shapes=[pltpu.VMEM((tc * N_SEG, SSM_REAL), F32), pltpu.VMEM((N_SEG, SSM_REAL), F32),
                        pltpu.VMEM((tc * N_SEG, LANES), F32), pltpu.VMEM((tc * N_SEG, LANES), F32)],
        compiler_params=_cparams("arbitrary", "arbitrary", "arbitrary"),
        name="ssm_scan_y" if emit_y else "ssm_scan_state",
    )(u, b_blk, a_rows, c_blk, init)
    if emit_y:
        return res[0].reshape(2, -1, SSM_WIDTH), res[1]
    return None, res[0]


def _ssm_params(lam_re, lam_im, log_step, b_re, b_im, c_re, c_im):
    lam = lax.complex(lam_re.astype(F32), lam_im.astype(F32))
    step = jnp.exp(log_step.astype(F32))[..., None]
    lam_bar = jnp.exp(lam * step)
    b = lax.complex(b_re.astype(F32), b_im.astype(F32))
    b_bar = ((lam_bar - 1.0) / lam)[..., None] * b
    eye = jnp.eye(SSM_GROUPS, dtype=F32)
    b_blk_re = jnp.einsum('dgph,gk->dghkp', jnp.real(b_bar), eye).reshape(2, SSM_WIDTH, SSM_COMPLEX)
    b_blk_im = jnp.einsum('dgph,gk->dghkp', jnp.imag(b_bar), eye).reshape(2, SSM_WIDTH, SSM_COMPLEX)
    b_blk = jnp.concatenate([b_blk_re, b_blk_im], axis=-1).astype(BF16)
    c_blk_re = jnp.einsum('ghp,gk->gpkh', c_re.astype(F32), eye).reshape(SSM_COMPLEX, SSM_WIDTH)
    c_blk_im = jnp.einsum('ghp,gk->gpkh', c_im.astype(F32), eye).reshape(SSM_COMPLEX, SSM_WIDTH)
    c_blk = jnp.concatenate([c_blk_re, -c_blk_im], axis=0).astype(BF16)
    a_rows = jnp.stack([jnp.real(lam_bar).reshape(2, SSM_COMPLEX),
                        jnp.imag(lam_bar).reshape(2, SSM_COMPLEX)], axis=1)
    return lam_bar.reshape(2, SSM_COMPLEX), b_blk, a_rows, c_blk


def _chain_init(fin, lam_pow):
    f = lax.complex(fin[..., :SSM_COMPLEX], fin[..., SSM_COMPLEX:])
    zero = jnp.zeros_like(f[0, 0])
    fwd = [zero]
    for s in range(1, N_SEG):
        fwd.append(lam_pow[0] * fwd[-1] + f[0, s - 1])
    bwd = [zero]
    for s in range(N_SEG - 2, -1, -1):
        bwd.append(lam_pow[1] * bwd[-1] + f[1, s + 1])
    init = jnp.stack([jnp.stack(fwd), jnp.stack(bwd[::-1])])
    return jnp.concatenate([jnp.real(init), jnp.imag(init)], axis=-1).astype(F32)


def _attn_sgu_kernel(flags_ref, q_ref, kp_ref, kc_ref, kn_ref, vp_ref, vc_ref, vn_ref, bias_ref, sink_ref,
                     sgu_ref, sgn_ref, sgw_ref, sgb_ref, attn_ref, sguo_ref, *, nbq):
    n = pl.program_id(0)
    no_prev = flags_ref[0, n * nbq] == 0
    no_next = flags_ref[1, n * nbq + nbq - 1] == 0
    def key_block(prev_ref, cur_ref, next_ref, i):
        if i == 0:
            return prev_ref[...]
        if i == nbq + 1:
            return next_ref[...]
        return cur_ref[(i - 1) * BLOCK:i * BLOCK, :]

    for b in range(nbq):
        rows = slice(b * BLOCK, (b + 1) * BLOCK)
        q = q_ref[rows, :]
        ks = [key_block(kp_ref, kc_ref, kn_ref, b + i) for i in range(3)]
        vs = [key_block(vp_ref, vc_ref, vn_ref, b + i) for i in range(3)]
        outs = []
        for j in range(N_KV_HEADS):
            qs = jnp.concatenate(
                [q[:, (j * Q_PER_KV + i) * HEAD_DIM:(j * Q_PER_KV + i + 1) * HEAD_DIM] for i in range(Q_PER_KV)],
                axis=0)
            kv_cols = slice(j * HEAD_DIM, (j + 1) * HEAD_DIM)
            s = []
            for i in range(3):
                si = lax.dot_general(qs, ks[i][:, kv_cols], (((1,), (1,)), ((), ())), preferred_element_type=F32)
                si = si * (HEAD_DIM ** -0.5) + bias_ref[j, :, i * BLOCK:(i + 1) * BLOCK]
                if b == 0 and i == 0:
                    si = jnp.where(no_prev, NEG_INF, si)
                if b == nbq - 1 and i == 2:
                    si = jnp.where(no_next, NEG_INF, si)
                s.append(si)
            sink = sink_ref[j]
            m = jnp.max(jnp.maximum(jnp.maximum(s[0], s[1]), s[2]), axis=-1, keepdims=True)
            m = jnp.maximum(m, sink)
            p = [jnp.exp(si - m) for si in s]
            denom = jnp.sum(p[0] + p[1] + p[2], axis=-1, keepdims=True) + jnp.exp(sink - m)
            o = sum(jnp.dot(p[i].astype(BF16), vs[i][:, kv_cols], preferred_element_type=F32) for i in range(3))
            o = o / denom
            outs += [o[i * BLOCK:(i + 1) * BLOCK] for i in range(Q_PER_KV)]
        attn_ref[rows, :] = jnp.concatenate(outs, axis=1).astype(BF16)

        gu = jax.nn.gelu(sgu_ref[rows, :SGU_WIDTH].astype(F32))
        gv = _rms(jax.nn.gelu(sgu_ref[rows, SGU_WIDTH:].astype(F32)), sgn_ref[...]).astype(BF16)
        mixed = jnp.concatenate(
            [jnp.dot(sgw_ref[g], gv[:, g * SGU_GROUP_WIDTH:(g + 1) * SGU_GROUP_WIDTH], preferred_element_type=F32)
             for g in range(SGU_GROUPS)], axis=1) + sgb_ref[...]
        sguo_ref[rows, :] = (gu * mixed).astype(BF16)


def _attn_sgu(flags, qkv, sgu, bias, sink, sgn, sgw, sgb, nbq):
    t = qkv.shape[0]
    nb = t // BLOCK
    tile = nbq * BLOCK
    kcol = ATTN_WIDTH // KV_WIDTH
    vcol = kcol + 1
    prev = lambda col: pl.BlockSpec((BLOCK, KV_WIDTH), lambda n, f: (jnp.maximum(n * nbq - 1, 0), col))
    cur = lambda col: pl.BlockSpec((tile, KV_WIDTH), lambda n, f: (n, col))
    nxt = lambda col: pl.BlockSpec((BLOCK, KV_WIDTH), lambda n, f: (jnp.minimum((n + 1) * nbq, nb - 1), col))
    grid_spec = pltpu.PrefetchScalarGridSpec(
        num_scalar_prefetch=1,
        grid=(nb // nbq,),
        in_specs=[
            pl.BlockSpec((tile, ATTN_WIDTH), lambda n, f: (n, 0)),
            prev(kcol), cur(kcol), nxt(kcol),
            prev(vcol), cur(vcol), nxt(vcol),
            pl.BlockSpec((N_KV_HEADS, Q_PER_KV * BLOCK, 3 * BLOCK), lambda n, f: (0, 0, 0)),
            pl.BlockSpec((N_KV_HEADS, Q_PER_KV * BLOCK, 1), lambda n, f: (0, 0, 0)),
            pl.BlockSpec((tile, 2 * SGU_WIDTH), lambda n, f: (n, 0)),
            pl.BlockSpec((1, SGU_WIDTH), lambda n, f: (0, 0)),
            pl.BlockSpec((SGU_GROUPS, BLOCK, BLOCK), lambda n, f: (0, 0, 0)),
            pl.BlockSpec((BLOCK, SGU_WIDTH), lambda n, f: (0, 0)),
        ],
        out_specs=[
            pl.BlockSpec((tile, ATTN_WIDTH), lambda n, f: (n, 0)),
            pl.BlockSpec((tile, SGU_WIDTH), lambda n, f: (n, 0)),
        ],
    )
    return pl.pallas_call(
        functools.partial(_attn_sgu_kernel, nbq=nbq),
        grid_spec=grid_spec,
        out_shape=[jax.ShapeDtypeStruct((t, ATTN_WIDTH), BF16), jax.ShapeDtypeStruct((t, SGU_WIDTH), BF16)],
        compiler_params=_cparams("arbitrary"),
        name="attn_sgu",
    )(flags, qkv, qkv, qkv, qkv, qkv, qkv, qkv, bias, sink, sgu, sgn, sgw, sgb)


def _t5_bucket(rel):
    half = N_BUCKETS // 2
    max_exact = half // 2
    ret = jnp.where(rel > 0, half, 0)
    n = jnp.abs(rel)
    nf = jnp.maximum(n, 1).astype(F32)
    large = max_exact + (jnp.log(nf / max_exact) / math.log(MAX_DISTANCE / max_exact)
                         * (half - max_exact)).astype(jnp.int32)
    large = jnp.minimum(large, half - 1)
    return ret + jnp.where(n < max_exact, n, large)


def _band_bias(rel_table):
    q_pos = jnp.arange(BLOCK)[:, None]
    k_pos = jnp.arange(3 * BLOCK)[None, :] - BLOCK
    rel = k_pos - q_pos
    hit = _t5_bucket(rel)[..., None] == jnp.arange(N_BUCKETS)
    bias = jnp.sum(jnp.where(hit[..., None], rel_table.astype(F32), 0.0), axis=2)
    bias = jnp.where((jnp.abs(rel) <= BLOCK)[..., None], bias, NEG_INF)
    return bias.transpose(2, 0, 1).reshape(N_KV_HEADS, Q_PER_KV * BLOCK, 3 * BLOCK)


def _outproj_router_kernel(*refs, tm, n_parts, steps_per_part):
    x_refs = refs[:n_parts]
    (ssm_ref, yf_ref, yb_ref, attn_ref, sguo_ref, d_ref, gout_ref, wout_ref, gffn_ref, wrh_ref, wrl_ref, br_ref,
     x1_ref, h_ref, idx_ref, rank_ref, wgt_ref, cnt_ref) = refs[n_parts:]
    u = ssm_ref[:, :SSM_WIDTH].astype(F32)
    gate = ssm_ref[:, SSM_WIDTH:].astype(F32)
    ya = yf_ref[...] + yb_ref[...] + d_ref[...] * u
    ya = jax.nn.gelu(ya) * jax.nn.sigmoid(gate)
    g = gout_ref[...]
    mixed = jnp.concatenate([
        _rms(ya, g[:, :SSM_WIDTH]),
        _rms(attn_ref[...].astype(F32), g[:, SSM_WIDTH:SSM_WIDTH + ATTN_WIDTH]),
        _rms(sguo_ref[...].astype(F32), g[:, SSM_WIDTH + ATTN_WIDTH:]),
    ], axis=1).astype(BF16)
    x1 = _pick_part(x_refs, steps_per_part) + jnp.dot(mixed, wout_ref[...], preferred_element_type=F32)
    x1_ref[...] = x1
    h = _rms(x1, gffn_ref[...])
    _store_token_tiles(h_ref, 0, h, tm)

    h_hi = h.astype(BF16)
    h_lo = (h - h_hi.astype(F32)).astype(BF16)
    nt_dot = lambda a, b: lax.dot_general(a, b, (((1,), (1,)), ((), ())), preferred_element_type=F32)
    lg = nt_dot(wrh_ref[...], h_hi) + nt_dot(wrl_ref[...], h_hi) + nt_dot(wrh_ref[...], h_lo) + br_ref[...]
    row = lax.broadcasted_iota(jnp.int32, (N_EXPERTS, tm), 0)
    vals, idxs, sels = [], [], []
    for _ in range(TOP_K):
        m = jnp.max(lg, axis=0, keepdims=True)
        idx = jnp.min(jnp.where(lg == m, row, N_EXPERTS), axis=0, keepdims=True)
        sel = row == idx
        lg = jnp.where(sel, -jnp.inf, lg)
        vals.append(m)
        idxs.append(idx)
        sels.append(sel)
    e = [jnp.exp(v - vals[0]) for v in vals]
    tot = e[0] + e[1] + e[2] + e[3]
    wgt_ref[...] = jnp.concatenate([ek / tot for ek in e], axis=0)
    idx_ref[...] = jnp.concatenate(idxs, axis=0)

    picked = (sels[0] | sels[1] | sels[2] | sels[3])
    pm = jnp.where(picked, 1.0, 0.0)
    r_io = lax.broadcasted_iota(jnp.int32, (tm, tm), 0)
    c_io = lax.broadcasted_iota(jnp.int32, (tm, tm), 1)
    earlier = jnp.where(r_io < c_io, 1.0, 0.0).astype(BF16)
    before = jnp.dot(pm.astype(BF16), earlier, preferred_element_type=F32)
    rank_ref[...] = jnp.concatenate(
        [jnp.sum(jnp.where(s, before, 0.0), axis=0, keepdims=True) for s in sels], axis=0).astype(jnp.int32)
    cnt_ref[0] = jnp.sum(pm, axis=1, keepdims=True).astype(jnp.int32)


def _outproj_router(x_parts, ssm, yf, yb, attn, sguo, d_skip, g_out, w_out, g_ffn, w_r, b_r, tm):
    t = ssm.shape[0]
    row = lambda w: pl.BlockSpec((tm, w), lambda i: (i, 0))
    full = lambda a, b: pl.BlockSpec((a, b), lambda i: (0, 0))
    per_choice = pl.BlockSpec((TOP_K, tm), lambda i: (0, i))
    w_rt = w_r.astype(F32).T
    w_hi = w_rt.astype(BF16)
    w_lo = (w_rt - w_hi.astype(F32)).astype(BF16)
    return pl.pallas_call(
        functools.partial(_outproj_router_kernel, tm=tm, n_parts=len(x_parts),
                          steps_per_part=x_parts[0].shape[0] // tm),
        grid=(t // tm,),
        in_specs=_part_specs(x_parts, tm) + [
            row(2 * SSM_WIDTH), row(SSM_WIDTH), row(SSM_WIDTH), row(ATTN_WIDTH), row(SGU_WIDTH),
            full(1, SSM_WIDTH), full(1, D_MODEL), full(D_MODEL, D_MODEL), full(1, D_MODEL),
            full(N_EXPERTS, D_MODEL), full(N_EXPERTS, D_MODEL), full(N_EXPERTS, 1),
        ],
        out_specs=[row(D_MODEL), pl.BlockSpec((tm * TOKEN_TILE_ROWS, LANES), lambda i: (i, 0)),
                   per_choice, per_choice, per_choice, pl.BlockSpec((1, N_EXPERTS, 1), lambda i: (i, 0, 0))],
        out_shape=[
            jax.ShapeDtypeStruct((t, D_MODEL), F32),
            jax.ShapeDtypeStruct((t * TOKEN_TILE_ROWS, LANES), F32),
            jax.ShapeDtypeStruct((TOP_K, t), jnp.int32),
            jax.ShapeDtypeStruct((TOP_K, t), jnp.int32),
            jax.ShapeDtypeStruct((TOP_K, t), F32),
            jax.ShapeDtypeStruct((t // tm, N_EXPERTS, 1), jnp.int32),
        ],
        compiler_params=_cparams("arbitrary"),
        name="outproj_router",
    )(*x_parts, ssm, yf, yb, attn, sguo, d_skip, g_out, w_out, g_ffn, w_hi, w_lo,
      b_r.astype(F32).reshape(N_EXPERTS, 1))


ISSUE_UNROLL = 8
FETCH_ROWS = 16


def _for_rows(n, fn):
    def body(c, carry):
        for u in range(ISSUE_UNROLL):
            fn(c * ISSUE_UNROLL + u, u % 2)
        return carry

    lax.fori_loop(0, n // ISSUE_UNROLL, body, 0)


def _tile_rows(ref, first_tile, n_tiles=1):
    start = first_tile * TOKEN_TILE_ROWS
    if not isinstance(start, int):
        start = pl.multiple_of(start, TOKEN_TILE_ROWS)
    return ref.at[pl.ds(start, n_tiles * TOKEN_TILE_ROWS)]


EXPERT_COL_CHUNK = 256


GATHER_SLOTS = 3


def _experts_kernel(ltile_ref, otile_ref, exp_ref, lo_ref, hi_ref, tok_ref, tok1_ref, tok2_ref, wrow_ref, h_hbm, wgu_ref,
                    bgu_ref, wd_ref, bd_ref, out_ref, x_buf, y_buf, wgu_bf, wd_bf, gsem, *, tm):
    w = pl.program_id(0)
    nw = pl.num_programs(0)
    slot = lax.rem(w, GATHER_SLOTS)
    ahead = lax.rem(w + GATHER_SLOTS - 1, GATHER_SLOTS)
    prev = jnp.maximum(w - 1, 0)

    def gather_row(list_ref, to_slot, r, queue):
        pltpu.make_async_copy(_tile_rows(h_hbm, list_ref[0, 0, r]), _tile_rows(x_buf, to_slot * tm + r),
                              gsem.at[to_slot]).start(priority=queue)

    def wait_gather(of_slot):
        pltpu.make_async_copy(_tile_rows(h_hbm, 0, tm), _tile_rows(x_buf, of_slot * tm, tm), gsem.at[of_slot]).wait()

    @pl.when(w == 0)
    def _():
        _for_rows(tm, lambda r, queue: gather_row(tok_ref, 0, r, queue))
        _for_rows(tm, lambda r, queue: gather_row(tok1_ref, 1, r, queue))

    @pl.when((w == 0) | (exp_ref[w] != exp_ref[prev]))
    def _():
        wgu_bf[...] = wgu_ref[0].astype(BF16)
        wd_bf[...] = wd_ref[0].astype(BF16)

    wait_gather(slot)

    n_pieces = (D_EXPERT + D_MODEL) // EXPERT_COL_CHUNK
    per_piece = tm // n_pieces
    piece = 0

    def issue_next_gather():
        nonlocal piece
        for r in range(piece * per_piece, (piece + 1) * per_piece):
            gather_row(tok2_ref, ahead, r, r % 2)
        piece += 1

    w_rows = jnp.transpose(jnp.broadcast_to(wrow_ref[0], (LANES, tm)))
    w_rows = jnp.concatenate([w_rows] * (EXPERT_COL_CHUNK // LANES), axis=1)

    base = pl.multiple_of(slot * (tm * TOKEN_TILE_ROWS), tm * TOKEN_TILE_ROWS)
    x = _load_token_tiles(x_buf, base, tm).astype(BF16)
    acts = []
    for c in range(D_EXPERT // EXPERT_COL_CHUNK):
        issue_next_gather()
        g_cols = slice(c * EXPERT_COL_CHUNK, (c + 1) * EXPERT_COL_CHUNK)
        u_cols = slice(D_EXPERT + c * EXPERT_COL_CHUNK, D_EXPERT + (c + 1) * EXPERT_COL_CHUNK)
        gate = jnp.dot(x, wgu_bf[:, g_cols], preferred_element_type=F32) + bgu_ref[0, :, g_cols]
        up = jnp.dot(x, wgu_bf[:, u_cols], preferred_element_type=F32) + bgu_ref[0, :, u_cols]
        gate = jnp.minimum(gate, SWIGLU_LIMIT)
        up = jnp.clip(up, -SWIGLU_LIMIT, SWIGLU_LIMIT)
        acts.append(((up + 1.0) * gate * jax.nn.sigmoid(SWIGLU_ALPHA * gate)).astype(BF16))
    act = jnp.concatenate(acts, axis=1)
    tiles_per_chunk = EXPERT_COL_CHUNK // LANES
    for c in range(D_MODEL // EXPERT_COL_CHUNK):
        issue_next_gather()
        cols = slice(c * EXPERT_COL_CHUNK, (c + 1) * EXPERT_COL_CHUNK)
        y = (jnp.dot(act, wd_bf[:, cols], preferred_element_type=F32) + bd_ref[0, :, cols]) * w_rows
        for j in range(tiles_per_chunk):
            y_buf[pl.ds(c * tiles_per_chunk + j, tm, stride=TOKEN_TILE_ROWS), :] = y[:, j * LANES:(j + 1) * LANES]

    first = (w == 0) | (otile_ref[w] != otile_ref[prev])

    @pl.when(first)
    def _():
        out_ref[...] = y_buf[...]

    @pl.when(jnp.logical_not(first))
    def _():
        row = lax.shift_right_logical(lax.broadcasted_iota(jnp.int32, y_buf.shape, 0),
                                      TOKEN_TILE_ROWS.bit_length() - 1)
        mine = (row >= lo_ref[w]) & (row < hi_ref[w])
        out_ref[...] = jnp.where(mine, y_buf[...], out_ref[...])

    @pl.when(w == nw - 1)
    def _():
        for k in range(1, GATHER_SLOTS):
            wait_gather(lax.rem(w + k, GATHER_SLOTS))


def _experts(items, tok_sorted, w_sorted, h_tiles, wgu, bgu, wd, bd, layer, tm):
    nw = items[0].shape[0]
    n_rows = tok_sorted.shape[0]
    nt = n_rows // tm
    lists = lambda a: a.reshape(nt, 1, tm)
    later = lambda k: (lambda i, lt, ot, ex, lo, hi: (lt[jnp.minimum(i + k, nw - 1)], 0, 0))
    by_expert = lambda i, lt, ot, ex, lo, hi: (layer, ex[i], 0, 0)
    grid_spec = pltpu.PrefetchScalarGridSpec(
        num_scalar_prefetch=5,
        grid=(nw,),
        in_specs=[
            pl.BlockSpec((1, 1, tm), later(0), memory_space=pltpu.SMEM),
            pl.BlockSpec((1, 1, tm), later(1), memory_space=pltpu.SMEM),
            pl.BlockSpec((1, 1, tm), later(2), memory_space=pltpu.SMEM),
            pl.BlockSpec((1, 1, tm), later(0)),
            pl.BlockSpec(memory_space=pl.ANY),
            pl.BlockSpec((None, 1, D_MODEL, 2 * D_EXPERT), by_expert),
            pl.BlockSpec((None, 1, 1, 2 * D_EXPERT), by_expert),
            pl.BlockSpec((None, 1, D_EXPERT, D_MODEL), by_expert),
            pl.BlockSpec((None, 1, 1, D_MODEL), by_expert),
        ],
        out_specs=pl.BlockSpec((tm * TOKEN_TILE_ROWS, LANES), lambda i, lt, ot, ex, lo, hi: (ot[i], 0)),
        scratch_shapes=[
            pltpu.VMEM((GATHER_SLOTS * tm * TOKEN_TILE_ROWS, LANES), F32),
            pltpu.VMEM((tm * TOKEN_TILE_ROWS, LANES), F32),
            pltpu.VMEM((D_MODEL, 2 * D_EXPERT), BF16),
            pltpu.VMEM((D_EXPERT, D_MODEL), BF16),
            pltpu.SemaphoreType.DMA((GATHER_SLOTS,)),
        ],
    )
    return pl.pallas_call(
        functools.partial(_experts_kernel, tm=tm),
        grid_spec=grid_spec,
        out_shape=jax.ShapeDtypeStruct(((n_rows + FETCH_ROWS) * TOKEN_TILE_ROWS, LANES), F32),
        compiler_params=_cparams("arbitrary"),
        name="experts",
    )(*items, lists(tok_sorted), lists(tok_sorted), lists(tok_sorted), lists(w_sorted), h_tiles, wgu, bgu, wd, bd)


def _combine_kernel(src_ref, cst_ref, nch_ref, y_hbm, x1_ref, qpos_ref, g_ref, *rest, tm, nq, final, n_out):
    out_refs, (yc_buf, sem) = rest[:n_out], rest[n_out:]
    i = pl.program_id(0)
    slot = lax.rem(i, 2)

    def chunk_copy(src_row, dst_row, at_slot):
        return pltpu.make_async_copy(_tile_rows(y_hbm, src_row, FETCH_ROWS),
                                     _tile_rows(yc_buf, at_slot * nq + dst_row, FETCH_ROWS), sem.at[at_slot])

    def fetch_tile(tile, to_slot):
        for e in range(N_EXPERTS):
            src0 = src_ref[tile * N_EXPERTS + e]
            dst0 = cst_ref[tile * N_EXPERTS + e]

            def fetch(j, carry, src0=src0, dst0=dst0):
                chunk_copy(src0 + j * FETCH_ROWS, dst0 + j * FETCH_ROWS, to_slot).start()
                return carry

            lax.fori_loop(0, nch_ref[tile * N_EXPERTS + e], fetch, 0)

    @pl.when(i == 0)
    def _():
        yc_buf[...] = jnp.zeros_like(yc_buf)
        fetch_tile(0, 0)

    @pl.when(i + 1 < pl.num_programs(0))
    def _():
        fetch_tile(i + 1, 1 - slot)

    total = nch_ref[i * N_EXPERTS]
    for e in range(1, N_EXPERTS):
        total = total + nch_ref[i * N_EXPERTS + e]

    def wait(j, carry):
        chunk_copy(0, 0, slot).wait()
        return carry

    lax.fori_loop(0, total, wait, 0)

    q = lax.broadcasted_iota(jnp.int32, (tm, nq), 1)
    qpos = qpos_ref[...]
    hit = q == qpos[:, 0:1]
    for k in range(1, TOP_K):
        hit = hit | (q == qpos[:, k:k + 1])
    p = jnp.where(hit, 1.0, 0.0).astype(BF16)
    base = pl.multiple_of(slot * (nq * TOKEN_TILE_ROWS), nq * TOKEN_TILE_ROWS)
    yc = _load_token_tiles(yc_buf, base, nq).astype(BF16)
    x2 = x1_ref[...] + jnp.dot(p, yc, preferred_element_type=F32)
    if final:
        x2 = _rms(x2, g_ref[...])
    steps_per_out = pl.num_programs(0) // n_out
    for j, out_ref in enumerate(out_refs):
        @pl.when((i >= j * steps_per_out) & (i < (j + 1) * steps_per_out))
        def _(out_ref=out_ref):
            out_ref[...] = x2


def _combine(src0, cstart, nchunk, y_sorted, x1, qpos, g_final, tm, final, n_out):
    t = x1.shape[0]
    steps_per_out = t // tm // n_out
    nq = -(-(tm * TOP_K + N_EXPERTS * (FETCH_ROWS - 1)) // LANES) * LANES
    grid_spec = pltpu.PrefetchScalarGridSpec(
        num_scalar_prefetch=3,
        grid=(t // tm,),
        in_specs=[
            pl.BlockSpec(memory_space=pl.ANY),
            pl.BlockSpec((tm, D_MODEL), lambda i, a, b, c: (i, 0)),
            pl.BlockSpec((tm, TOP_K), lambda i, a, b, c: (i, 0)),
            pl.BlockSpec((1, D_MODEL), lambda i, a, b, c: (0, 0)),
        ],
        out_specs=[pl.BlockSpec((tm, D_MODEL), (lambda j: lambda i, a, b, c: (
            jnp.clip(i - j * steps_per_out, 0, steps_per_out - 1), 0))(j)) for j in range(n_out)],
        scratch_shapes=[pltpu.VMEM((2 * nq * TOKEN_TILE_ROWS, LANES), F32), pltpu.SemaphoreType.DMA((2,))],
    )
    return pl.pallas_call(
        functools.partial(_combine_kernel, tm=tm, nq=nq, final=final, n_out=n_out),
        grid_spec=grid_spec,
        out_shape=[jax.ShapeDtypeStruct((t // n_out, D_MODEL), F32)] * n_out,
        compiler_params=_cparams("arbitrary"),
        name="combine",
    )(src0.reshape(-1), cstart.reshape(-1), nchunk.reshape(-1), y_sorted, x1, qpos, g_final)


def _routing_lists(idx, lrank, wgt, tcnt, tm_tok, tm):
    t = idx.shape[1]
    n_rows = t * TOP_K
    nt = n_rows // tm
    cnt = jnp.sum(tcnt, axis=0)
    group_end = jnp.cumsum(cnt)
    group_start = group_end - cnt
    src0 = group_start[None, :] + jnp.cumsum(tcnt, axis=0) - tcnt
    aligned = -(-tcnt // FETCH_ROWS) * FETCH_ROWS
    cstart = jnp.cumsum(aligned, axis=1) - aligned
    nchunk = aligned // FETCH_ROWS
    experts = jnp.arange(N_EXPERTS, dtype=jnp.int32)
    onehot = idx.reshape(TOP_K, -1, tm_tok)[..., None] == experts
    pick = lambda table: jnp.sum(jnp.where(onehot, table[None, :, None, :], 0), axis=-1).reshape(TOP_K, t)
    pos = lrank + pick(src0)
    qpos = (lrank + pick(cstart)).T
    flat = jnp.arange(n_rows, dtype=jnp.int32)
    _, flat_sorted, w_sorted = lax.sort((pos.reshape(-1).astype(jnp.int32), flat, wgt.reshape(-1)), num_keys=1)
    tok_sorted = flat_sorted % t
    bounds = jnp.sort(jnp.concatenate([jnp.arange(nt, dtype=jnp.int32) * tm, group_start.astype(jnp.int32)]))
    nxt = jnp.concatenate([bounds[1:], jnp.array([n_rows], jnp.int32)])
    tile = jnp.minimum(bounds // tm, nt - 1)
    lo = bounds - tile * tm
    hi = jnp.where(nxt > bounds, nxt - tile * tm, lo)
    expert = jnp.minimum(jnp.sum(group_end[None, :] <= bounds[:, None], axis=-1), N_EXPERTS - 1)
    last = lambda a, v: jnp.concatenate([a, jnp.array([v], a.dtype)])
    items = (last(tile, nt - 1), last(tile, nt), jnp.concatenate([expert, expert[-1:]]), last(lo, 0), last(hi, 0))
    items = tuple(a.astype(jnp.int32) for a in items)
    to_i32 = lambda a: a.astype(jnp.int32)
    return items, tok_sorted, w_sorted, to_i32(src0), to_i32(cstart), to_i32(nchunk), to_i32(qpos)


def _block_flags(n_prompt_seq, seg_len, n_tokens):
    nb = n_tokens // BLOCK
    per_seg = seg_len // BLOCK
    b = jnp.arange(nb)
    prompt_blocks = n_prompt_seq * per_seg
    in_prompt = b < prompt_blocks
    first = jnp.where(in_prompt, b % per_seg == 0, b == prompt_blocks)
    last = jnp.where(in_prompt, b % per_seg == per_seg - 1, b == nb - 1)
    return jnp.stack([~first, ~last]).astype(jnp.int32)


def _trunk(x_parts, n_prompt_seq, seg_len, p, tm, tm_moe, tc, nbq):
    n_parts = len(x_parts)
    x = list(x_parts)
    t = sum(a.shape[0] for a in x)
    depth = p['w_in'].shape[0]
    flags = _block_flags(n_prompt_seq, seg_len, t)
    bias = _band_bias(p['rel_bias'])
    zero_init = jnp.zeros((2, 1, N_SEG, SSM_REAL), F32)
    for layer in range(depth):
        ssm, qkv, sgu = _inproj(x, p['norm_mix'][layer][None], p['w_in'][layer].astype(BF16), tm)

        lam_bar, b_blk, a_rows, c_blk = _ssm_params(
            p['ssm_lam_re'][layer], p['ssm_lam_im'][layer], p['ssm_log_step'][layer], p['ssm_b_re'][layer],
            p['ssm_b_im'][layer], p['ssm_c_re'][layer], p['ssm_c_im'][layer])
        _, fin = _ssm_scan(ssm, seg_len, 1, 1, b_blk, a_rows, c_blk, zero_init, tc, False)
        init = jnp.concatenate([zero_init, _chain_init(fin[:, 0], lam_bar ** seg_len)[:, None]], axis=1)
        y, _ = _ssm_scan(ssm, seg_len, 0, 2, b_blk, a_rows, c_blk, init, tc, True)

        sink = jnp.repeat(p['attn_sink'][layer].astype(F32), BLOCK).reshape(N_KV_HEADS, Q_PER_KV * BLOCK, 1)
        sgb = jnp.repeat(p['sgu_b'][layer].astype(F32).T, SGU_GROUP_WIDTH, axis=1)
        attn, sguo = _attn_sgu(flags, qkv, sgu, bias, sink, p['sgu_norm'][layer][None].astype(F32),
                               p['sgu_w'][layer].astype(BF16), sgb, nbq)

        x1, h, idx, rank, wgt, cnt = _outproj_router(
            x, ssm, y[0], y[1], attn, sguo, p['ssm_d'][layer][None].astype(F32), p['out_norm'][layer][None],
            p['w_out'][layer].astype(BF16), p['norm_ffn'][layer][None], p['w_router'][layer].astype(F32),
            p['b_router'][layer][None].astype(F32), tm)

        items, tok_sorted, w_sorted, src0, cstart, nchunk, qpos = _routing_lists(
            idx, rank, wgt, cnt[:, :, 0], tm, tm_moe)
        y_sorted = _experts(items, tok_sorted, w_sorted, h, p['w_gate_up'].astype(F32),
                            p['b_gate_up'][:, :, None].astype(F32), p['w_down'].astype(F32),
                            p['b_down'][:, :, None].astype(F32), layer, tm_moe)
        last = layer == depth - 1
        x = _combine(src0, cstart, nchunk, y_sorted, x1, qpos, p['final_norm'][None].astype(F32), tm,
                     last, n_parts if last else 1)
    return x


def _run(x_prompt, x_sample, p, tm=512, tm_moe=512, tc=64, nbq=2):
    bsz, seg_len, _ = x_prompt.shape
    assert bsz == N_SEG and x_sample.shape[0] == 1 and x_sample.shape[1] == N_SEG * seg_len
    assert seg_len % (nbq * BLOCK) == 0 and seg_len % tc == 0 and seg_len % tm == 0
    parts = [x_prompt.reshape(-1, D_MODEL).astype(F32), x_sample.reshape(-1, D_MODEL).astype(F32)]
    y_prompt, y_sample = _trunk(parts, bsz, seg_len, p, tm, tm_moe, tc, nbq)
    return y_prompt.reshape(x_prompt.shape), y_sample.reshape(x_sample.shape)


def kernel(x_prompt, x_sample, norm_mix, w_in, ssm_lam_re, ssm_lam_im, ssm_log_step, ssm_b_re, ssm_b_im, ssm_c_re, ssm_c_im, ssm_d, attn_sink, rel_bias, sgu_norm, sgu_w, sgu_b, out_norm, w_out, norm_ffn, w_router, b_router, w_gate_up, b_gate_up, w_down, b_down, final_norm):
    p = dict(norm_mix=norm_mix, w_in=w_in, ssm_lam_re=ssm_lam_re, ssm_lam_im=ssm_lam_im, ssm_log_step=ssm_log_step,
             ssm_b_re=ssm_b_re, ssm_b_im=ssm_b_im, ssm_c_re=ssm_c_re, ssm_c_im=ssm_c_im, ssm_d=ssm_d,
             attn_sink=attn_sink, rel_bias=rel_bias, sgu_norm=sgu_norm, sgu_w=sgu_w, sgu_b=sgu_b, out_norm=out_norm,
             w_out=w_out, norm_ffn=norm_ffn, w_router=w_router, b_router=b_router, w_gate_up=w_gate_up,
             b_gate_up=b_gate_up, w_down=w_down, b_down=b_down, final_norm=final_norm)
    return _run(x_prompt, x_sample, p)
```

```python
import functools
import math

import jax
import jax.numpy as jnp
from jax import lax
from jax.experimental import pallas as pl
from jax.experimental.pallas import tpu as pltpu

D_MODEL = 1024
SSM_WIDTH = 256
SSM_GROUP = 16
SSM_GROUPS = 16
SSM_STATE = 64
SSM_COMPLEX = SSM_GROUPS * SSM_STATE
SSM_REAL = 2 * SSM_COMPLEX
ATTN_WIDTH = 512
HEAD_DIM = 64
N_Q_HEADS = 8
N_KV_HEADS = 2
Q_PER_KV = 4
KV_WIDTH = 128
BLOCK = 128
N_BUCKETS = 32
MAX_DISTANCE = 128
SGU_WIDTH = 256
SGU_GROUPS = 4
SGU_GROUP_WIDTH = 64
IN_COLS = 1792
QKV_COLS = ATTN_WIDTH + 2 * KV_WIDTH
N_EXPERTS = 32
TOP_K = 4
D_EXPERT = 1024
SWIGLU_LIMIT = 7.0
SWIGLU_ALPHA = 1.702
EPS = 1e-6
NEG_INF = -1e30

N_SEG = 8
V7X_VMEM_LIMIT = 56 * 1024 * 1024

F32 = jnp.float32
BF16 = jnp.bfloat16


def _rms(xf, g):
    return xf * lax.rsqrt(jnp.mean(xf * xf, axis=-1, keepdims=True) + EPS) * g


LANES = 128
TOKEN_TILE_ROWS = D_MODEL // LANES


def _load_token_tiles(ref, base, n):
    return jnp.concatenate(
        [ref[pl.ds(base + j, n, stride=TOKEN_TILE_ROWS), :] for j in range(TOKEN_TILE_ROWS)], axis=1)


def _store_token_tiles(ref, base, val, n):
    for j in range(TOKEN_TILE_ROWS):
        ref[pl.ds(base + j, n, stride=TOKEN_TILE_ROWS), :] = val[:, j * LANES:(j + 1) * LANES]


def _cparams(*sem):
    return pltpu.CompilerParams(dimension_semantics=sem, vmem_limit_bytes=V7X_VMEM_LIMIT)


def _part_specs(parts, tm):
    per = parts[0].shape[0] // tm
    return [pl.BlockSpec((tm, D_MODEL), (lambda j: lambda i, *_: (jnp.clip(i - j * per, 0, per - 1), 0))(j))
            for j in range(len(parts))]


def _pick_part(x_refs, steps_per_part):
    i = pl.program_id(0)
    x = x_refs[0][...]
    for j in range(1, len(x_refs)):
        x = jnp.where(i >= j * steps_per_part, x_refs[j][...], x)
    return x


def _inproj_kernel(*refs, n_parts, steps_per_part):
    x_refs, (g_ref, w_ref, ssm_ref, qk_ref, vt_ref, sgu_ref) = refs[:n_parts], refs[n_parts:]
    h = _rms(_pick_part(x_refs, steps_per_part), g_ref[...]).astype(BF16)
    p = jnp.dot(h, w_ref[...], preferred_element_type=F32)
    q0, k0, v0 = 2 * SSM_WIDTH, 2 * SSM_WIDTH + ATTN_WIDTH, 2 * SSM_WIDTH + ATTN_WIDTH + KV_WIDTH
    ssm_ref[...] = p[:, :q0].astype(BF16)
    qk_ref[...] = jnp.concatenate([p[:, q0:k0] * (HEAD_DIM ** -0.5), p[:, k0:v0]], axis=1).astype(BF16)
    vt_ref[...] = jnp.transpose(p[:, v0:v0 + KV_WIDTH]).astype(BF16)
    sgu_ref[...] = p[:, v0 + KV_WIDTH:].astype(BF16)


def _inproj(x_parts, g, w, tm):
    t = sum(x.shape[0] for x in x_parts)
    return pl.pallas_call(
        functools.partial(_inproj_kernel, n_parts=len(x_parts), steps_per_part=x_parts[0].shape[0] // tm),
        grid=(t // tm,),
        in_specs=_part_specs(x_parts, tm) + [
            pl.BlockSpec((1, D_MODEL), lambda i: (0, 0)),
            pl.BlockSpec((D_MODEL, IN_COLS), lambda i: (0, 0)),
        ],
        out_specs=[
            pl.BlockSpec((tm, 2 * SSM_WIDTH), lambda i: (i, 0)),
            pl.BlockSpec((tm, ATTN_WIDTH + KV_WIDTH), lambda i: (i, 0)),
            pl.BlockSpec((KV_WIDTH, tm), lambda i: (0, i)),
            pl.BlockSpec((tm, 2 * SGU_WIDTH), lambda i: (i, 0)),
        ],
        out_shape=[
            jax.ShapeDtypeStruct((t, 2 * SSM_WIDTH), BF16),
            jax.ShapeDtypeStruct((t, ATTN_WIDTH + KV_WIDTH), BF16),
            jax.ShapeDtypeStruct((KV_WIDTH, t), BF16),
            jax.ShapeDtypeStruct((t, 2 * SGU_WIDTH), BF16),
        ],
        compiler_params=_cparams("arbitrary"),
        name="inproj",
    )(*x_parts, g, w)


SCAN_COL_BLOCKS = 2


def _ssm_kernel(u_ref, b_ref, a_ref, c_ref, init_ref, *rest, tc, emit_y):
    if emit_y:
        y_ref, fin_ref, bu_scr, st_scr, tm_lo, tm_hi = rest
    else:
        fin_ref, bu_scr, st_scr, tm_lo, tm_hi = rest
    d = pl.program_id(0)
    c = pl.program_id(2)
    halves = ((tm_lo, slice(0, LANES)), (tm_hi, slice(LANES, 2 * LANES)))

    @pl.when(c == 0)
    def _():
        st_scr[...] = init_ref[0, 0]

    for s in range(N_SEG):
        u_s = u_ref[s].astype(F32)
        for scr, cols in halves:
            scr[pl.ds(s, tc, stride=N_SEG), :] = u_s[:, cols]
    u_tm = jnp.concatenate([tm_lo[...], tm_hi[...]], axis=1).astype(BF16)
    bu_scr[...] = jnp.dot(u_tm, b_ref[0], preferred_element_type=F32)
    w = SSM_COMPLEX // SCAN_COL_BLOCKS
    for cb in range(SCAN_COL_BLOCKS):
        re_cols = slice(cb * w, (cb + 1) * w)
        im_cols = slice(SSM_COMPLEX + cb * w, SSM_COMPLEX + (cb + 1) * w)
        a_re = jnp.broadcast_to(a_ref[0, 0:1, re_cols], (N_SEG, w))
        a_im = jnp.broadcast_to(a_ref[0, 1:2, re_cols], (N_SEG, w))

        def body(tt, carry, re_cols=re_cols, im_cols=im_cols, a_re=a_re, a_im=a_im):
            x_re, x_im = carry
            t = jnp.where(d == 0, tt, tc - 1 - tt)
            rows = pl.ds(pl.multiple_of(t * N_SEG, N_SEG), N_SEG)
            n_re = a_re * x_re - a_im * x_im + bu_scr[rows, re_cols]
            n_im = a_re * x_im + a_im * x_re + bu_scr[rows, im_cols]
            bu_scr[rows, re_cols] = n_re
            bu_scr[rows, im_cols] = n_im
            return n_re, n_im

        x_re, x_im = lax.fori_loop(0, tc, body, (st_scr[:, re_cols], st_scr[:, im_cols]), unroll=4)
        st_scr[:, re_cols] = x_re
        st_scr[:, im_cols] = x_im

    if emit_y:
        y = jnp.dot(bu_scr[...].astype(BF16), c_ref[...], preferred_element_type=F32)
        for scr, cols in halves:
            scr[...] = y[:, cols]
        for s in range(N_SEG):
            for scr, cols in halves:
                y_ref[0, s, :, cols] = scr[pl.ds(s, tc, stride=N_SEG), :]

    @pl.when(c == pl.num_programs(2) - 1)
    def _():
        fin_ref[0, 0] = st_scr[...]


def _ssm_scan(ssm_proj, seg_len, first_group, n_groups, b_blk, a_rows, c_blk, init, tc, emit_y):
    u = ssm_proj.reshape(-1, seg_len, 2 * SSM_WIDTH)
    nc = seg_len // tc

    def chunk(d, c):
        return jnp.where(d == 0, c, nc - 1 - c)

    out_specs = [pl.BlockSpec((1, 1, N_SEG, SSM_REAL), lambda d, g, c: (d, g, 0, 0))]
    out_shape = [jax.ShapeDtypeStruct((2, n_groups, N_SEG, SSM_REAL), F32)]
    if emit_y:
        out_specs = [pl.BlockSpec((1, N_SEG, tc, SSM_WIDTH), lambda d, g, c: (d, g, chunk(d, c), 0))] + out_specs
        out_shape = [jax.ShapeDtypeStruct((2, n_groups * N_SEG, seg_len, SSM_WIDTH), F32)] + out_shape
    res = pl.pallas_call(
        functools.partial(_ssm_kernel, tc=tc, emit_y=emit_y),
        grid=(2, n_groups, nc),
        in_specs=[
            pl.BlockSpec((N_SEG, tc, SSM_WIDTH), lambda d, g, c: (first_group + g, chunk(d, c), 0)),
            pl.BlockSpec((1, SSM_WIDTH, SSM_REAL), lambda d, g, c: (d, 0, 0)),
            pl.BlockSpec((1, 2, SSM_COMPLEX), lambda d, g, c: (d, 0, 0)),
            pl.BlockSpec((SSM_REAL, SSM_WIDTH), lambda d, g, c: (0, 0)),
            pl.BlockSpec((1, 1, N_SEG, SSM_REAL), lambda d, g, c: (d, g, 0, 0)),
        ],
        out_specs=out_specs,
        out_shape=out_shape,
        scratch_shapes=[pltpu.VMEM((tc * N_SEG, SSM_REAL), F32), pltpu.VMEM((N_SEG, SSM_REAL), F32),
                        pltpu.VMEM((tc * N_SEG, LANES), F32), pltpu.VMEM((tc * N_SEG, LANES), F32)],
        compiler_params=_cparams("arbitrary", "arbitrary", "arbitrary"),
        name="ssm_scan_y" if emit_y else "ssm_scan_state",
    )(u, b_blk, a_rows, c_blk, init)
    if emit_y:
        return res[0].reshape(2, -1, SSM_WIDTH), res[1]
    return None, res[0]


def _ssm_params(lam_re, lam_im, log_step, b_re, b_im, c_re, c_im):
    lam = lax.complex(lam_re.astype(F32), lam_im.astype(F32))
    step = jnp.exp(log_step.astype(F32))[..., None]
    lam_bar = jnp.exp(lam * step)
    b = lax.complex(b_re.astype(F32), b_im.astype(F32))
    b_bar = ((lam_bar - 1.0) / lam)[..., None] * b
    eye = jnp.eye(SSM_GROUPS, dtype=F32)
    b_blk_re = jnp.einsum('dgph,gk->dghkp', jnp.real(b_bar), eye).reshape(2, SSM_WIDTH, SSM_COMPLEX)
    b_blk_im = jnp.einsum('dgph,gk->dghkp', jnp.imag(b_bar), eye).reshape(2, SSM_WIDTH, SSM_COMPLEX)
    b_blk = jnp.concatenate([b_blk_re, b_blk_im], axis=-1).astype(BF16)
    c_blk_re = jnp.einsum('ghp,gk->gpkh', c_re.astype(F32), eye).reshape(SSM_COMPLEX, SSM_WIDTH)
    c_blk_im = jnp.einsum('ghp,gk->gpkh', c_im.astype(F32), eye).reshape(SSM_COMPLEX, SSM_WIDTH)
    c_blk = jnp.concatenate([c_blk_re, -c_blk_im], axis=0).astype(BF16)
    a_rows = jnp.stack([jnp.real(lam_bar).reshape(2, SSM_COMPLEX),
                        jnp.imag(lam_bar).reshape(2, SSM_COMPLEX)], axis=1)
    return lam_bar.reshape(2, SSM_COMPLEX), b_blk, a_rows, c_blk


def _chain_init(fin, lam_pow):
    f = lax.complex(fin[..., :SSM_COMPLEX], fin[..., SSM_COMPLEX:])
    zero = jnp.zeros_like(f[0, 0])
    fwd = [zero]
    for s in range(1, N_SEG):
        fwd.append(lam_pow[0] * fwd[-1] + f[0, s - 1])
    bwd = [zero]
    for s in range(N_SEG - 2, -1, -1):
        bwd.append(lam_pow[1] * bwd[-1] + f[1, s + 1])
    init = jnp.stack([jnp.stack(fwd), jnp.stack(bwd[::-1])])
    return jnp.concatenate([jnp.real(init), jnp.imag(init)], axis=-1).astype(F32)


def _attn_sgu_kernel(flags_ref, q_ref, kp_ref, kc_ref, kn_ref, vp_ref, vc_ref, vn_ref, bias_ref, sink_ref,
                     sgu_ref, sgn_ref, sgw_ref, sgb_ref, attn_ref, sguo_ref, *, nbq):
    n = pl.program_id(0)
    no_prev = flags_ref[0, n * nbq] == 0
    no_next = flags_ref[1, n * nbq + nbq - 1] == 0

    def k_block(i):
        if i == 0:
            return kp_ref[...]
        if i == nbq + 1:
            return kn_ref[...]
        return kc_ref[(i - 1) * BLOCK:i * BLOCK, :]

    def vt_block(i):
        if i == 0:
            return vp_ref[...]
        if i == nbq + 1:
            return vn_ref[...]
        return vc_ref[:, (i - 1) * BLOCK:i * BLOCK]

    for b in range(nbq):
        rows = slice(b * BLOCK, (b + 1) * BLOCK)
        q = q_ref[rows, :]
        outs = []
        for j in range(N_KV_HEADS):
            qs = jnp.concatenate(
                [q[:, (j * Q_PER_KV + i) * HEAD_DIM:(j * Q_PER_KV + i + 1) * HEAD_DIM] for i in range(Q_PER_KV)],
                axis=0)
            kv = slice(j * HEAD_DIM, (j + 1) * HEAD_DIM)
            s = []
            for i in range(3):
                si = lax.dot_general(k_block(b + i)[:, kv], qs, (((1,), (1,)), ((), ())),
                                     preferred_element_type=F32)
                si = si + bias_ref[j, i * BLOCK:(i + 1) * BLOCK, :]
                if b == 0 and i == 0:
                    si = jnp.where(no_prev, NEG_INF, si)
                if b == nbq - 1 and i == 2:
                    si = jnp.where(no_next, NEG_INF, si)
                s.append(si)
            sink = sink_ref[j]
            m = jnp.max(jnp.maximum(jnp.maximum(s[0], s[1]), s[2]), axis=0, keepdims=True)
            m = jnp.maximum(m, sink)
            p = [jnp.exp(si - m) for si in s]
            denom = jnp.sum(p[0] + p[1] + p[2], axis=0, keepdims=True) + jnp.exp(sink - m)
            inv = 1.0 / denom
            o_t = sum(jnp.dot(vt_block(b + i)[kv, :], (p[i] * inv).astype(BF16), preferred_element_type=F32)
                      for i in range(3))
            outs += [jnp.transpose(o_t[:, i * BLOCK:(i + 1) * BLOCK]) for i in range(Q_PER_KV)]
        attn_ref[rows, :] = jnp.concatenate(outs, axis=1).astype(BF16)

        gu = jax.nn.gelu(sgu_ref[rows, :SGU_WIDTH].astype(F32))
        gv = _rms(jax.nn.gelu(sgu_ref[rows, SGU_WIDTH:].astype(F32)), sgn_ref[...]).astype(BF16)
        mixed = jnp.concatenate(
            [jnp.dot(sgw_ref[g], gv[:, g * SGU_GROUP_WIDTH:(g + 1) * SGU_GROUP_WIDTH], preferred_element_type=F32)
             for g in range(SGU_GROUPS)], axis=1) + sgb_ref[...]
        sguo_ref[rows, :] = (gu * mixed).astype(BF16)


def _attn_sgu(flags, qk, vt, sgu, bias, sink, sgn, sgw, sgb, nbq):
    t = qk.shape[0]
    nb = t // BLOCK
    tile = nbq * BLOCK
    kcol = ATTN_WIDTH // KV_WIDTH
    before = lambda n: jnp.maximum(n * nbq - 1, 0)
    after = lambda n: jnp.minimum((n + 1) * nbq, nb - 1)
    grid_spec = pltpu.PrefetchScalarGridSpec(
        num_scalar_prefetch=1,
        grid=(nb // nbq,),
        in_specs=[
            pl.BlockSpec((tile, ATTN_WIDTH), lambda n, f: (n, 0)),
            pl.BlockSpec((BLOCK, KV_WIDTH), lambda n, f: (before(n), kcol)),
            pl.BlockSpec((tile, KV_WIDTH), lambda n, f: (n, kcol)),
            pl.BlockSpec((BLOCK, KV_WIDTH), lambda n, f: (after(n), kcol)),
            pl.BlockSpec((KV_WIDTH, BLOCK), lambda n, f: (0, before(n))),
            pl.BlockSpec((KV_WIDTH, tile), lambda n, f: (0, n)),
            pl.BlockSpec((KV_WIDTH, BLOCK), lambda n, f: (0, after(n))),
            pl.BlockSpec((N_KV_HEADS, 3 * BLOCK, Q_PER_KV * BLOCK), lambda n, f: (0, 0, 0)),
            pl.BlockSpec((N_KV_HEADS, 1, Q_PER_KV * BLOCK), lambda n, f: (0, 0, 0)),
            pl.BlockSpec((tile, 2 * SGU_WIDTH), lambda n, f: (n, 0)),
            pl.BlockSpec((1, SGU_WIDTH), lambda n, f: (0, 0)),
            pl.BlockSpec((SGU_GROUPS, BLOCK, BLOCK), lambda n, f: (0, 0, 0)),
            pl.BlockSpec((BLOCK, SGU_WIDTH), lambda n, f: (0, 0)),
        ],
        out_specs=[
            pl.BlockSpec((tile, ATTN_WIDTH), lambda n, f: (n, 0)),
            pl.BlockSpec((tile, SGU_WIDTH), lambda n, f: (n, 0)),
        ],
    )
    return pl.pallas_call(
        functools.partial(_attn_sgu_kernel, nbq=nbq),
        grid_spec=grid_spec,
        out_shape=[jax.ShapeDtypeStruct((t, ATTN_WIDTH), BF16), jax.ShapeDtypeStruct((t, SGU_WIDTH), BF16)],
        compiler_params=_cparams("arbitrary"),
        name="attn_sgu",
    )(flags, qk, qk, qk, qk, vt, vt, vt, bias, sink, sgu, sgn, sgw, sgb)


def _t5_bucket(rel):
    half = N_BUCKETS // 2
    max_exact = half // 2
    ret = jnp.where(rel > 0, half, 0)
    n = jnp.abs(rel)
    nf = jnp.maximum(n, 1).astype(F32)
    large = max_exact + (jnp.log(nf / max_exact) / math.log(MAX_DISTANCE / max_exact)
                         * (half - max_exact)).astype(jnp.int32)
    large = jnp.minimum(large, half - 1)
    return ret + jnp.where(n < max_exact, n, large)


def _band_bias(rel_table):
    q_pos = jnp.arange(BLOCK)[:, None]
    k_pos = jnp.arange(3 * BLOCK)[None, :] - BLOCK
    rel = k_pos - q_pos
    hit = _t5_bucket(rel)[..., None] == jnp.arange(N_BUCKETS)
    bias = jnp.sum(jnp.where(hit[..., None], rel_table.astype(F32), 0.0), axis=2)
    bias = jnp.where((jnp.abs(rel) <= BLOCK)[..., None], bias, NEG_INF)
    bias = bias.reshape(BLOCK, 3 * BLOCK, N_KV_HEADS, Q_PER_KV)
    return bias.transpose(2, 1, 3, 0).reshape(N_KV_HEADS, 3 * BLOCK, Q_PER_KV * BLOCK)


def _outproj_router_kernel(*refs, tm, n_parts, steps_per_part):
    x_refs = refs[:n_parts]
    (ssm_ref, yf_ref, yb_ref, attn_ref, sguo_ref, d_ref, gout_ref, wout_ref, gffn_ref, wrh_ref, wrl_ref, br_ref,
     x1_ref, h_ref, idx_ref, rank_ref, wgt_ref, cnt_ref) = refs[n_parts:]
    u = ssm_ref[:, :SSM_WIDTH].astype(F32)
    gate = ssm_ref[:, SSM_WIDTH:].astype(F32)
    ya = yf_ref[...] + yb_ref[...] + d_ref[...] * u
    ya = jax.nn.gelu(ya) * jax.nn.sigmoid(gate)
    g = gout_ref[...]
    mixed = jnp.concatenate([
        _rms(ya, g[:, :SSM_WIDTH]),
        _rms(attn_ref[...].astype(F32), g[:, SSM_WIDTH:SSM_WIDTH + ATTN_WIDTH]),
        _rms(sguo_ref[...].astype(F32), g[:, SSM_WIDTH + ATTN_WIDTH:]),
    ], axis=1).astype(BF16)
    x1 = _pick_part(x_refs, steps_per_part) + jnp.dot(mixed, wout_ref[...], preferred_element_type=F32)
    x1_ref[...] = x1
    h = _rms(x1, gffn_ref[...])
    _store_token_tiles(h_ref, 0, h, tm)

    h_hi = h.astype(BF16)
    h_lo = (h - h_hi.astype(F32)).astype(BF16)
    nt_dot = lambda a, b: lax.dot_general(a, b, (((1,), (1,)), ((), ())), preferred_element_type=F32)
    lg = nt_dot(wrh_ref[...], h_hi) + nt_dot(wrl_ref[...], h_hi) + nt_dot(wrh_ref[...], h_lo) + br_ref[...]
    row = lax.broadcasted_iota(jnp.int32, (N_EXPERTS, tm), 0)
    vals, idxs, sels = [], [], []
    for _ in range(TOP_K):
        m = jnp.max(lg, axis=0, keepdims=True)
        idx = jnp.min(jnp.where(lg == m, row, N_EXPERTS), axis=0, keepdims=True)
        sel = row == idx
        lg = jnp.where(sel, -jnp.inf, lg)
        vals.append(m)
        idxs.append(idx)
        sels.append(sel)
    e = [jnp.exp(v - vals[0]) for v in vals]
    tot = e[0] + e[1] + e[2] + e[3]
    wgt_ref[...] = jnp.concatenate([ek / tot for ek in e], axis=0)
    idx_ref[...] = jnp.concatenate(idxs, axis=0)

    picked = (sels[0] | sels[1] | sels[2] | sels[3])
    pm = jnp.where(picked, 1.0, 0.0)
    r_io = lax.broadcasted_iota(jnp.int32, (tm, tm), 0)
    c_io = lax.broadcasted_iota(jnp.int32, (tm, tm), 1)
    earlier = jnp.where(r_io < c_io, 1.0, 0.0).astype(BF16)
    before = jnp.dot(pm.astype(BF16), earlier, preferred_element_type=F32)
    rank_ref[...] = jnp.concatenate(
        [jnp.sum(jnp.where(s, before, 0.0), axis=0, keepdims=True) for s in sels], axis=0).astype(jnp.int32)
    cnt_ref[0] = jnp.sum(pm, axis=1, keepdims=True).astype(jnp.int32)


def _outproj_router(x_parts, ssm, yf, yb, attn, sguo, d_skip, g_out, w_out, g_ffn, w_r, b_r, tm):
    t = ssm.shape[0]
    row = lambda w: pl.BlockSpec((tm, w), lambda i: (i, 0))
    full = lambda a, b: pl.BlockSpec((a, b), lambda i: (0, 0))
    per_choice = pl.BlockSpec((TOP_K, tm), lambda i: (0, i))
    w_rt = w_r.astype(F32).T
    w_hi = w_rt.astype(BF16)
    w_lo = (w_rt - w_hi.astype(F32)).astype(BF16)
    return pl.pallas_call(
        functools.partial(_outproj_router_kernel, tm=tm, n_parts=len(x_parts),
                          steps_per_part=x_parts[0].shape[0] // tm),
        grid=(t // tm,),
        in_specs=_part_specs(x_parts, tm) + [
            row(2 * SSM_WIDTH), row(SSM_WIDTH), row(SSM_WIDTH), row(ATTN_WIDTH), row(SGU_WIDTH),
            full(1, SSM_WIDTH), full(1, D_MODEL), full(D_MODEL, D_MODEL), full(1, D_MODEL),
            full(N_EXPERTS, D_MODEL), full(N_EXPERTS, D_MODEL), full(N_EXPERTS, 1),
        ],
        out_specs=[row(D_MODEL), pl.BlockSpec((tm * TOKEN_TILE_ROWS, LANES), lambda i: (i, 0)),
                   per_choice, per_choice, per_choice, pl.BlockSpec((1, N_EXPERTS, 1), lambda i: (i, 0, 0))],
        out_shape=[
            jax.ShapeDtypeStruct((t, D_MODEL), F32),
            jax.ShapeDtypeStruct((t * TOKEN_TILE_ROWS, LANES), F32),
            jax.ShapeDtypeStruct((TOP_K, t), jnp.int32),
            jax.ShapeDtypeStruct((TOP_K, t), jnp.int32),
            jax.ShapeDtypeStruct((TOP_K, t), F32),
            jax.ShapeDtypeStruct((t // tm, N_EXPERTS, 1), jnp.int32),
        ],
        compiler_params=_cparams("arbitrary"),
        name="outproj_router",
    )(*x_parts, ssm, yf, yb, attn, sguo, d_skip, g_out, w_out, g_ffn, w_hi, w_lo,
      b_r.astype(F32).reshape(N_EXPERTS, 1))


ISSUE_UNROLL = 8
FETCH_ROWS = 16


def _for_rows(n, fn):
    def body(c, carry):
        for u in range(ISSUE_UNROLL):
            fn(c * ISSUE_UNROLL + u, u % 2)
        return carry

    lax.fori_loop(0, n // ISSUE_UNROLL, body, 0)


def _tile_rows(ref, first_tile, n_tiles=1):
    start = first_tile * TOKEN_TILE_ROWS
    if not isinstance(start, int):
        start = pl.multiple_of(start, TOKEN_TILE_ROWS)
    return ref.at[pl.ds(start, n_tiles * TOKEN_TILE_ROWS)]


EXPERT_COL_CHUNK = 256


GATHER_SLOTS = 3


def _experts_kernel(ltile_ref, otile_ref, exp_ref, lo_ref, hi_ref, tok_ref, tok1_ref, tok2_ref, wrow_ref, h_hbm, wgu_ref,
                    bgu_ref, wd_ref, bd_ref, out_ref, x_buf, y_buf, wgu_bf, wd_bf, gsem, *, tm):
    w = pl.program_id(0)
    nw = pl.num_programs(0)
    slot = lax.rem(w, GATHER_SLOTS)
    ahead = lax.rem(w + GATHER_SLOTS - 1, GATHER_SLOTS)
    prev = jnp.maximum(w - 1, 0)

    def gather_row(list_ref, to_slot, r, queue):
        pltpu.make_async_copy(_tile_rows(h_hbm, list_ref[0, 0, r]), _tile_rows(x_buf, to_slot * tm + r),
                              gsem.at[to_slot]).start(priority=queue)

    def wait_gather(of_slot):
        pltpu.make_async_copy(_tile_rows(h_hbm, 0, tm), _tile_rows(x_buf, of_slot * tm, tm), gsem.at[of_slot]).wait()

    @pl.when(w == 0)
    def _():
        _for_rows(tm, lambda r, queue: gather_row(tok_ref, 0, r, queue))
        _for_rows(tm, lambda r, queue: gather_row(tok1_ref, 1, r, queue))

    @pl.when((w == 0) | (exp_ref[w] != exp_ref[prev]))
    def _():
        wgu_bf[...] = wgu_ref[0].astype(BF16)
        wd_bf[...] = wd_ref[0].astype(BF16)

    wait_gather(slot)

    n_pieces = (D_EXPERT + D_MODEL) // EXPERT_COL_CHUNK
    per_piece = tm // n_pieces
    piece = 0

    def issue_next_gather():
        nonlocal piece
        for r in range(piece * per_piece, (piece + 1) * per_piece):
            gather_row(tok2_ref, ahead, r, r % 2)
        piece += 1

    w_rows = jnp.transpose(jnp.broadcast_to(wrow_ref[0], (LANES, tm)))
    w_rows = jnp.concatenate([w_rows] * (EXPERT_COL_CHUNK // LANES), axis=1)

    base = pl.multiple_of(slot * (tm * TOKEN_TILE_ROWS), tm * TOKEN_TILE_ROWS)
    x = _load_token_tiles(x_buf, base, tm).astype(BF16)
    acts = []
    for c in range(D_EXPERT // EXPERT_COL_CHUNK):
        issue_next_gather()
        g_cols = slice(c * EXPERT_COL_CHUNK, (c + 1) * EXPERT_COL_CHUNK)
        u_cols = slice(D_EXPERT + c * EXPERT_COL_CHUNK, D_EXPERT + (c + 1) * EXPERT_COL_CHUNK)
        gate = jnp.dot(x, wgu_bf[:, g_cols], preferred_element_type=F32) + bgu_ref[0, :, g_cols]
        up = jnp.dot(x, wgu_bf[:, u_cols], preferred_element_type=F32) + bgu_ref[0, :, u_cols]
        gate = jnp.minimum(gate, SWIGLU_LIMIT)
        up = jnp.clip(up, -SWIGLU_LIMIT, SWIGLU_LIMIT)
        acts.append(((up + 1.0) * gate * jax.nn.sigmoid(SWIGLU_ALPHA * gate)).astype(BF16))
    act = jnp.concatenate(acts, axis=1)
    tiles_per_chunk = EXPERT_COL_CHUNK // LANES
    for c in range(D_MODEL // EXPERT_COL_CHUNK):
        issue_next_gather()
        cols = slice(c * EXPERT_COL_CHUNK, (c + 1) * EXPERT_COL_CHUNK)
        y = (jnp.dot(act, wd_bf[:, cols], preferred_element_type=F32) + bd_ref[0, :, cols]) * w_rows
        for j in range(tiles_per_chunk):
            y_buf[pl.ds(c * tiles_per_chunk + j, tm, stride=TOKEN_TILE_ROWS), :] = y[:, j * LANES:(j + 1) * LANES]

    first = (w == 0) | (otile_ref[w] != otile_ref[prev])

    @pl.when(first)
    def _():
        out_ref[...] = y_buf[...]

    @pl.when(jnp.logical_not(first))
    def _():
        row = lax.shift_right_logical(lax.broadcasted_iota(jnp.int32, y_buf.shape, 0),
                                      TOKEN_TILE_ROWS.bit_length() - 1)
        mine = (row >= lo_ref[w]) & (row < hi_ref[w])
        out_ref[...] = jnp.where(mine, y_buf[...], out_ref[...])

    @pl.when(w == nw - 1)
    def _():
        for k in range(1, GATHER_SLOTS):
            wait_gather(lax.rem(w + k, GATHER_SLOTS))


def _experts(items, tok_sorted, w_sorted, h_tiles, wgu, bgu, wd, bd, layer, tm):
    nw = items[0].shape[0]
    n_rows = tok_sorted.shape[0]
    nt = n_rows // tm
    lists = lambda a: a.reshape(nt, 1, tm)
    later = lambda k: (lambda i, lt, ot, ex, lo, hi: (lt[jnp.minimum(i + k, nw - 1)], 0, 0))
    by_expert = lambda i, lt, ot, ex, lo, hi: (layer, ex[i], 0, 0)
    grid_spec = pltpu.PrefetchScalarGridSpec(
        num_scalar_prefetch=5,
        grid=(nw,),
        in_specs=[
            pl.BlockSpec((1, 1, tm), later(0), memory_space=pltpu.SMEM),
            pl.BlockSpec((1, 1, tm), later(1), memory_space=pltpu.SMEM),
            pl.BlockSpec((1, 1, tm), later(2), memory_space=pltpu.SMEM),
            pl.BlockSpec((1, 1, tm), later(0)),
            pl.BlockSpec(memory_space=pl.ANY),
            pl.BlockSpec((None, 1, D_MODEL, 2 * D_EXPERT), by_expert),
            pl.BlockSpec((None, 1, 1, 2 * D_EXPERT), by_expert),
            pl.BlockSpec((None, 1, D_EXPERT, D_MODEL), by_expert),
            pl.BlockSpec((None, 1, 1, D_MODEL), by_expert),
        ],
        out_specs=pl.BlockSpec((tm * TOKEN_TILE_ROWS, LANES), lambda i, lt, ot, ex, lo, hi: (ot[i], 0)),
        scratch_shapes=[
            pltpu.VMEM((GATHER_SLOTS * tm * TOKEN_TILE_ROWS, LANES), F32),
            pltpu.VMEM((tm * TOKEN_TILE_ROWS, LANES), F32),
            pltpu.VMEM((D_MODEL, 2 * D_EXPERT), BF16),
            pltpu.VMEM((D_EXPERT, D_MODEL), BF16),
            pltpu.SemaphoreType.DMA((GATHER_SLOTS,)),
        ],
    )
    return pl.pallas_call(
        functools.partial(_experts_kernel, tm=tm),
        grid_spec=grid_spec,
        out_shape=jax.ShapeDtypeStruct(((n_rows + FETCH_ROWS) * TOKEN_TILE_ROWS, LANES), F32),
        compiler_params=_cparams("arbitrary"),
        name="experts",
    )(*items, lists(tok_sorted), lists(tok_sorted), lists(tok_sorted), lists(w_sorted), h_tiles, wgu, bgu, wd, bd)


def _combine_kernel(src_ref, cst_ref, nch_ref, y_hbm, x1_ref, qpos_ref, g_ref, *rest, tm, nq, final, n_out):
    out_refs, (yc_buf, sem) = rest[:n_out], rest[n_out:]
    i = pl.program_id(0)
    slot = lax.rem(i, 2)

    def chunk_copy(src_row, dst_row, at_slot):
        return pltpu.make_async_copy(_tile_rows(y_hbm, src_row, FETCH_ROWS),
                                     _tile_rows(yc_buf, at_slot * nq + dst_row, FETCH_ROWS), sem.at[at_slot])

    def fetch_tile(tile, to_slot):
        for e in range(N_EXPERTS):
            src0 = src_ref[tile * N_EXPERTS + e]
            dst0 = cst_ref[tile * N_EXPERTS + e]

            def fetch(j, carry, src0=src0, dst0=dst0):
                chunk_copy(src0 + j * FETCH_ROWS, dst0 + j * FETCH_ROWS, to_slot).start()
                return carry

            lax.fori_loop(0, nch_ref[tile * N_EXPERTS + e], fetch, 0)

    @pl.when(i == 0)
    def _():
        yc_buf[...] = jnp.zeros_like(yc_buf)
        fetch_tile(0, 0)

    @pl.when(i + 1 < pl.num_programs(0))
    def _():
        fetch_tile(i + 1, 1 - slot)

    total = nch_ref[i * N_EXPERTS]
    for e in range(1, N_EXPERTS):
        total = total + nch_ref[i * N_EXPERTS + e]

    def wait(j, carry):
        chunk_copy(0, 0, slot).wait()
        return carry

    lax.fori_loop(0, total, wait, 0)

    q = lax.broadcasted_iota(jnp.int32, (tm, nq), 1)
    qpos = qpos_ref[...]
    hit = q == qpos[:, 0:1]
    for k in range(1, TOP_K):
        hit = hit | (q == qpos[:, k:k + 1])
    p = jnp.where(hit, 1.0, 0.0).astype(BF16)
    base = pl.multiple_of(slot * (nq * TOKEN_TILE_ROWS), nq * TOKEN_TILE_ROWS)
    yc = _load_token_tiles(yc_buf, base, nq).astype(BF16)
    x2 = x1_ref[...] + jnp.dot(p, yc, preferred_element_type=F32)
    if final:
        x2 = _rms(x2, g_ref[...])
    steps_per_out = pl.num_programs(0) // n_out
    for j, out_ref in enumerate(out_refs):
        @pl.when((i >= j * steps_per_out) & (i < (j + 1) * steps_per_out))
        def _(out_ref=out_ref):
            out_ref[...] = x2


def _combine(src0, cstart, nchunk, y_sorted, x1, qpos, g_final, tm, final, n_out):
    t = x1.shape[0]
    steps_per_out = t // tm // n_out
    nq = -(-(tm * TOP_K + N_EXPERTS * (FETCH_ROWS - 1)) // LANES) * LANES
    grid_spec = pltpu.PrefetchScalarGridSpec(
        num_scalar_prefetch=3,
        grid=(t // tm,),
        in_specs=[
            pl.BlockSpec(memory_space=pl.ANY),
            pl.BlockSpec((tm, D_MODEL), lambda i, a, b, c: (i, 0)),
            pl.BlockSpec((tm, TOP_K), lambda i, a, b, c: (i, 0)),
            pl.BlockSpec((1, D_MODEL), lambda i, a, b, c: (0, 0)),
        ],
        out_specs=[pl.BlockSpec((tm, D_MODEL), (lambda j: lambda i, a, b, c: (
            jnp.clip(i - j * steps_per_out, 0, steps_per_out - 1), 0))(j)) for j in range(n_out)],
        scratch_shapes=[pltpu.VMEM((2 * nq * TOKEN_TILE_ROWS, LANES), F32), pltpu.SemaphoreType.DMA((2,))],
    )
    return pl.pallas_call(
        functools.partial(_combine_kernel, tm=tm, nq=nq, final=final, n_out=n_out),
        grid_spec=grid_spec,
        out_shape=[jax.ShapeDtypeStruct((t // n_out, D_MODEL), F32)] * n_out,
        compiler_params=_cparams("arbitrary"),
        name="combine",
    )(src0.reshape(-1), cstart.reshape(-1), nchunk.reshape(-1), y_sorted, x1, qpos, g_final)


def _routing_lists(idx, lrank, wgt, tcnt, tm_tok, tm):
    t = idx.shape[1]
    n_rows = t * TOP_K
    nt = n_rows // tm
    cnt = jnp.sum(tcnt, axis=0)
    group_end = jnp.cumsum(cnt)
    group_start = group_end - cnt
    src0 = group_start[None, :] + jnp.cumsum(tcnt, axis=0) - tcnt
    aligned = -(-tcnt // FETCH_ROWS) * FETCH_ROWS
    cstart = jnp.cumsum(aligned, axis=1) - aligned
    nchunk = aligned // FETCH_ROWS
    experts = jnp.arange(N_EXPERTS, dtype=jnp.int32)
    onehot = idx.reshape(TOP_K, -1, tm_tok)[..., None] == experts
    pick = lambda table: jnp.sum(jnp.where(onehot, table[None, :, None, :], 0), axis=-1).reshape(TOP_K, t)
    pos = lrank + pick(src0)
    qpos = (lrank + pick(cstart)).T
    flat = jnp.arange(n_rows, dtype=jnp.int32)
    _, flat_sorted, w_sorted = lax.sort((pos.reshape(-1).astype(jnp.int32), flat, wgt.reshape(-1)), num_keys=1)
    tok_sorted = flat_sorted % t
    bounds = jnp.sort(jnp.concatenate([jnp.arange(nt, dtype=jnp.int32) * tm, group_start.astype(jnp.int32)]))
    nxt = jnp.concatenate([bounds[1:], jnp.array([n_rows], jnp.int32)])
    tile = jnp.minimum(bounds // tm, nt - 1)
    lo = bounds - tile * tm
    hi = jnp.where(nxt > bounds, nxt - tile * tm, lo)
    expert = jnp.minimum(jnp.sum(group_end[None, :] <= bounds[:, None], axis=-1), N_EXPERTS - 1)
    last = lambda a, v: jnp.concatenate([a, jnp.array([v], a.dtype)])
    items = (last(tile, nt - 1), last(tile, nt), jnp.concatenate([expert, expert[-1:]]), last(lo, 0), last(hi, 0))
    items = tuple(a.astype(jnp.int32) for a in items)
    to_i32 = lambda a: a.astype(jnp.int32)
    return items, tok_sorted, w_sorted, to_i32(src0), to_i32(cstart), to_i32(nchunk), to_i32(qpos)


def _block_flags(n_prompt_seq, seg_len, n_tokens):
    nb = n_tokens // BLOCK
    per_seg = seg_len // BLOCK
    b = jnp.arange(nb)
    prompt_blocks = n_prompt_seq * per_seg
    in_prompt = b < prompt_blocks
    first = jnp.where(in_prompt, b % per_seg == 0, b == prompt_blocks)
    last = jnp.where(in_prompt, b % per_seg == per_seg - 1, b == nb - 1)
    return jnp.stack([~first, ~last]).astype(jnp.int32)


def _trunk(x_parts, n_prompt_seq, seg_len, p, tm, tm_moe, tc, nbq):
    n_parts = len(x_parts)
    x = list(x_parts)
    t = sum(a.shape[0] for a in x)
    depth = p['w_in'].shape[0]
    flags = _block_flags(n_prompt_seq, seg_len, t)
    bias = _band_bias(p['rel_bias'])
    zero_init = jnp.zeros((2, 1, N_SEG, SSM_REAL), F32)
    for layer in range(depth):
        ssm, qk, vt, sgu = _inproj(x, p['norm_mix'][layer][None], p['w_in'][layer].astype(BF16), tm)

        lam_bar, b_blk, a_rows, c_blk = _ssm_params(
            p['ssm_lam_re'][layer], p['ssm_lam_im'][layer], p['ssm_log_step'][layer], p['ssm_b_re'][layer],
            p['ssm_b_im'][layer], p['ssm_c_re'][layer], p['ssm_c_im'][layer])
        _, fin = _ssm_scan(ssm, seg_len, 1, 1, b_blk, a_rows, c_blk, zero_init, tc, False)
        init = jnp.concatenate([zero_init, _chain_init(fin[:, 0], lam_bar ** seg_len)[:, None]], axis=1)
        y, _ = _ssm_scan(ssm, seg_len, 0, 2, b_blk, a_rows, c_blk, init, tc, True)

        sink = jnp.repeat(p['attn_sink'][layer].astype(F32), BLOCK).reshape(N_KV_HEADS, 1, Q_PER_KV * BLOCK)
        sgb = jnp.repeat(p['sgu_b'][layer].astype(F32).T, SGU_GROUP_WIDTH, axis=1)
        attn, sguo = _attn_sgu(flags, qk, vt, sgu, bias, sink, p['sgu_norm'][layer][None].astype(F32),
                               p['sgu_w'][layer].astype(BF16), sgb, nbq)

        x1, h, idx, rank, wgt, cnt = _outproj_router(
            x, ssm, y[0], y[1], attn, sguo, p['ssm_d'][layer][None].astype(F32), p['out_norm'][layer][None],
            p['w_out'][layer].astype(BF16), p['norm_ffn'][layer][None], p['w_router'][layer].astype(F32),
            p['b_router'][layer][None].astype(F32), tm)

        items, tok_sorted, w_sorted, src0, cstart, nchunk, qpos = _routing_lists(
            idx, rank, wgt, cnt[:, :, 0], tm, tm_moe)
        y_sorted = _experts(items, tok_sorted, w_sorted, h, p['w_gate_up'].astype(F32),
                            p['b_gate_up'][:, :, None].astype(F32), p['w_down'].astype(F32),
                            p['b_down'][:, :, None].astype(F32), layer, tm_moe)
        last = layer == depth - 1
        x = _combine(src0, cstart, nchunk, y_sorted, x1, qpos, p['final_norm'][None].astype(F32), tm,
                     last, n_parts if last else 1)
    return x


def _run(x_prompt, x_sample, p, tm=512, tm_moe=512, tc=128, nbq=2):
    bsz, seg_len, _ = x_prompt.shape
    assert bsz == N_SEG and x_sample.shape[0] == 1 and x_sample.shape[1] == N_SEG * seg_len
    assert seg_len % (nbq * BLOCK) == 0 and seg_len % tc == 0 and seg_len % tm == 0
    parts = [x_prompt.reshape(-1, D_MODEL).astype(F32), x_sample.reshape(-1, D_MODEL).astype(F32)]
    y_prompt, y_sample = _trunk(parts, bsz, seg_len, p, tm, tm_moe, tc, nbq)
    return y_prompt.reshape(x_prompt.shape), y_sample.reshape(x_sample.shape)


def kernel(x_prompt, x_sample, norm_mix, w_in, ssm_lam_re, ssm_lam_im, ssm_log_step, ssm_b_re, ssm_b_im, ssm_c_re, ssm_c_im, ssm_d, attn_sink, rel_bias, sgu_norm, sgu_w, sgu_b, out_norm, w_out, norm_ffn, w_router, b_router, w_gate_up, b_gate_up, w_down, b_down, final_norm):
    p = dict(norm_mix=norm_mix, w_in=w_in, ssm_lam_re=ssm_lam_re, ssm_lam_im=ssm_lam_im, ssm_log_step=ssm_log_step,
             ssm_b_re=ssm_b_re, ssm_b_im=ssm_b_im, ssm_c_re=ssm_c_re, ssm_c_im=ssm_c_im, ssm_d=ssm_d,
             attn_sink=attn_sink, rel_bias=rel_bias, sgu_norm=sgu_norm, sgu_w=sgu_w, sgu_b=sgu_b, out_norm=out_norm,
             w_out=w_out, norm_ffn=norm_ffn, w_router=w_router, b_router=b_router, w_gate_up=w_gate_up,
             b_gate_up=b_gate_up, w_down=w_down, b_down=b_down, final_norm=final_norm)
    return _run(x_prompt, x_sample, p)
```

```python
import functools
import math

import jax
import jax.numpy as jnp
from jax import lax
from jax.experimental import pallas as pl
from jax.experimental.pallas import tpu as pltpu

D_MODEL = 1024
SSM_WIDTH = 256
SSM_GROUP = 16
SSM_GROUPS = 16
SSM_STATE = 64
SSM_COMPLEX = SSM_GROUPS * SSM_STATE
SSM_REAL = 2 * SSM_COMPLEX
ATTN_WIDTH = 512
HEAD_DIM = 64
N_Q_HEADS = 8
N_KV_HEADS = 2
Q_PER_KV = 4
KV_WIDTH = 128
BLOCK = 128
N_BUCKETS = 32
MAX_DISTANCE = 128
SGU_WIDTH = 256
SGU_GROUPS = 4
SGU_GROUP_WIDTH = 64
IN_COLS = 1792
QKV_COLS = ATTN_WIDTH + 2 * KV_WIDTH
N_EXPERTS = 32
TOP_K = 4
D_EXPERT = 1024
SWIGLU_LIMIT = 7.0
SWIGLU_ALPHA = 1.702
EPS = 1e-6
NEG_INF = -1e30

N_SEG = 8
V7X_VMEM_LIMIT = 56 * 1024 * 1024

F32 = jnp.float32
BF16 = jnp.bfloat16


def _rms(xf, g):
    return xf * lax.rsqrt(jnp.mean(xf * xf, axis=-1, keepdims=True) + EPS) * g


LANES = 128
TOKEN_TILE_ROWS = D_MODEL // LANES


def _load_token_tiles(ref, base, n):
    return jnp.concatenate(
        [ref[pl.ds(base + j, n, stride=TOKEN_TILE_ROWS), :] for j in range(TOKEN_TILE_ROWS)], axis=1)


def _store_token_tiles(ref, base, val, n):
    for j in range(TOKEN_TILE_ROWS):
        ref[pl.ds(base + j, n, stride=TOKEN_TILE_ROWS), :] = val[:, j * LANES:(j + 1) * LANES]


def _cparams(*sem):
    return pltpu.CompilerParams(dimension_semantics=sem, vmem_limit_bytes=V7X_VMEM_LIMIT)


def _part_specs(parts, tm):
    per = parts[0].shape[0] // tm
    return [pl.BlockSpec((tm, D_MODEL), (lambda j: lambda i, *_: (jnp.clip(i - j * per, 0, per - 1), 0))(j))
            for j in range(len(parts))]


def _pick_part(x_refs, steps_per_part):
    i = pl.program_id(0)
    x = x_refs[0][...]
    for j in range(1, len(x_refs)):
        x = jnp.where(i >= j * steps_per_part, x_refs[j][...], x)
    return x


def _inproj_kernel(*refs, n_parts, steps_per_part):
    x_refs, (g_ref, w_ref, ssm_ref, qk_ref, vt_ref, sgu_ref) = refs[:n_parts], refs[n_parts:]
    h = _rms(_pick_part(x_refs, steps_per_part), g_ref[...]).astype(BF16)
    p = jnp.dot(h, w_ref[...], preferred_element_type=F32)
    q0, k0, v0 = 2 * SSM_WIDTH, 2 * SSM_WIDTH + ATTN_WIDTH, 2 * SSM_WIDTH + ATTN_WIDTH + KV_WIDTH
    ssm_ref[...] = p[:, :q0].astype(BF16)
    qk_ref[...] = jnp.concatenate([p[:, q0:k0] * (HEAD_DIM ** -0.5), p[:, k0:v0]], axis=1).astype(BF16)
    vt_ref[...] = jnp.transpose(p[:, v0:v0 + KV_WIDTH]).astype(BF16)
    sgu_ref[...] = p[:, v0 + KV_WIDTH:].astype(BF16)


def _inproj(x_parts, g, w, tm):
    t = sum(x.shape[0] for x in x_parts)
    return pl.pallas_call(
        functools.partial(_inproj_kernel, n_parts=len(x_parts), steps_per_part=x_parts[0].shape[0] // tm),
        grid=(t // tm,),
        in_specs=_part_specs(x_parts, tm) + [
            pl.BlockSpec((1, D_MODEL), lambda i: (0, 0)),
            pl.BlockSpec((D_MODEL, IN_COLS), lambda i: (0, 0)),
        ],
        out_specs=[
            pl.BlockSpec((tm, 2 * SSM_WIDTH), lambda i: (i, 0)),
            pl.BlockSpec((tm, ATTN_WIDTH + KV_WIDTH), lambda i: (i, 0)),
            pl.BlockSpec((KV_WIDTH, tm), lambda i: (0, i)),
            pl.BlockSpec((tm, 2 * SGU_WIDTH), lambda i: (i, 0)),
        ],
        out_shape=[
            jax.ShapeDtypeStruct((t, 2 * SSM_WIDTH), BF16),
            jax.ShapeDtypeStruct((t, ATTN_WIDTH + KV_WIDTH), BF16),
            jax.ShapeDtypeStruct((KV_WIDTH, t), BF16),
            jax.ShapeDtypeStruct((t, 2 * SGU_WIDTH), BF16),
        ],
        compiler_params=_cparams("arbitrary"),
        name="inproj",
    )(*x_parts, g, w)


SCAN_COL_BLOCKS = 2


def _ssm_kernel(u_ref, b_ref, a_ref, c_ref, init_ref, *rest, tc, emit_y):
    if emit_y:
        y_ref, fin_ref, bu_scr, st_scr, tm_lo, tm_hi = rest
    else:
        fin_ref, bu_scr, st_scr, tm_lo, tm_hi = rest
    d = pl.program_id(0)
    c = pl.program_id(2)
    halves = ((tm_lo, slice(0, LANES)), (tm_hi, slice(LANES, 2 * LANES)))

    @pl.when(c == 0)
    def _():
        st_scr[...] = init_ref[0, 0]

    for s in range(N_SEG):
        u_s = u_ref[s].astype(F32)
        for scr, cols in halves:
            scr[pl.ds(s, tc, stride=N_SEG), :] = u_s[:, cols]
    u_tm = jnp.concatenate([tm_lo[...], tm_hi[...]], axis=1).astype(BF16)
    bu_scr[...] = jnp.dot(u_tm, b_ref[0], preferred_element_type=F32)
    w = SSM_COMPLEX // SCAN_COL_BLOCKS
    for cb in range(SCAN_COL_BLOCKS):
        re_cols = slice(cb * w, (cb + 1) * w)
        im_cols = slice(SSM_COMPLEX + cb * w, SSM_COMPLEX + (cb + 1) * w)
        a_re = jnp.broadcast_to(a_ref[0, 0:1, re_cols], (N_SEG, w))
        a_im = jnp.broadcast_to(a_ref[0, 1:2, re_cols], (N_SEG, w))

        def body(tt, carry, re_cols=re_cols, im_cols=im_cols, a_re=a_re, a_im=a_im):
            x_re, x_im = carry
            t = jnp.where(d == 0, tt, tc - 1 - tt)
            rows = pl.ds(pl.multiple_of(t * N_SEG, N_SEG), N_SEG)
            n_re = a_re * x_re - a_im * x_im + bu_scr[rows, re_cols]
            n_im = a_re * x_im + a_im * x_re + bu_scr[rows, im_cols]
            bu_scr[rows, re_cols] = n_re
            bu_scr[rows, im_cols] = n_im
            return n_re, n_im

        x_re, x_im = lax.fori_loop(0, tc, body, (st_scr[:, re_cols], st_scr[:, im_cols]), unroll=4)
        st_scr[:, re_cols] = x_re
        st_scr[:, im_cols] = x_im

    if emit_y:
        y = jnp.dot(bu_scr[...].astype(BF16), c_ref[...], preferred_element_type=F32)
        for scr, cols in halves:
            scr[...] = y[:, cols]
        for s in range(N_SEG):
            for scr, cols in halves:
                y_ref[0, s, :, cols] = scr[pl.ds(s, tc, stride=N_SEG), :]

    @pl.when(c == pl.num_programs(2) - 1)
    def _():
        fin_ref[0, 0] = st_scr[...]


def _ssm_pipe_kernel(u_ref, b_ref, a_ref, c_ref, init_ref, y_ref, fin_ref, bu0, bu1, bu2, st_scr, ut_lo, ut_hi,
                     yt_lo, yt_hi, *, tc, nc):
    d = pl.program_id(0)
    s = pl.program_id(2)
    bufs = (bu0, bu1, bu2)
    u_halves = ((ut_lo, slice(0, LANES)), (ut_hi, slice(LANES, 2 * LANES)))
    y_halves = ((yt_lo, slice(0, LANES)), (yt_hi, slice(LANES, 2 * LANES)))

    @pl.when(s == 0)
    def _():
        st_scr[...] = init_ref[0, 0]
        for buf in bufs:
            buf[...] = jnp.zeros_like(buf)

    def projections(b_buf, c_buf):
        for sg in range(N_SEG):
            u_s = u_ref[sg].astype(F32)
            for scr, cols in u_halves:
                scr[pl.ds(sg, tc, stride=N_SEG), :] = u_s[:, cols]
        u_tm = jnp.concatenate([ut_lo[...], ut_hi[...]], axis=1).astype(BF16)
        y = jnp.dot(c_buf[...].astype(BF16), c_ref[...], preferred_element_type=F32)
        b_buf[...] = jnp.dot(u_tm, b_ref[0], preferred_element_type=F32)
        for scr, cols in y_halves:
            scr[...] = y[:, cols]
        for sg in range(N_SEG):
            for scr, cols in y_halves:
                y_ref[0, sg, :, cols] = scr[pl.ds(sg, tc, stride=N_SEG), :]

    def recurrence(buf):
        w = SSM_COMPLEX // SCAN_COL_BLOCKS
        for cb in range(SCAN_COL_BLOCKS):
            re_cols = slice(cb * w, (cb + 1) * w)
            im_cols = slice(SSM_COMPLEX + cb * w, SSM_COMPLEX + (cb + 1) * w)
            a_re = jnp.broadcast_to(a_ref[0, 0:1, re_cols], (N_SEG, w))
            a_im = jnp.broadcast_to(a_ref[0, 1:2, re_cols], (N_SEG, w))

            def body(tt, carry, re_cols=re_cols, im_cols=im_cols, a_re=a_re, a_im=a_im):
                x_re, x_im = carry
                t = jnp.where(d == 0, tt, tc - 1 - tt)
                rows = pl.ds(pl.multiple_of(t * N_SEG, N_SEG), N_SEG)
                n_re = a_re * x_re - a_im * x_im + buf[rows, re_cols]
                n_im = a_re * x_im + a_im * x_re + buf[rows, im_cols]
                buf[rows, re_cols] = n_re
                buf[rows, im_cols] = n_im
                return n_re, n_im

            x_re, x_im = lax.fori_loop(0, tc, body, (st_scr[:, re_cols], st_scr[:, im_cols]), unroll=4)
            st_scr[:, re_cols] = x_re
            st_scr[:, im_cols] = x_im

    turn = lax.rem(s, 3)
    for k in range(3):
        @pl.when(turn == k)
        def _(k=k):
            projections(bufs[k], bufs[(k + 1) % 3])

    for k in range(3):
        @pl.when((turn == k) & (s >= 1) & (s <= nc))
        def _(k=k):
            recurrence(bufs[(k + 2) % 3])

    @pl.when(s == pl.num_programs(2) - 1)
    def _():
        fin_ref[0, 0] = st_scr[...]


def _ssm_scan_pipelined(ssm_proj, seg_len, n_groups, b_blk, a_rows, c_blk, init, tc):
    u = ssm_proj.reshape(-1, seg_len, 2 * SSM_WIDTH)
    nc = seg_len // tc

    def chunk(d, c):
        c = jnp.clip(c, 0, nc - 1)
        return jnp.where(d == 0, c, nc - 1 - c)

    rows = tc * N_SEG
    y, fin = pl.pallas_call(
        functools.partial(_ssm_pipe_kernel, tc=tc, nc=nc),
        grid=(2, n_groups, nc + 2),
        in_specs=[
            pl.BlockSpec((N_SEG, tc, SSM_WIDTH), lambda d, g, s: (g, chunk(d, s), 0)),
            pl.BlockSpec((1, SSM_WIDTH, SSM_REAL), lambda d, g, s: (d, 0, 0)),
            pl.BlockSpec((1, 2, SSM_COMPLEX), lambda d, g, s: (d, 0, 0)),
            pl.BlockSpec((SSM_REAL, SSM_WIDTH), lambda d, g, s: (0, 0)),
            pl.BlockSpec((1, 1, N_SEG, SSM_REAL), lambda d, g, s: (d, g, 0, 0)),
        ],
        out_specs=[
            pl.BlockSpec((1, N_SEG, tc, SSM_WIDTH), lambda d, g, s: (d, g, chunk(d, s - 2), 0)),
            pl.BlockSpec((1, 1, N_SEG, SSM_REAL), lambda d, g, s: (d, g, 0, 0)),
        ],
        out_shape=[
            jax.ShapeDtypeStruct((2, n_groups * N_SEG, seg_len, SSM_WIDTH), F32),
            jax.ShapeDtypeStruct((2, n_groups, N_SEG, SSM_REAL), F32),
        ],
        scratch_shapes=[pltpu.VMEM((rows, SSM_REAL), F32)] * 3 + [pltpu.VMEM((N_SEG, SSM_REAL), F32)]
        + [pltpu.VMEM((rows, LANES), F32)] * 4,
        compiler_params=_cparams("arbitrary", "arbitrary", "arbitrary"),
        name="ssm_scan_y",
    )(u, b_blk, a_rows, c_blk, init)
    return y.reshape(2, -1, SSM_WIDTH), fin


def _ssm_scan(ssm_proj, seg_len, first_group, n_groups, b_blk, a_rows, c_blk, init, tc, emit_y):
    u = ssm_proj.reshape(-1, seg_len, 2 * SSM_WIDTH)
    nc = seg_len // tc

    def chunk(d, c):
        return jnp.where(d == 0, c, nc - 1 - c)

    out_specs = [pl.BlockSpec((1, 1, N_SEG, SSM_REAL), lambda d, g, c: (d, g, 0, 0))]
    out_shape = [jax.ShapeDtypeStruct((2, n_groups, N_SEG, SSM_REAL), F32)]
    if emit_y:
        out_specs = [pl.BlockSpec((1, N_SEG, tc, SSM_WIDTH), lambda d, g, c: (d, g, chunk(d, c), 0))] + out_specs
        out_shape = [jax.ShapeDtypeStruct((2, n_groups * N_SEG, seg_len, SSM_WIDTH), F32)] + out_shape
    res = pl.pallas_call(
        functools.partial(_ssm_kernel, tc=tc, emit_y=emit_y),
        grid=(2, n_groups, nc),
        in_specs=[
            pl.BlockSpec((N_SEG, tc, SSM_WIDTH), lambda d, g, c: (first_group + g, chunk(d, c), 0)),
            pl.BlockSpec((1, SSM_WIDTH, SSM_REAL), lambda d, g, c: (d, 0, 0)),
            pl.BlockSpec((1, 2, SSM_COMPLEX), lambda d, g, c: (d, 0, 0)),
            pl.BlockSpec((SSM_REAL, SSM_WIDTH), lambda d, g, c: (0, 0)),
            pl.BlockSpec((1, 1, N_SEG, SSM_REAL), lambda d, g, c: (d, g, 0, 0)),
        ],
        out_specs=out_specs,
        out_shape=out_shape,
        scratch_shapes=[pltpu.VMEM((tc * N_SEG, SSM_REAL), F32), pltpu.VMEM((N_SEG, SSM_REAL), F32),
                        pltpu.VMEM((tc * N_SEG, LANES), F32), pltpu.VMEM((tc * N_SEG, LANES), F32)],
        compiler_params=_cparams("arbitrary", "arbitrary", "arbitrary"),
        name="ssm_scan_y" if emit_y else "ssm_scan_state",
    )(u, b_blk, a_rows, c_blk, init)
    if emit_y:
        return res[0].reshape(2, -1, SSM_WIDTH), res[1]
    return None, res[0]


def _ssm_params(lam_re, lam_im, log_step, b_re, b_im, c_re, c_im):
    lam = lax.complex(lam_re.astype(F32), lam_im.astype(F32))
    step = jnp.exp(log_step.astype(F32))[..., None]
    lam_bar = jnp.exp(lam * step)
    b = lax.complex(b_re.astype(F32), b_im.astype(F32))
    b_bar = ((lam_bar - 1.0) / lam)[..., None] * b
    eye = jnp.eye(SSM_GROUPS, dtype=F32)
    b_blk_re = jnp.einsum('dgph,gk->dghkp', jnp.real(b_bar), eye).reshape(2, SSM_WIDTH, SSM_COMPLEX)
    b_blk_im = jnp.einsum('dgph,gk->dghkp', jnp.imag(b_bar), eye).reshape(2, SSM_WIDTH, SSM_COMPLEX)
    b_blk = jnp.concatenate([b_blk_re, b_blk_im], axis=-1).astype(BF16)
    c_blk_re = jnp.einsum('ghp,gk->gpkh', c_re.astype(F32), eye).reshape(SSM_COMPLEX, SSM_WIDTH)
    c_blk_im = jnp.einsum('ghp,gk->gpkh', c_im.astype(F32), eye).reshape(SSM_COMPLEX, SSM_WIDTH)
    c_blk = jnp.concatenate([c_blk_re, -c_blk_im], axis=0).astype(BF16)
    a_rows = jnp.stack([jnp.real(lam_bar).reshape(2, SSM_COMPLEX),
                        jnp.imag(lam_bar).reshape(2, SSM_COMPLEX)], axis=1)
    return lam_bar.reshape(2, SSM_COMPLEX), b_blk, a_rows, c_blk


def _chain_init(fin, lam_pow):
    f = lax.complex(fin[..., :SSM_COMPLEX], fin[..., SSM_COMPLEX:])
    zero = jnp.zeros_like(f[0, 0])
    fwd = [zero]
    for s in range(1, N_SEG):
        fwd.append(lam_pow[0] * fwd[-1] + f[0, s - 1])
    bwd = [zero]
    for s in range(N_SEG - 2, -1, -1):
        bwd.append(lam_pow[1] * bwd[-1] + f[1, s + 1])
    init = jnp.stack([jnp.stack(fwd), jnp.stack(bwd[::-1])])
    return jnp.concatenate([jnp.real(init), jnp.imag(init)], axis=-1).astype(F32)


def _attn_sgu_kernel(flags_ref, q_ref, kp_ref, kc_ref, kn_ref, vp_ref, vc_ref, vn_ref, bias_ref, sink_ref,
                     sgu_ref, sgn_ref, sgw_ref, sgb_ref, attn_ref, sguo_ref, *, nbq):
    n = pl.program_id(0)
    no_prev = flags_ref[0, n * nbq] == 0
    no_next = flags_ref[1, n * nbq + nbq - 1] == 0

    def k_block(i):
        if i == 0:
            return kp_ref[...]
        if i == nbq + 1:
            return kn_ref[...]
        return kc_ref[(i - 1) * BLOCK:i * BLOCK, :]

    def vt_block(i):
        if i == 0:
            return vp_ref[...]
        if i == nbq + 1:
            return vn_ref[...]
        return vc_ref[:, (i - 1) * BLOCK:i * BLOCK]

    for b in range(nbq):
        rows = slice(b * BLOCK, (b + 1) * BLOCK)
        q = q_ref[rows, :]
        outs = []
        for j in range(N_KV_HEADS):
            qs = jnp.concatenate(
                [q[:, (j * Q_PER_KV + i) * HEAD_DIM:(j * Q_PER_KV + i + 1) * HEAD_DIM] for i in range(Q_PER_KV)],
                axis=0)
            kv = slice(j * HEAD_DIM, (j + 1) * HEAD_DIM)
            s = []
            for i in range(3):
                si = lax.dot_general(k_block(b + i)[:, kv], qs, (((1,), (1,)), ((), ())),
                                     preferred_element_type=F32)
                si = si + bias_ref[j, i * BLOCK:(i + 1) * BLOCK, :]
                if b == 0 and i == 0:
                    si = jnp.where(no_prev, NEG_INF, si)
                if b == nbq - 1 and i == 2:
                    si = jnp.where(no_next, NEG_INF, si)
                s.append(si)
            sink = sink_ref[j]
            m = jnp.max(jnp.maximum(jnp.maximum(s[0], s[1]), s[2]), axis=0, keepdims=True)
            m = jnp.maximum(m, sink)
            p = [jnp.exp(si - m) for si in s]
            denom = jnp.sum(p[0] + p[1] + p[2], axis=0, keepdims=True) + jnp.exp(sink - m)
            inv = 1.0 / denom
            o_t = sum(jnp.dot(vt_block(b + i)[kv, :], (p[i] * inv).astype(BF16), preferred_element_type=F32)
                      for i in range(3))
            outs += [jnp.transpose(o_t[:, i * BLOCK:(i + 1) * BLOCK]) for i in range(Q_PER_KV)]
        attn_ref[rows, :] = jnp.concatenate(outs, axis=1).astype(BF16)

        gu = jax.nn.gelu(sgu_ref[rows, :SGU_WIDTH].astype(F32))
        gv = _rms(jax.nn.gelu(sgu_ref[rows, SGU_WIDTH:].astype(F32)), sgn_ref[...]).astype(BF16)
        mixed = jnp.concatenate(
            [jnp.dot(sgw_ref[g], gv[:, g * SGU_GROUP_WIDTH:(g + 1) * SGU_GROUP_WIDTH], preferred_element_type=F32)
             for g in range(SGU_GROUPS)], axis=1) + sgb_ref[...]
        sguo_ref[rows, :] = (gu * mixed).astype(BF16)


def _attn_sgu(flags, qk, vt, sgu, bias, sink, sgn, sgw, sgb, nbq):
    t = qk.shape[0]
    nb = t // BLOCK
    tile = nbq * BLOCK
    kcol = ATTN_WIDTH // KV_WIDTH
    before = lambda n: jnp.maximum(n * nbq - 1, 0)
    after = lambda n: jnp.minimum((n + 1) * nbq, nb - 1)
    grid_spec = pltpu.PrefetchScalarGridSpec(
        num_scalar_prefetch=1,
        grid=(nb // nbq,),
        in_specs=[
            pl.BlockSpec((tile, ATTN_WIDTH), lambda n, f: (n, 0)),
            pl.BlockSpec((BLOCK, KV_WIDTH), lambda n, f: (before(n), kcol)),
            pl.BlockSpec((tile, KV_WIDTH), lambda n, f: (n, kcol)),
            pl.BlockSpec((BLOCK, KV_WIDTH), lambda n, f: (after(n), kcol)),
            pl.BlockSpec((KV_WIDTH, BLOCK), lambda n, f: (0, before(n))),
            pl.BlockSpec((KV_WIDTH, tile), lambda n, f: (0, n)),
            pl.BlockSpec((KV_WIDTH, BLOCK), lambda n, f: (0, after(n))),
            pl.BlockSpec((N_KV_HEADS, 3 * BLOCK, Q_PER_KV * BLOCK), lambda n, f: (0, 0, 0)),
            pl.BlockSpec((N_KV_HEADS, 1, Q_PER_KV * BLOCK), lambda n, f: (0, 0, 0)),
            pl.BlockSpec((tile, 2 * SGU_WIDTH), lambda n, f: (n, 0)),
            pl.BlockSpec((1, SGU_WIDTH), lambda n, f: (0, 0)),
            pl.BlockSpec((SGU_GROUPS, BLOCK, BLOCK), lambda n, f: (0, 0, 0)),
            pl.BlockSpec((BLOCK, SGU_WIDTH), lambda n, f: (0, 0)),
        ],
        out_specs=[
            pl.BlockSpec((tile, ATTN_WIDTH), lambda n, f: (n, 0)),
            pl.BlockSpec((tile, SGU_WIDTH), lambda n, f: (n, 0)),
        ],
    )
    return pl.pallas_call(
        functools.partial(_attn_sgu_kernel, nbq=nbq),
        grid_spec=grid_spec,
        out_shape=[jax.ShapeDtypeStruct((t, ATTN_WIDTH), BF16), jax.ShapeDtypeStruct((t, SGU_WIDTH), BF16)],
        compiler_params=_cparams("arbitrary"),
        name="attn_sgu",
    )(flags, qk, qk, qk, qk, vt, vt, vt, bias, sink, sgu, sgn, sgw, sgb)


def _t5_bucket(rel):
    half = N_BUCKETS // 2
    max_exact = half // 2
    ret = jnp.where(rel > 0, half, 0)
    n = jnp.abs(rel)
    nf = jnp.maximum(n, 1).astype(F32)
    large = max_exact + (jnp.log(nf / max_exact) / math.log(MAX_DISTANCE / max_exact)
                         * (half - max_exact)).astype(jnp.int32)
    large = jnp.minimum(large, half - 1)
    return ret + jnp.where(n < max_exact, n, large)


def _band_bias(rel_table):
    q_pos = jnp.arange(BLOCK)[:, None]
    k_pos = jnp.arange(3 * BLOCK)[None, :] - BLOCK
    rel = k_pos - q_pos
    hit = _t5_bucket(rel)[..., None] == jnp.arange(N_BUCKETS)
    bias = jnp.sum(jnp.where(hit[..., None], rel_table.astype(F32), 0.0), axis=2)
    bias = jnp.where((jnp.abs(rel) <= BLOCK)[..., None], bias, NEG_INF)
    bias = bias.reshape(BLOCK, 3 * BLOCK, N_KV_HEADS, Q_PER_KV)
    return bias.transpose(2, 1, 3, 0).reshape(N_KV_HEADS, 3 * BLOCK, Q_PER_KV * BLOCK)


def _outproj_router_kernel(*refs, tm, n_parts, steps_per_part):
    x_refs = refs[:n_parts]
    (ssm_ref, yf_ref, yb_ref, attn_ref, sguo_ref, d_ref, gout_ref, wout_ref, gffn_ref, wrh_ref, wrl_ref, br_ref,
     x1_ref, h_ref, idx_ref, rank_ref, wgt_ref, cnt_ref) = refs[n_parts:]
    u = ssm_ref[:, :SSM_WIDTH].astype(F32)
    gate = ssm_ref[:, SSM_WIDTH:].astype(F32)
    ya = yf_ref[...] + yb_ref[...] + d_ref[...] * u
    ya = jax.nn.gelu(ya) * jax.nn.sigmoid(gate)
    g = gout_ref[...]
    mixed = jnp.concatenate([
        _rms(ya, g[:, :SSM_WIDTH]),
        _rms(attn_ref[...].astype(F32), g[:, SSM_WIDTH:SSM_WIDTH + ATTN_WIDTH]),
        _rms(sguo_ref[...].astype(F32), g[:, SSM_WIDTH + ATTN_WIDTH:]),
    ], axis=1).astype(BF16)
    x1 = _pick_part(x_refs, steps_per_part) + jnp.dot(mixed, wout_ref[...], preferred_element_type=F32)
    x1_ref[...] = x1
    h = _rms(x1, gffn_ref[...])
    _store_token_tiles(h_ref, 0, h, tm)

    h_hi = h.astype(BF16)
    h_lo = (h - h_hi.astype(F32)).astype(BF16)
    nt_dot = lambda a, b: lax.dot_general(a, b, (((1,), (1,)), ((), ())), preferred_element_type=F32)
    lg = nt_dot(wrh_ref[...], h_hi) + nt_dot(wrl_ref[...], h_hi) + nt_dot(wrh_ref[...], h_lo) + br_ref[...]
    row = lax.broadcasted_iota(jnp.int32, (N_EXPERTS, tm), 0)
    vals, idxs, sels = [], [], []
    for _ in range(TOP_K):
        m = jnp.max(lg, axis=0, keepdims=True)
        idx = jnp.min(jnp.where(lg == m, row, N_EXPERTS), axis=0, keepdims=True)
        sel = row == idx
        lg = jnp.where(sel, -jnp.inf, lg)
        vals.append(m)
        idxs.append(idx)
        sels.append(sel)
    e = [jnp.exp(v - vals[0]) for v in vals]
    tot = e[0] + e[1] + e[2] + e[3]
    wgt_ref[...] = jnp.concatenate([ek / tot for ek in e], axis=0)
    idx_ref[...] = jnp.concatenate(idxs, axis=0)

    picked = (sels[0] | sels[1] | sels[2] | sels[3])
    pm = jnp.where(picked, 1.0, 0.0)
    r_io = lax.broadcasted_iota(jnp.int32, (tm, tm), 0)
    c_io = lax.broadcasted_iota(jnp.int32, (tm, tm), 1)
    earlier = jnp.where(r_io < c_io, 1.0, 0.0).astype(BF16)
    before = jnp.dot(pm.astype(BF16), earlier, preferred_element_type=F32)
    rank_ref[...] = jnp.concatenate(
        [jnp.sum(jnp.where(s, before, 0.0), axis=0, keepdims=True) for s in sels], axis=0).astype(jnp.int32)
    cnt_ref[0] = jnp.sum(pm, axis=1, keepdims=True).astype(jnp.int32)


def _outproj_router(x_parts, ssm, y_dirs, attn, sguo, d_skip, g_out, w_out, g_ffn, w_r, b_r, tm):
    t = ssm.shape[0]
    row = lambda w: pl.BlockSpec((tm, w), lambda i: (i, 0))
    full = lambda a, b: pl.BlockSpec((a, b), lambda i: (0, 0))
    direction = lambda d: pl.BlockSpec((None, tm, SSM_WIDTH), lambda i: (d, i, 0))
    per_choice = pl.BlockSpec((TOP_K, tm), lambda i: (0, i))
    w_rt = w_r.astype(F32).T
    w_hi = w_rt.astype(BF16)
    w_lo = (w_rt - w_hi.astype(F32)).astype(BF16)
    return pl.pallas_call(
        functools.partial(_outproj_router_kernel, tm=tm, n_parts=len(x_parts),
                          steps_per_part=x_parts[0].shape[0] // tm),
        grid=(t // tm,),
        in_specs=_part_specs(x_parts, tm) + [
            row(2 * SSM_WIDTH), direction(0), direction(1), row(ATTN_WIDTH), row(SGU_WIDTH),
            full(1, SSM_WIDTH), full(1, D_MODEL), full(D_MODEL, D_MODEL), full(1, D_MODEL),
            full(N_EXPERTS, D_MODEL), full(N_EXPERTS, D_MODEL), full(N_EXPERTS, 1),
        ],
        out_specs=[row(D_MODEL), pl.BlockSpec((tm * TOKEN_TILE_ROWS, LANES), lambda i: (i, 0)),
                   per_choice, per_choice, per_choice, pl.BlockSpec((1, N_EXPERTS, 1), lambda i: (i, 0, 0))],
        out_shape=[
            jax.ShapeDtypeStruct((t, D_MODEL), F32),
            jax.ShapeDtypeStruct((t * TOKEN_TILE_ROWS, LANES), F32),
            jax.ShapeDtypeStruct((TOP_K, t), jnp.int32),
            jax.ShapeDtypeStruct((TOP_K, t), jnp.int32),
            jax.ShapeDtypeStruct((TOP_K, t), F32),
            jax.ShapeDtypeStruct((t // tm, N_EXPERTS, 1), jnp.int32),
        ],
        compiler_params=_cparams("arbitrary"),
        name="outproj_router",
    )(*x_parts, ssm, y_dirs, y_dirs, attn, sguo, d_skip, g_out, w_out, g_ffn, w_hi, w_lo,
      b_r.astype(F32).reshape(N_EXPERTS, 1))


ISSUE_UNROLL = 8
FETCH_ROWS = 16


def _for_rows(n, fn):
    def body(c, carry):
        for u in range(ISSUE_UNROLL):
            fn(c * ISSUE_UNROLL + u, u % 2)
        return carry

    lax.fori_loop(0, n // ISSUE_UNROLL, body, 0)


def _tile_rows(ref, first_tile, n_tiles=1):
    start = first_tile * TOKEN_TILE_ROWS
    if not isinstance(start, int):
        start = pl.multiple_of(start, TOKEN_TILE_ROWS)
    return ref.at[pl.ds(start, n_tiles * TOKEN_TILE_ROWS)]


EXPERT_COL_CHUNK = 256


GATHER_SLOTS = 3


def _experts_kernel(ltile_ref, otile_ref, exp_ref, lo_ref, hi_ref, tok_ref, tok1_ref, tok2_ref, wrow_ref, h_hbm, wgu_ref,
                    bgu_ref, wd_ref, bd_ref, out_ref, x_buf, y_buf, wgu_bf, wd_bf, gsem, *, tm):
    w = pl.program_id(0)
    nw = pl.num_programs(0)
    slot = lax.rem(w, GATHER_SLOTS)
    ahead = lax.rem(w + GATHER_SLOTS - 1, GATHER_SLOTS)
    prev = jnp.maximum(w - 1, 0)

    def gather_row(list_ref, to_slot, r, queue):
        pltpu.make_async_copy(_tile_rows(h_hbm, list_ref[0, 0, r]), _tile_rows(x_buf, to_slot * tm + r),
                              gsem.at[to_slot]).start(priority=queue)

    def wait_gather(of_slot):
        pltpu.make_async_copy(_tile_rows(h_hbm, 0, tm), _tile_rows(x_buf, of_slot * tm, tm), gsem.at[of_slot]).wait()

    @pl.when(w == 0)
    def _():
        _for_rows(tm, lambda r, queue: gather_row(tok_ref, 0, r, queue))
        _for_rows(tm, lambda r, queue: gather_row(tok1_ref, 1, r, queue))

    @pl.when((w == 0) | (exp_ref[w] != exp_ref[prev]))
    def _():
        wgu_bf[...] = wgu_ref[0].astype(BF16)
        wd_bf[...] = wd_ref[0].astype(BF16)

    wait_gather(slot)

    n_pieces = (D_EXPERT + D_MODEL) // EXPERT_COL_CHUNK
    per_piece = tm // n_pieces
    piece = 0

    def issue_next_gather():
        nonlocal piece
        for r in range(piece * per_piece, (piece + 1) * per_piece):
            gather_row(tok2_ref, ahead, r, r % 2)
        piece += 1

    w_rows = jnp.transpose(jnp.broadcast_to(wrow_ref[0], (LANES, tm)))
    w_rows = jnp.concatenate([w_rows] * (EXPERT_COL_CHUNK // LANES), axis=1)

    base = pl.multiple_of(slot * (tm * TOKEN_TILE_ROWS), tm * TOKEN_TILE_ROWS)
    x = _load_token_tiles(x_buf, base, tm).astype(BF16)
    acts = []
    for c in range(D_EXPERT // EXPERT_COL_CHUNK):
        issue_next_gather()
        g_cols = slice(c * EXPERT_COL_CHUNK, (c + 1) * EXPERT_COL_CHUNK)
        u_cols = slice(D_EXPERT + c * EXPERT_COL_CHUNK, D_EXPERT + (c + 1) * EXPERT_COL_CHUNK)
        gate = jnp.dot(x, wgu_bf[:, g_cols], preferred_element_type=F32) + bgu_ref[0, :, g_cols]
        up = jnp.dot(x, wgu_bf[:, u_cols], preferred_element_type=F32) + bgu_ref[0, :, u_cols]
        gate = jnp.minimum(gate, SWIGLU_LIMIT)
        up = jnp.clip(up, -SWIGLU_LIMIT, SWIGLU_LIMIT)
        acts.append(((up + 1.0) * gate * jax.nn.sigmoid(SWIGLU_ALPHA * gate)).astype(BF16))
    act = jnp.concatenate(acts, axis=1)
    tiles_per_chunk = EXPERT_COL_CHUNK // LANES
    for c in range(D_MODEL // EXPERT_COL_CHUNK):
        issue_next_gather()
        cols = slice(c * EXPERT_COL_CHUNK, (c + 1) * EXPERT_COL_CHUNK)
        y = (jnp.dot(act, wd_bf[:, cols], preferred_element_type=F32) + bd_ref[0, :, cols]) * w_rows
        for j in range(tiles_per_chunk):
            y_buf[pl.ds(c * tiles_per_chunk + j, tm, stride=TOKEN_TILE_ROWS), :] = y[:, j * LANES:(j + 1) * LANES]

    first = (w == 0) | (otile_ref[w] != otile_ref[prev])

    @pl.when(first)
    def _():
        out_ref[...] = y_buf[...]

    @pl.when(jnp.logical_not(first))
    def _():
        row = lax.shift_right_logical(lax.broadcasted_iota(jnp.int32, y_buf.shape, 0),
                                      TOKEN_TILE_ROWS.bit_length() - 1)
        mine = (row >= lo_ref[w]) & (row < hi_ref[w])
        out_ref[...] = jnp.where(mine, y_buf[...], out_ref[...])

    @pl.when(w == nw - 1)
    def _():
        for k in range(1, GATHER_SLOTS):
            wait_gather(lax.rem(w + k, GATHER_SLOTS))


def _experts(items, tok_sorted, w_sorted, h_tiles, wgu, bgu, wd, bd, layer, tm):
    nw = items[0].shape[0]
    n_rows = tok_sorted.shape[0]
    nt = n_rows // tm
    lists = lambda a: a.reshape(nt, 1, tm)
    later = lambda k: (lambda i, lt, ot, ex, lo, hi: (lt[jnp.minimum(i + k, nw - 1)], 0, 0))
    by_expert = lambda i, lt, ot, ex, lo, hi: (layer, ex[i], 0, 0)
    grid_spec = pltpu.PrefetchScalarGridSpec(
        num_scalar_prefetch=5,
        grid=(nw,),
        in_specs=[
            pl.BlockSpec((1, 1, tm), later(0), memory_space=pltpu.SMEM),
            pl.BlockSpec((1, 1, tm), later(1), memory_space=pltpu.SMEM),
            pl.BlockSpec((1, 1, tm), later(2), memory_space=pltpu.SMEM),
            pl.BlockSpec((1, 1, tm), later(0)),
            pl.BlockSpec(memory_space=pl.ANY),
            pl.BlockSpec((None, 1, D_MODEL, 2 * D_EXPERT), by_expert),
            pl.BlockSpec((None, 1, 1, 2 * D_EXPERT), by_expert),
            pl.BlockSpec((None, 1, D_EXPERT, D_MODEL), by_expert),
            pl.BlockSpec((None, 1, 1, D_MODEL), by_expert),
        ],
        out_specs=pl.BlockSpec((tm * TOKEN_TILE_ROWS, LANES), lambda i, lt, ot, ex, lo, hi: (ot[i], 0)),
        scratch_shapes=[
            pltpu.VMEM((GATHER_SLOTS * tm * TOKEN_TILE_ROWS, LANES), F32),
            pltpu.VMEM((tm * TOKEN_TILE_ROWS, LANES), F32),
            pltpu.VMEM((D_MODEL, 2 * D_EXPERT), BF16),
            pltpu.VMEM((D_EXPERT, D_MODEL), BF16),
            pltpu.SemaphoreType.DMA((GATHER_SLOTS,)),
        ],
    )
    return pl.pallas_call(
        functools.partial(_experts_kernel, tm=tm),
        grid_spec=grid_spec,
        out_shape=jax.ShapeDtypeStruct(((n_rows + FETCH_ROWS) * TOKEN_TILE_ROWS, LANES), F32),
        compiler_params=_cparams("arbitrary"),
        name="experts",
    )(*items, lists(tok_sorted), lists(tok_sorted), lists(tok_sorted), lists(w_sorted), h_tiles, wgu, bgu, wd, bd)


def _combine_kernel(src_ref, cst_ref, nch_ref, y_hbm, x1_ref, qpos_ref, g_ref, *rest, tm, nq, final, n_out):
    out_refs, (yc_buf, sem) = rest[:n_out], rest[n_out:]
    i = pl.program_id(0)
    slot = lax.rem(i, 2)

    def chunk_copy(src_row, dst_row, at_slot):
        return pltpu.make_async_copy(_tile_rows(y_hbm, src_row, FETCH_ROWS),
                                     _tile_rows(yc_buf, at_slot * nq + dst_row, FETCH_ROWS), sem.at[at_slot])

    def fetch_tile(tile, to_slot):
        for e in range(N_EXPERTS):
            src0 = src_ref[tile * N_EXPERTS + e]
            dst0 = cst_ref[tile * N_EXPERTS + e]

            def fetch(j, carry, src0=src0, dst0=dst0):
                chunk_copy(src0 + j * FETCH_ROWS, dst0 + j * FETCH_ROWS, to_slot).start()
                return carry

            lax.fori_loop(0, nch_ref[tile * N_EXPERTS + e], fetch, 0)

    @pl.when(i == 0)
    def _():
        yc_buf[...] = jnp.zeros_like(yc_buf)
        fetch_tile(0, 0)

    @pl.when(i + 1 < pl.num_programs(0))
    def _():
        fetch_tile(i + 1, 1 - slot)

    total = nch_ref[i * N_EXPERTS]
    for e in range(1, N_EXPERTS):
        total = total + nch_ref[i * N_EXPERTS + e]

    def wait(j, carry):
        chunk_copy(0, 0, slot).wait()
        return carry

    lax.fori_loop(0, total, wait, 0)

    q = lax.broadcasted_iota(jnp.int32, (tm, nq), 1)
    qpos = qpos_ref[...]
    hit = q == qpos[:, 0:1]
    for k in range(1, TOP_K):
        hit = hit | (q == qpos[:, k:k + 1])
    p = jnp.where(hit, 1.0, 0.0).astype(BF16)
    base = pl.multiple_of(slot * (nq * TOKEN_TILE_ROWS), nq * TOKEN_TILE_ROWS)
    yc = _load_token_tiles(yc_buf, base, nq).astype(BF16)
    x2 = x1_ref[...] + jnp.dot(p, yc, preferred_element_type=F32)
    if final:
        x2 = _rms(x2, g_ref[...])
    steps_per_out = pl.num_programs(0) // n_out
    for j, out_ref in enumerate(out_refs):
        @pl.when((i >= j * steps_per_out) & (i < (j + 1) * steps_per_out))
        def _(out_ref=out_ref):
            out_ref[...] = x2


def _combine(src0, cstart, nchunk, y_sorted, x1, qpos, g_final, tm, final, n_out):
    t = x1.shape[0]
    steps_per_out = t // tm // n_out
    nq = -(-(tm * TOP_K + N_EXPERTS * (FETCH_ROWS - 1)) // LANES) * LANES
    grid_spec = pltpu.PrefetchScalarGridSpec(
        num_scalar_prefetch=3,
        grid=(t // tm,),
        in_specs=[
            pl.BlockSpec(memory_space=pl.ANY),
            pl.BlockSpec((tm, D_MODEL), lambda i, a, b, c: (i, 0)),
            pl.BlockSpec((tm, TOP_K), lambda i, a, b, c: (i, 0)),
            pl.BlockSpec((1, D_MODEL), lambda i, a, b, c: (0, 0)),
        ],
        out_specs=[pl.BlockSpec((tm, D_MODEL), (lambda j: lambda i, a, b, c: (
            jnp.clip(i - j * steps_per_out, 0, steps_per_out - 1), 0))(j)) for j in range(n_out)],
        scratch_shapes=[pltpu.VMEM((2 * nq * TOKEN_TILE_ROWS, LANES), F32), pltpu.SemaphoreType.DMA((2,))],
    )
    return pl.pallas_call(
        functools.partial(_combine_kernel, tm=tm, nq=nq, final=final, n_out=n_out),
        grid_spec=grid_spec,
        out_shape=[jax.ShapeDtypeStruct((t // n_out, D_MODEL), F32)] * n_out,
        compiler_params=_cparams("arbitrary"),
        name="combine",
    )(src0.reshape(-1), cstart.reshape(-1), nchunk.reshape(-1), y_sorted, x1, qpos, g_final)


def _routing_lists(idx, lrank, wgt, tcnt, tm_tok, tm):
    t = idx.shape[1]
    n_rows = t * TOP_K
    nt = n_rows // tm
    cnt = jnp.sum(tcnt, axis=0)
    group_end = jnp.cumsum(cnt)
    group_start = group_end - cnt
    src0 = group_start[None, :] + jnp.cumsum(tcnt, axis=0) - tcnt
    aligned = -(-tcnt // FETCH_ROWS) * FETCH_ROWS
    cstart = jnp.cumsum(aligned, axis=1) - aligned
    nchunk = aligned // FETCH_ROWS
    experts = jnp.arange(N_EXPERTS, dtype=jnp.int32)
    onehot = idx.reshape(TOP_K, -1, tm_tok)[..., None] == experts
    pick = lambda table: jnp.sum(jnp.where(onehot, table[None, :, None, :], 0), axis=-1).reshape(TOP_K, t)
    pos = lrank + pick(src0)
    qpos = (lrank + pick(cstart)).T
    flat = jnp.arange(n_rows, dtype=jnp.int32)
    _, flat_sorted, w_sorted = lax.sort((pos.reshape(-1).astype(jnp.int32), flat, wgt.reshape(-1)), num_keys=1)
    tok_sorted = flat_sorted % t
    bounds = jnp.sort(jnp.concatenate([jnp.arange(nt, dtype=jnp.int32) * tm, group_start.astype(jnp.int32)]))
    nxt = jnp.concatenate([bounds[1:], jnp.array([n_rows], jnp.int32)])
    tile = jnp.minimum(bounds // tm, nt - 1)
    lo = bounds - tile * tm
    hi = jnp.where(nxt > bounds, nxt - tile * tm, lo)
    expert = jnp.minimum(jnp.sum(group_end[None, :] <= bounds[:, None], axis=-1), N_EXPERTS - 1)
    last = lambda a, v: jnp.concatenate([a, jnp.array([v], a.dtype)])
    items = (last(tile, nt - 1), last(tile, nt), jnp.concatenate([expert, expert[-1:]]), last(lo, 0), last(hi, 0))
    items = tuple(a.astype(jnp.int32) for a in items)
    to_i32 = lambda a: a.astype(jnp.int32)
    return items, tok_sorted, w_sorted, to_i32(src0), to_i32(cstart), to_i32(nchunk), to_i32(qpos)


def _block_flags(n_prompt_seq, seg_len, n_tokens):
    nb = n_tokens // BLOCK
    per_seg = seg_len // BLOCK
    b = jnp.arange(nb)
    prompt_blocks = n_prompt_seq * per_seg
    in_prompt = b < prompt_blocks
    first = jnp.where(in_prompt, b % per_seg == 0, b == prompt_blocks)
    last = jnp.where(in_prompt, b % per_seg == per_seg - 1, b == nb - 1)
    return jnp.stack([~first, ~last]).astype(jnp.int32)


def _trunk(x_parts, n_prompt_seq, seg_len, p, tm, tm_moe, tc, nbq):
    n_parts = len(x_parts)
    x = list(x_parts)
    t = sum(a.shape[0] for a in x)
    depth = p['w_in'].shape[0]
    flags = _block_flags(n_prompt_seq, seg_len, t)
    bias = _band_bias(p['rel_bias'])
    zero_init = jnp.zeros((2, 1, N_SEG, SSM_REAL), F32)
    for layer in range(depth):
        ssm, qk, vt, sgu = _inproj(x, p['norm_mix'][layer][None], p['w_in'][layer].astype(BF16), tm)

        lam_bar, b_blk, a_rows, c_blk = _ssm_params(
            p['ssm_lam_re'][layer], p['ssm_lam_im'][layer], p['ssm_log_step'][layer], p['ssm_b_re'][layer],
            p['ssm_b_im'][layer], p['ssm_c_re'][layer], p['ssm_c_im'][layer])
        _, fin = _ssm_scan(ssm, seg_len, 1, 1, b_blk, a_rows, c_blk, zero_init, tc, False)
        init = jnp.concatenate([zero_init, _chain_init(fin[:, 0], lam_bar ** seg_len)[:, None]], axis=1)
        y, _ = _ssm_scan_pipelined(ssm, seg_len, 2, b_blk, a_rows, c_blk, init, tc)

        sink = jnp.repeat(p['attn_sink'][layer].astype(F32), BLOCK).reshape(N_KV_HEADS, 1, Q_PER_KV * BLOCK)
        sgb = jnp.repeat(p['sgu_b'][layer].astype(F32).T, SGU_GROUP_WIDTH, axis=1)
        attn, sguo = _attn_sgu(flags, qk, vt, sgu, bias, sink, p['sgu_norm'][layer][None].astype(F32),
                               p['sgu_w'][layer].astype(BF16), sgb, nbq)

        x1, h, idx, rank, wgt, cnt = _outproj_router(
            x, ssm, y, attn, sguo, p['ssm_d'][layer][None].astype(F32), p['out_norm'][layer][None],
            p['w_out'][layer].astype(BF16), p['norm_ffn'][layer][None], p['w_router'][layer].astype(F32),
            p['b_router'][layer][None].astype(F32), tm)

        items, tok_sorted, w_sorted, src0, cstart, nchunk, qpos = _routing_lists(
            idx, rank, wgt, cnt[:, :, 0], tm, tm_moe)
        y_sorted = _experts(items, tok_sorted, w_sorted, h, p['w_gate_up'].astype(F32),
                            p['b_gate_up'][:, :, None].astype(F32), p['w_down'].astype(F32),
                            p['b_down'][:, :, None].astype(F32), layer, tm_moe)
        last = layer == depth - 1
        x = _combine(src0, cstart, nchunk, y_sorted, x1, qpos, p['final_norm'][None].astype(F32), tm,
                     last, n_parts if last else 1)
    return x


def _run(x_prompt, x_sample, p, tm=512, tm_moe=512, tc=64, nbq=2):
    bsz, seg_len, _ = x_prompt.shape
    assert bsz == N_SEG and x_sample.shape[0] == 1 and x_sample.shape[1] == N_SEG * seg_len
    assert seg_len % (nbq * BLOCK) == 0 and seg_len % tc == 0 and seg_len % tm == 0
    parts = [x_prompt.reshape(-1, D_MODEL).astype(F32), x_sample.reshape(-1, D_MODEL).astype(F32)]
    y_prompt, y_sample = _trunk(parts, bsz, seg_len, p, tm, tm_moe, tc, nbq)
    return y_prompt.reshape(x_prompt.shape), y_sample.reshape(x_sample.shape)


def kernel(x_prompt, x_sample, norm_mix, w_in, ssm_lam_re, ssm_lam_im, ssm_log_step, ssm_b_re, ssm_b_im, ssm_c_re, ssm_c_im, ssm_d, attn_sink, rel_bias, sgu_norm, sgu_w, sgu_b, out_norm, w_out, norm_ffn, w_router, b_router, w_gate_up, b_gate_up, w_down, b_down, final_norm):
    p = dict(norm_mix=norm_mix, w_in=w_in, ssm_lam_re=ssm_lam_re, ssm_lam_im=ssm_lam_im, ssm_log_step=ssm_log_step,
             ssm_b_re=ssm_b_re, ssm_b_im=ssm_b_im, ssm_c_re=ssm_c_re, ssm_c_im=ssm_c_im, ssm_d=ssm_d,
             attn_sink=attn_sink, rel_bias=rel_bias, sgu_norm=sgu_norm, sgu_w=sgu_w, sgu_b=sgu_b, out_norm=out_norm,
             w_out=w_out, norm_ffn=norm_ffn, w_router=w_router, b_router=b_router, w_gate_up=w_gate_up,
             b_gate_up=b_gate_up, w_down=w_down, b_down=b_down, final_norm=final_norm)
    return _run(x_prompt, x_sample, p)
```

```python
import functools
import math

import jax
import jax.numpy as jnp
from jax import lax
from jax.experimental import pallas as pl
from jax.experimental.pallas import tpu as pltpu

D_MODEL = 1024
SSM_WIDTH = 256
SSM_GROUP = 16
SSM_GROUPS = 16
SSM_STATE = 64
SSM_COMPLEX = SSM_GROUPS * SSM_STATE
SSM_REAL = 2 * SSM_COMPLEX
ATTN_WIDTH = 512
HEAD_DIM = 64
N_Q_HEADS = 8
N_KV_HEADS = 2
Q_PER_KV = 4
KV_WIDTH = 128
BLOCK = 128
N_BUCKETS = 32
MAX_DISTANCE = 128
SGU_WIDTH = 256
SGU_GROUPS = 4
SGU_GROUP_WIDTH = 64
IN_COLS = 1792
QKV_COLS = ATTN_WIDTH + 2 * KV_WIDTH
N_EXPERTS = 32
TOP_K = 4
D_EXPERT = 1024
SWIGLU_LIMIT = 7.0
SWIGLU_ALPHA = 1.702
EPS = 1e-6
NEG_INF = -1e30

N_SEG = 8
V7X_VMEM_LIMIT = 56 * 1024 * 1024

F32 = jnp.float32
BF16 = jnp.bfloat16


def _rms(xf, g):
    return xf * lax.rsqrt(jnp.mean(xf * xf, axis=-1, keepdims=True) + EPS) * g


LANES = 128
TOKEN_TILE_ROWS = D_MODEL // LANES


def _load_token_tiles(ref, base, n):
    return jnp.concatenate(
        [ref[pl.ds(base + j, n, stride=TOKEN_TILE_ROWS), :] for j in range(TOKEN_TILE_ROWS)], axis=1)


def _store_token_tiles(ref, base, val, n):
    for j in range(TOKEN_TILE_ROWS):
        ref[pl.ds(base + j, n, stride=TOKEN_TILE_ROWS), :] = val[:, j * LANES:(j + 1) * LANES]


def _cparams(*sem):
    return pltpu.CompilerParams(dimension_semantics=sem, vmem_limit_bytes=V7X_VMEM_LIMIT)


def _part_specs(parts, tm):
    per = parts[0].shape[0] // tm
    return [pl.BlockSpec((tm, D_MODEL), (lambda j: lambda i, *_: (jnp.clip(i - j * per, 0, per - 1), 0))(j))
            for j in range(len(parts))]


def _pick_part(x_refs, steps_per_part):
    i = pl.program_id(0)
    x = x_refs[0][...]
    for j in range(1, len(x_refs)):
        x = jnp.where(i >= j * steps_per_part, x_refs[j][...], x)
    return x


def _inproj_kernel(*refs, n_parts, steps_per_part):
    x_refs, (g_ref, w_ref, ssm_ref, qk_ref, vt_ref, sgu_ref) = refs[:n_parts], refs[n_parts:]
    h = _rms(_pick_part(x_refs, steps_per_part), g_ref[...]).astype(BF16)
    p = jnp.dot(h, w_ref[...], preferred_element_type=F32)
    q0, k0, v0 = 2 * SSM_WIDTH, 2 * SSM_WIDTH + ATTN_WIDTH, 2 * SSM_WIDTH + ATTN_WIDTH + KV_WIDTH
    ssm_ref[...] = p[:, :q0].astype(BF16)
    qk_ref[...] = jnp.concatenate([p[:, q0:k0] * (HEAD_DIM ** -0.5), p[:, k0:v0]], axis=1).astype(BF16)
    vt_ref[...] = jnp.transpose(p[:, v0:v0 + KV_WIDTH]).astype(BF16)
    sgu_ref[...] = p[:, v0 + KV_WIDTH:].astype(BF16)


def _inproj(x_parts, g, w, tm):
    t = sum(x.shape[0] for x in x_parts)
    return pl.pallas_call(
        functools.partial(_inproj_kernel, n_parts=len(x_parts), steps_per_part=x_parts[0].shape[0] // tm),
        grid=(t // tm,),
        in_specs=_part_specs(x_parts, tm) + [
            pl.BlockSpec((1, D_MODEL), lambda i: (0, 0)),
            pl.BlockSpec((D_MODEL, IN_COLS), lambda i: (0, 0)),
        ],
        out_specs=[
            pl.BlockSpec((tm, 2 * SSM_WIDTH), lambda i: (i, 0)),
            pl.BlockSpec((tm, ATTN_WIDTH + KV_WIDTH), lambda i: (i, 0)),
            pl.BlockSpec((KV_WIDTH, tm), lambda i: (0, i)),
            pl.BlockSpec((tm, 2 * SGU_WIDTH), lambda i: (i, 0)),
        ],
        out_shape=[
            jax.ShapeDtypeStruct((t, 2 * SSM_WIDTH), BF16),
            jax.ShapeDtypeStruct((t, ATTN_WIDTH + KV_WIDTH), BF16),
            jax.ShapeDtypeStruct((KV_WIDTH, t), BF16),
            jax.ShapeDtypeStruct((t, 2 * SGU_WIDTH), BF16),
        ],
        compiler_params=_cparams("arbitrary"),
        name="inproj",
    )(*x_parts, g, w)


SCAN_COL_BLOCKS = 1


def _ssm_kernel(u_ref, b_ref, a_ref, c_ref, init_ref, *rest, tc, emit_y):
    if emit_y:
        y_ref, fin_ref, bu_scr, st_scr, tm_lo, tm_hi = rest
    else:
        fin_ref, bu_scr, st_scr, tm_lo, tm_hi = rest
    d = pl.program_id(0)
    c = pl.program_id(2)
    halves = ((tm_lo, slice(0, LANES)), (tm_hi, slice(LANES, 2 * LANES)))

    @pl.when(c == 0)
    def _():
        st_scr[...] = init_ref[0, 0]

    for s in range(N_SEG):
        u_s = u_ref[s].astype(F32)
        for scr, cols in halves:
            scr[pl.ds(s, tc, stride=N_SEG), :] = u_s[:, cols]
    u_tm = jnp.concatenate([tm_lo[...], tm_hi[...]], axis=1).astype(BF16)
    bu_scr[...] = jnp.dot(u_tm, b_ref[0], preferred_element_type=F32)
    w = SSM_COMPLEX // SCAN_COL_BLOCKS
    for cb in range(SCAN_COL_BLOCKS):
        re_cols = slice(cb * w, (cb + 1) * w)
        im_cols = slice(SSM_COMPLEX + cb * w, SSM_COMPLEX + (cb + 1) * w)
        a_re = jnp.broadcast_to(a_ref[0, 0:1, re_cols], (N_SEG, w))
        a_im = jnp.broadcast_to(a_ref[0, 1:2, re_cols], (N_SEG, w))

        def body(tt, carry, re_cols=re_cols, im_cols=im_cols, a_re=a_re, a_im=a_im):
            x_re, x_im = carry
            t = jnp.where(d == 0, tt, tc - 1 - tt)
            rows = pl.ds(pl.multiple_of(t * N_SEG, N_SEG), N_SEG)
            n_re = a_re * x_re - a_im * x_im + bu_scr[rows, re_cols]
            n_im = a_re * x_im + a_im * x_re + bu_scr[rows, im_cols]
            bu_scr[rows, re_cols] = n_re
            bu_scr[rows, im_cols] = n_im
            return n_re, n_im

        x_re, x_im = lax.fori_loop(0, tc, body, (st_scr[:, re_cols], st_scr[:, im_cols]), unroll=8)
        st_scr[:, re_cols] = x_re
        st_scr[:, im_cols] = x_im

    if emit_y:
        y = jnp.dot(bu_scr[...].astype(BF16), c_ref[...], preferred_element_type=F32)
        for scr, cols in halves:
            scr[...] = y[:, cols]
        for s in range(N_SEG):
            for scr, cols in halves:
                y_ref[0, s, :, cols] = scr[pl.ds(s, tc, stride=N_SEG), :]

    @pl.when(c == pl.num_programs(2) - 1)
    def _():
        fin_ref[0, 0] = st_scr[...]


def _ssm_pipe_kernel(u_ref, b_ref, a_ref, c_ref, init_ref, y_ref, fin_ref, bu0, bu1, bu2, st_scr, ut_lo, ut_hi,
                     yt_lo, yt_hi, *, tc, nc):
    d = pl.program_id(0)
    s = pl.program_id(2)
    bufs = (bu0, bu1, bu2)
    u_halves = ((ut_lo, slice(0, LANES)), (ut_hi, slice(LANES, 2 * LANES)))
    y_halves = ((yt_lo, slice(0, LANES)), (yt_hi, slice(LANES, 2 * LANES)))

    @pl.when(s == 0)
    def _():
        st_scr[...] = init_ref[0, 0]
        for buf in bufs:
            buf[...] = jnp.zeros_like(buf)

    def projections(b_buf, c_buf):
        for sg in range(N_SEG):
            u_s = u_ref[sg].astype(F32)
            for scr, cols in u_halves:
                scr[pl.ds(sg, tc, stride=N_SEG), :] = u_s[:, cols]
        u_tm = jnp.concatenate([ut_lo[...], ut_hi[...]], axis=1).astype(BF16)
        y = jnp.dot(c_buf[...].astype(BF16), c_ref[...], preferred_element_type=F32)
        b_buf[...] = jnp.dot(u_tm, b_ref[0], preferred_element_type=F32)
        for scr, cols in y_halves:
            scr[...] = y[:, cols]
        for sg in range(N_SEG):
            for scr, cols in y_halves:
                y_ref[0, sg, :, cols] = scr[pl.ds(sg, tc, stride=N_SEG), :]

    def recurrence(buf):
        w = SSM_COMPLEX // SCAN_COL_BLOCKS
        for cb in range(SCAN_COL_BLOCKS):
            re_cols = slice(cb * w, (cb + 1) * w)
            im_cols = slice(SSM_COMPLEX + cb * w, SSM_COMPLEX + (cb + 1) * w)
            a_re = jnp.broadcast_to(a_ref[0, 0:1, re_cols], (N_SEG, w))
            a_im = jnp.broadcast_to(a_ref[0, 1:2, re_cols], (N_SEG, w))

            def body(tt, carry, re_cols=re_cols, im_cols=im_cols, a_re=a_re, a_im=a_im):
                x_re, x_im = carry
                t = jnp.where(d == 0, tt, tc - 1 - tt)
                rows = pl.ds(pl.multiple_of(t * N_SEG, N_SEG), N_SEG)
                n_re = a_re * x_re - a_im * x_im + buf[rows, re_cols]
                n_im = a_re * x_im + a_im * x_re + buf[rows, im_cols]
                buf[rows, re_cols] = n_re
                buf[rows, im_cols] = n_im
                return n_re, n_im

            x_re, x_im = lax.fori_loop(0, tc, body, (st_scr[:, re_cols], st_scr[:, im_cols]), unroll=8)
            st_scr[:, re_cols] = x_re
            st_scr[:, im_cols] = x_im

    turn = lax.rem(s, 3)
    for k in range(3):
        @pl.when(turn == k)
        def _(k=k):
            projections(bufs[k], bufs[(k + 1) % 3])

    for k in range(3):
        @pl.when((turn == k) & (s >= 1) & (s <= nc))
        def _(k=k):
            recurrence(bufs[(k + 2) % 3])

    @pl.when(s == pl.num_programs(2) - 1)
    def _():
        fin_ref[0, 0] = st_scr[...]


def _ssm_scan_pipelined(ssm_proj, seg_len, n_groups, b_blk, a_rows, c_blk, init, tc):
    u = ssm_proj.reshape(-1, seg_len, 2 * SSM_WIDTH)
    nc = seg_len // tc

    def chunk(d, c):
        c = jnp.clip(c, 0, nc - 1)
        return jnp.where(d == 0, c, nc - 1 - c)

    rows = tc * N_SEG
    y, fin = pl.pallas_call(
        functools.partial(_ssm_pipe_kernel, tc=tc, nc=nc),
        grid=(2, n_groups, nc + 2),
        in_specs=[
            pl.BlockSpec((N_SEG, tc, SSM_WIDTH), lambda d, g, s: (g, chunk(d, s), 0)),
            pl.BlockSpec((1, SSM_WIDTH, SSM_REAL), lambda d, g, s: (d, 0, 0)),
            pl.BlockSpec((1, 2, SSM_COMPLEX), lambda d, g, s: (d, 0, 0)),
            pl.BlockSpec((SSM_REAL, SSM_WIDTH), lambda d, g, s: (0, 0)),
            pl.BlockSpec((1, 1, N_SEG, SSM_REAL), lambda d, g, s: (d, g, 0, 0)),
        ],
        out_specs=[
            pl.BlockSpec((1, N_SEG, tc, SSM_WIDTH), lambda d, g, s: (d, g, chunk(d, s - 2), 0)),
            pl.BlockSpec((1, 1, N_SEG, SSM_REAL), lambda d, g, s: (d, g, 0, 0)),
        ],
        out_shape=[
            jax.ShapeDtypeStruct((2, n_groups * N_SEG, seg_len, SSM_WIDTH), F32),
            jax.ShapeDtypeStruct((2, n_groups, N_SEG, SSM_REAL), F32),
        ],
        scratch_shapes=[pltpu.VMEM((rows, SSM_REAL), F32)] * 3 + [pltpu.VMEM((N_SEG, SSM_REAL), F32)]
        + [pltpu.VMEM((rows, LANES), F32)] * 4,
        compiler_params=_cparams("arbitrary", "arbitrary", "arbitrary"),
        name="ssm_scan_y",
    )(u, b_blk, a_rows, c_blk, init)
    return y.reshape(2, -1, SSM_WIDTH), fin


def _ssm_scan(ssm_proj, seg_len, first_group, n_groups, b_blk, a_rows, c_blk, init, tc, emit_y):
    u = ssm_proj.reshape(-1, seg_len, 2 * SSM_WIDTH)
    nc = seg_len // tc

    def chunk(d, c):
        return jnp.where(d == 0, c, nc - 1 - c)

    out_specs = [pl.BlockSpec((1, 1, N_SEG, SSM_REAL), lambda d, g, c: (d, g, 0, 0))]
    out_shape = [jax.ShapeDtypeStruct((2, n_groups, N_SEG, SSM_REAL), F32)]
    if emit_y:
        out_specs = [pl.BlockSpec((1, N_SEG, tc, SSM_WIDTH), lambda d, g, c: (d, g, chunk(d, c), 0))] + out_specs
        out_shape = [jax.ShapeDtypeStruct((2, n_groups * N_SEG, seg_len, SSM_WIDTH), F32)] + out_shape
    res = pl.pallas_call(
        functools.partial(_ssm_kernel, tc=tc, emit_y=emit_y),
        grid=(2, n_groups, nc),
        in_specs=[
            pl.BlockSpec((N_SEG, tc, SSM_WIDTH), lambda d, g, c: (first_group + g, chunk(d, c), 0)),
            pl.BlockSpec((1, SSM_WIDTH, SSM_REAL), lambda d, g, c: (d, 0, 0)),
            pl.BlockSpec((1, 2, SSM_COMPLEX), lambda d, g, c: (d, 0, 0)),
            pl.BlockSpec((SSM_REAL, SSM_WIDTH), lambda d, g, c: (0, 0)),
            pl.BlockSpec((1, 1, N_SEG, SSM_REAL), lambda d, g, c: (d, g, 0, 0)),
        ],
        out_specs=out_specs,
        out_shape=out_shape,
        scratch_shapes=[pltpu.VMEM((tc * N_SEG, SSM_REAL), F32), pltpu.VMEM((N_SEG, SSM_REAL), F32),
                        pltpu.VMEM((tc * N_SEG, LANES), F32), pltpu.VMEM((tc * N_SEG, LANES), F32)],
        compiler_params=_cparams("arbitrary", "arbitrary", "arbitrary"),
        name="ssm_scan_y" if emit_y else "ssm_scan_state",
    )(u, b_blk, a_rows, c_blk, init)
    if emit_y:
        return res[0].reshape(2, -1, SSM_WIDTH), res[1]
    return None, res[0]


def _ssm_params(lam_re, lam_im, log_step, b_re, b_im, c_re, c_im):
    lam = lax.complex(lam_re.astype(F32), lam_im.astype(F32))
    step = jnp.exp(log_step.astype(F32))[..., None]
    lam_bar = jnp.exp(lam * step)
    b = lax.complex(b_re.astype(F32), b_im.astype(F32))
    b_bar = ((lam_bar - 1.0) / lam)[..., None] * b
    eye = jnp.eye(SSM_GROUPS, dtype=F32)
    b_blk_re = jnp.einsum('dgph,gk->dghkp', jnp.real(b_bar), eye).reshape(2, SSM_WIDTH, SSM_COMPLEX)
    b_blk_im = jnp.einsum('dgph,gk->dghkp', jnp.imag(b_bar), eye).reshape(2, SSM_WIDTH, SSM_COMPLEX)
    b_blk = jnp.concatenate([b_blk_re, b_blk_im], axis=-1).astype(BF16)
    c_blk_re = jnp.einsum('ghp,gk->gpkh', c_re.astype(F32), eye).reshape(SSM_COMPLEX, SSM_WIDTH)
    c_blk_im = jnp.einsum('ghp,gk->gpkh', c_im.astype(F32), eye).reshape(SSM_COMPLEX, SSM_WIDTH)
    c_blk = jnp.concatenate([c_blk_re, -c_blk_im], axis=0).astype(BF16)
    a_rows = jnp.stack([jnp.real(lam_bar).reshape(2, SSM_COMPLEX),
                        jnp.imag(lam_bar).reshape(2, SSM_COMPLEX)], axis=1)
    return lam_bar.reshape(2, SSM_COMPLEX), b_blk, a_rows, c_blk


def _chain_init(fin, lam_pow):
    f = lax.complex(fin[..., :SSM_COMPLEX], fin[..., SSM_COMPLEX:])
    zero = jnp.zeros_like(f[0, 0])
    fwd = [zero]
    for s in range(1, N_SEG):
        fwd.append(lam_pow[0] * fwd[-1] + f[0, s - 1])
    bwd = [zero]
    for s in range(N_SEG - 2, -1, -1):
        bwd.append(lam_pow[1] * bwd[-1] + f[1, s + 1])
    init = jnp.stack([jnp.stack(fwd), jnp.stack(bwd[::-1])])
    return jnp.concatenate([jnp.real(init), jnp.imag(init)], axis=-1).astype(F32)


def _attn_sgu_kernel(flags_ref, q_ref, kp_ref, kc_ref, kn_ref, vp_ref, vc_ref, vn_ref, bias_ref, sink_ref,
                     sgu_ref, sgn_ref, sgw_ref, sgb_ref, attn_ref, sguo_ref, *, nbq):
    n = pl.program_id(0)
    no_prev = flags_ref[0, n * nbq] == 0
    no_next = flags_ref[1, n * nbq + nbq - 1] == 0

    def k_block(i):
        if i == 0:
            return kp_ref[...]
        if i == nbq + 1:
            return kn_ref[...]
        return kc_ref[(i - 1) * BLOCK:i * BLOCK, :]

    def vt_block(i):
        if i == 0:
            return vp_ref[...]
        if i == nbq + 1:
            return vn_ref[...]
        return vc_ref[:, (i - 1) * BLOCK:i * BLOCK]

    for b in range(nbq):
        rows = slice(b * BLOCK, (b + 1) * BLOCK)
        q = q_ref[rows, :]
        outs = []
        for j in range(N_KV_HEADS):
            qs = jnp.concatenate(
                [q[:, (j * Q_PER_KV + i) * HEAD_DIM:(j * Q_PER_KV + i + 1) * HEAD_DIM] for i in range(Q_PER_KV)],
                axis=0)
            kv = slice(j * HEAD_DIM, (j + 1) * HEAD_DIM)
            s = []
            for i in range(3):
                si = lax.dot_general(k_block(b + i)[:, kv], qs, (((1,), (1,)), ((), ())),
                                     preferred_element_type=F32)
                si = si + bias_ref[j, i * BLOCK:(i + 1) * BLOCK, :]
                if b == 0 and i == 0:
                    si = jnp.where(no_prev, NEG_INF, si)
                if b == nbq - 1 and i == 2:
                    si = jnp.where(no_next, NEG_INF, si)
                s.append(si)
            sink = sink_ref[j]
            m = jnp.max(jnp.maximum(jnp.maximum(s[0], s[1]), s[2]), axis=0, keepdims=True)
            m = jnp.maximum(m, sink)
            p = [jnp.exp(si - m) for si in s]
            denom = jnp.sum(p[0] + p[1] + p[2], axis=0, keepdims=True) + jnp.exp(sink - m)
            inv = 1.0 / denom
            o_t = sum(jnp.dot(vt_block(b + i)[kv, :], (p[i] * inv).astype(BF16), preferred_element_type=F32)
                      for i in range(3))
            outs += [jnp.transpose(o_t[:, i * BLOCK:(i + 1) * BLOCK]) for i in range(Q_PER_KV)]
        attn_ref[rows, :] = jnp.concatenate(outs, axis=1).astype(BF16)

        gu = jax.nn.gelu(sgu_ref[rows, :SGU_WIDTH].astype(F32))
        gv = _rms(jax.nn.gelu(sgu_ref[rows, SGU_WIDTH:].astype(F32)), sgn_ref[...]).astype(BF16)
        mixed = jnp.concatenate(
            [jnp.dot(sgw_ref[g], gv[:, g * SGU_GROUP_WIDTH:(g + 1) * SGU_GROUP_WIDTH], preferred_element_type=F32)
             for g in range(SGU_GROUPS)], axis=1) + sgb_ref[...]
        sguo_ref[rows, :] = (gu * mixed).astype(BF16)


def _attn_sgu(flags, qk, vt, sgu, bias, sink, sgn, sgw, sgb, nbq):
    t = qk.shape[0]
    nb = t // BLOCK
    tile = nbq * BLOCK
    kcol = ATTN_WIDTH // KV_WIDTH
    before = lambda n: jnp.maximum(n * nbq - 1, 0)
    after = lambda n: jnp.minimum((n + 1) * nbq, nb - 1)
    grid_spec = pltpu.PrefetchScalarGridSpec(
        num_scalar_prefetch=1,
        grid=(nb // nbq,),
        in_specs=[
            pl.BlockSpec((tile, ATTN_WIDTH), lambda n, f: (n, 0)),
            pl.BlockSpec((BLOCK, KV_WIDTH), lambda n, f: (before(n), kcol)),
            pl.BlockSpec((tile, KV_WIDTH), lambda n, f: (n, kcol)),
            pl.BlockSpec((BLOCK, KV_WIDTH), lambda n, f: (after(n), kcol)),
            pl.BlockSpec((KV_WIDTH, BLOCK), lambda n, f: (0, before(n))),
            pl.BlockSpec((KV_WIDTH, tile), lambda n, f: (0, n)),
            pl.BlockSpec((KV_WIDTH, BLOCK), lambda n, f: (0, after(n))),
            pl.BlockSpec((N_KV_HEADS, 3 * BLOCK, Q_PER_KV * BLOCK), lambda n, f: (0, 0, 0)),
            pl.BlockSpec((N_KV_HEADS, 1, Q_PER_KV * BLOCK), lambda n, f: (0, 0, 0)),
            pl.BlockSpec((tile, 2 * SGU_WIDTH), lambda n, f: (n, 0)),
            pl.BlockSpec((1, SGU_WIDTH), lambda n, f: (0, 0)),
            pl.BlockSpec((SGU_GROUPS, BLOCK, BLOCK), lambda n, f: (0, 0, 0)),
            pl.BlockSpec((BLOCK, SGU_WIDTH), lambda n, f: (0, 0)),
        ],
        out_specs=[
            pl.BlockSpec((tile, ATTN_WIDTH), lambda n, f: (n, 0)),
            pl.BlockSpec((tile, SGU_WIDTH), lambda n, f: (n, 0)),
        ],
    )
    return pl.pallas_call(
        functools.partial(_attn_sgu_kernel, nbq=nbq),
        grid_spec=grid_spec,
        out_shape=[jax.ShapeDtypeStruct((t, ATTN_WIDTH), BF16), jax.ShapeDtypeStruct((t, SGU_WIDTH), BF16)],
        compiler_params=_cparams("arbitrary"),
        name="attn_sgu",
    )(flags, qk, qk, qk, qk, vt, vt, vt, bias, sink, sgu, sgn, sgw, sgb)


def _t5_bucket(rel):
    half = N_BUCKETS // 2
    max_exact = half // 2
    ret = jnp.where(rel > 0, half, 0)
    n = jnp.abs(rel)
    nf = jnp.maximum(n, 1).astype(F32)
    large = max_exact + (jnp.log(nf / max_exact) / math.log(MAX_DISTANCE / max_exact)
                         * (half - max_exact)).astype(jnp.int32)
    large = jnp.minimum(large, half - 1)
    return ret + jnp.where(n < max_exact, n, large)


def _band_bias(rel_table):
    q_pos = jnp.arange(BLOCK)[:, None]
    k_pos = jnp.arange(3 * BLOCK)[None, :] - BLOCK
    rel = k_pos - q_pos
    hit = _t5_bucket(rel)[..., None] == jnp.arange(N_BUCKETS)
    bias = jnp.sum(jnp.where(hit[..., None], rel_table.astype(F32), 0.0), axis=2)
    bias = jnp.where((jnp.abs(rel) <= BLOCK)[..., None], bias, NEG_INF)
    bias = bias.reshape(BLOCK, 3 * BLOCK, N_KV_HEADS, Q_PER_KV)
    return bias.transpose(2, 1, 3, 0).reshape(N_KV_HEADS, 3 * BLOCK, Q_PER_KV * BLOCK)


def _outproj_router_kernel(*refs, tm, n_parts, steps_per_part):
    x_refs = refs[:n_parts]
    (ssm_ref, yf_ref, yb_ref, attn_ref, sguo_ref, d_ref, gout_ref, wout_ref, gffn_ref, wrh_ref, wrl_ref, br_ref,
     x1_ref, h_ref, idx_ref, rank_ref, wgt_ref, cnt_ref) = refs[n_parts:]
    u = ssm_ref[:, :SSM_WIDTH].astype(F32)
    gate = ssm_ref[:, SSM_WIDTH:].astype(F32)
    ya = yf_ref[...] + yb_ref[...] + d_ref[...] * u
    ya = jax.nn.gelu(ya) * jax.nn.sigmoid(gate)
    g = gout_ref[...]
    mixed = jnp.concatenate([
        _rms(ya, g[:, :SSM_WIDTH]),
        _rms(attn_ref[...].astype(F32), g[:, SSM_WIDTH:SSM_WIDTH + ATTN_WIDTH]),
        _rms(sguo_ref[...].astype(F32), g[:, SSM_WIDTH + ATTN_WIDTH:]),
    ], axis=1).astype(BF16)
    x1 = _pick_part(x_refs, steps_per_part) + jnp.dot(mixed, wout_ref[...], preferred_element_type=F32)
    x1_ref[...] = x1
    h = _rms(x1, gffn_ref[...])
    _store_token_tiles(h_ref, 0, h, tm)

    h_hi = h.astype(BF16)
    h_lo = (h - h_hi.astype(F32)).astype(BF16)
    nt_dot = lambda a, b: lax.dot_general(a, b, (((1,), (1,)), ((), ())), preferred_element_type=F32)
    lg = nt_dot(wrh_ref[...], h_hi) + nt_dot(wrl_ref[...], h_hi) + nt_dot(wrh_ref[...], h_lo) + br_ref[...]
    row = lax.broadcasted_iota(jnp.int32, (N_EXPERTS, tm), 0)
    vals, idxs, sels = [], [], []
    for _ in range(TOP_K):
        m = jnp.max(lg, axis=0, keepdims=True)
        idx = jnp.min(jnp.where(lg == m, row, N_EXPERTS), axis=0, keepdims=True)
        sel = row == idx
        lg = jnp.where(sel, -jnp.inf, lg)
        vals.append(m)
        idxs.append(idx)
        sels.append(sel)
    e = [jnp.exp(v - vals[0]) for v in vals]
    tot = e[0] + e[1] + e[2] + e[3]
    wgt_ref[...] = jnp.concatenate([ek / tot for ek in e], axis=0)
    idx_ref[...] = jnp.concatenate(idxs, axis=0)

    picked = (sels[0] | sels[1] | sels[2] | sels[3])
    pm = jnp.where(picked, 1.0, 0.0)
    r_io = lax.broadcasted_iota(jnp.int32, (tm, tm), 0)
    c_io = lax.broadcasted_iota(jnp.int32, (tm, tm), 1)
    earlier = jnp.where(r_io < c_io, 1.0, 0.0).astype(BF16)
    before = jnp.dot(pm.astype(BF16), earlier, preferred_element_type=F32)
    rank_ref[...] = jnp.concatenate(
        [jnp.sum(jnp.where(s, before, 0.0), axis=0, keepdims=True) for s in sels], axis=0).astype(jnp.int32)
    cnt_ref[0] = jnp.sum(pm, axis=1, keepdims=True).astype(jnp.int32)


def _outproj_router(x_parts, ssm, y_dirs, attn, sguo, d_skip, g_out, w_out, g_ffn, w_r, b_r, tm):
    t = ssm.shape[0]
    row = lambda w: pl.BlockSpec((tm, w), lambda i: (i, 0))
    full = lambda a, b: pl.BlockSpec((a, b), lambda i: (0, 0))
    direction = lambda d: pl.BlockSpec((None, tm, SSM_WIDTH), lambda i: (d, i, 0))
    per_choice = pl.BlockSpec((TOP_K, tm), lambda i: (0, i))
    w_rt = w_r.astype(F32).T
    w_hi = w_rt.astype(BF16)
    w_lo = (w_rt - w_hi.astype(F32)).astype(BF16)
    return pl.pallas_call(
        functools.partial(_outproj_router_kernel, tm=tm, n_parts=len(x_parts),
                          steps_per_part=x_parts[0].shape[0] // tm),
        grid=(t // tm,),
        in_specs=_part_specs(x_parts, tm) + [
            row(2 * SSM_WIDTH), direction(0), direction(1), row(ATTN_WIDTH), row(SGU_WIDTH),
            full(1, SSM_WIDTH), full(1, D_MODEL), full(D_MODEL, D_MODEL), full(1, D_MODEL),
            full(N_EXPERTS, D_MODEL), full(N_EXPERTS, D_MODEL), full(N_EXPERTS, 1),
        ],
        out_specs=[row(D_MODEL), pl.BlockSpec((tm * TOKEN_TILE_ROWS, LANES), lambda i: (i, 0)),
                   per_choice, per_choice, per_choice, pl.BlockSpec((1, N_EXPERTS, 1), lambda i: (i, 0, 0))],
        out_shape=[
            jax.ShapeDtypeStruct((t, D_MODEL), F32),
            jax.ShapeDtypeStruct((t * TOKEN_TILE_ROWS, LANES), F32),
            jax.ShapeDtypeStruct((TOP_K, t), jnp.int32),
            jax.ShapeDtypeStruct((TOP_K, t), jnp.int32),
            jax.ShapeDtypeStruct((TOP_K, t), F32),
            jax.ShapeDtypeStruct((t // tm, N_EXPERTS, 1), jnp.int32),
        ],
        compiler_params=_cparams("arbitrary"),
        name="outproj_router",
    )(*x_parts, ssm, y_dirs, y_dirs, attn, sguo, d_skip, g_out, w_out, g_ffn, w_hi, w_lo,
      b_r.astype(F32).reshape(N_EXPERTS, 1))


ISSUE_UNROLL = 8
FETCH_ROWS = 16


def _for_rows(n, fn):
    def body(c, carry):
        for u in range(ISSUE_UNROLL):
            fn(c * ISSUE_UNROLL + u, u % 2)
        return carry

    lax.fori_loop(0, n // ISSUE_UNROLL, body, 0)


def _tile_rows(ref, first_tile, n_tiles=1):
    start = first_tile * TOKEN_TILE_ROWS
    if not isinstance(start, int):
        start = pl.multiple_of(start, TOKEN_TILE_ROWS)
    return ref.at[pl.ds(start, n_tiles * TOKEN_TILE_ROWS)]


EXPERT_COL_CHUNK = 1024


GATHER_SLOTS = 3


def _experts_kernel(ltile_ref, otile_ref, exp_ref, lo_ref, hi_ref, tok_ref, tok1_ref, tok2_ref, wrow_ref, h_hbm, wgu_ref,
                    bgu_ref, wd_ref, bd_ref, out_ref, x_buf, y_buf, wgu_bf, wd_bf, gsem, *, tm):
    w = pl.program_id(0)
    nw = pl.num_programs(0)
    slot = lax.rem(w, GATHER_SLOTS)
    ahead = lax.rem(w + GATHER_SLOTS - 1, GATHER_SLOTS)
    prev = jnp.maximum(w - 1, 0)

    def gather_row(list_ref, to_slot, r, queue):
        pltpu.make_async_copy(_tile_rows(h_hbm, list_ref[0, 0, r]), _tile_rows(x_buf, to_slot * tm + r),
                              gsem.at[to_slot]).start(priority=queue)

    def wait_gather(of_slot):
        pltpu.make_async_copy(_tile_rows(h_hbm, 0, tm), _tile_rows(x_buf, of_slot * tm, tm), gsem.at[of_slot]).wait()

    @pl.when(w == 0)
    def _():
        _for_rows(tm, lambda r, queue: gather_row(tok_ref, 0, r, queue))
        _for_rows(tm, lambda r, queue: gather_row(tok1_ref, 1, r, queue))

    @pl.when((w == 0) | (exp_ref[w] != exp_ref[prev]))
    def _():
        wgu_bf[...] = wgu_ref[0].astype(BF16)
        wd_bf[...] = wd_ref[0].astype(BF16)

    wait_gather(slot)

    n_pieces = (D_EXPERT + D_MODEL) // EXPERT_COL_CHUNK
    per_piece = tm // n_pieces
    piece = 0

    def issue_next_gather():
        nonlocal piece
        for r in range(piece * per_piece, (piece + 1) * per_piece):
            gather_row(tok2_ref, ahead, r, r % 2)
        piece += 1

    w_rows = jnp.transpose(jnp.broadcast_to(wrow_ref[0], (LANES, tm)))
    w_rows = jnp.concatenate([w_rows] * (EXPERT_COL_CHUNK // LANES), axis=1)

    base = pl.multiple_of(slot * (tm * TOKEN_TILE_ROWS), tm * TOKEN_TILE_ROWS)
    x = _load_token_tiles(x_buf, base, tm).astype(BF16)
    acts = []
    for c in range(D_EXPERT // EXPERT_COL_CHUNK):
        issue_next_gather()
        g_cols = slice(c * EXPERT_COL_CHUNK, (c + 1) * EXPERT_COL_CHUNK)
        u_cols = slice(D_EXPERT + c * EXPERT_COL_CHUNK, D_EXPERT + (c + 1) * EXPERT_COL_CHUNK)
        gate = jnp.dot(x, wgu_bf[:, g_cols], preferred_element_type=F32) + bgu_ref[0, :, g_cols]
        up = jnp.dot(x, wgu_bf[:, u_cols], preferred_element_type=F32) + bgu_ref[0, :, u_cols]
        gate = jnp.minimum(gate, SWIGLU_LIMIT)
        up = jnp.clip(up, -SWIGLU_LIMIT, SWIGLU_LIMIT)
        acts.append(((up + 1.0) * gate * jax.nn.sigmoid(SWIGLU_ALPHA * gate)).astype(BF16))
    act = jnp.concatenate(acts, axis=1)
    tiles_per_chunk = EXPERT_COL_CHUNK // LANES
    for c in range(D_MODEL // EXPERT_COL_CHUNK):
        issue_next_gather()
        cols = slice(c * EXPERT_COL_CHUNK, (c + 1) * EXPERT_COL_CHUNK)
        y = (jnp.dot(act, wd_bf[:, cols], preferred_element_type=F32) + bd_ref[0, :, cols]) * w_rows
        for j in range(tiles_per_chunk):
            y_buf[pl.ds(c * tiles_per_chunk + j, tm, stride=TOKEN_TILE_ROWS), :] = y[:, j * LANES:(j + 1) * LANES]

    first = (w == 0) | (otile_ref[w] != otile_ref[prev])

    @pl.when(first)
    def _():
        out_ref[...] = y_buf[...]

    @pl.when(jnp.logical_not(first))
    def _():
        row = lax.shift_right_logical(lax.broadcasted_iota(jnp.int32, y_buf.shape, 0),
                                      TOKEN_TILE_ROWS.bit_length() - 1)
        mine = (row >= lo_ref[w]) & (row < hi_ref[w])
        out_ref[...] = jnp.where(mine, y_buf[...], out_ref[...])

    @pl.when(w == nw - 1)
    def _():
        for k in range(1, GATHER_SLOTS):
            wait_gather(lax.rem(w + k, GATHER_SLOTS))


def _experts(items, tok_sorted, w_sorted, h_tiles, wgu, bgu, wd, bd, layer, tm):
    nw = items[0].shape[0]
    n_rows = tok_sorted.shape[0]
    nt = n_rows // tm
    lists = lambda a: a.reshape(nt, 1, tm)
    later = lambda k: (lambda i, lt, ot, ex, lo, hi: (lt[jnp.minimum(i + k, nw - 1)], 0, 0))
    by_expert = lambda i, lt, ot, ex, lo, hi: (layer, ex[i], 0, 0)
    grid_spec = pltpu.PrefetchScalarGridSpec(
        num_scalar_prefetch=5,
        grid=(nw,),
        in_specs=[
            pl.BlockSpec((1, 1, tm), later(0), memory_space=pltpu.SMEM),
            pl.BlockSpec((1, 1, tm), later(1), memory_space=pltpu.SMEM),
            pl.BlockSpec((1, 1, tm), later(2), memory_space=pltpu.SMEM),
            pl.BlockSpec((1, 1, tm), later(0)),
            pl.BlockSpec(memory_space=pl.ANY),
            pl.BlockSpec((None, 1, D_MODEL, 2 * D_EXPERT), by_expert),
            pl.BlockSpec((None, 1, 1, 2 * D_EXPERT), by_expert),
            pl.BlockSpec((None, 1, D_EXPERT, D_MODEL), by_expert),
            pl.BlockSpec((None, 1, 1, D_MODEL), by_expert),
        ],
        out_specs=pl.BlockSpec((tm * TOKEN_TILE_ROWS, LANES), lambda i, lt, ot, ex, lo, hi: (ot[i], 0)),
        scratch_shapes=[
            pltpu.VMEM((GATHER_SLOTS * tm * TOKEN_TILE_ROWS, LANES), F32),
            pltpu.VMEM((tm * TOKEN_TILE_ROWS, LANES), F32),
            pltpu.VMEM((D_MODEL, 2 * D_EXPERT), BF16),
            pltpu.VMEM((D_EXPERT, D_MODEL), BF16),
            pltpu.SemaphoreType.DMA((GATHER_SLOTS,)),
        ],
    )
    return pl.pallas_call(
        functools.partial(_experts_kernel, tm=tm),
        grid_spec=grid_spec,
        out_shape=jax.ShapeDtypeStruct(((n_rows + FETCH_ROWS) * TOKEN_TILE_ROWS, LANES), F32),
        compiler_params=_cparams("arbitrary"),
        name="experts",
    )(*items, lists(tok_sorted), lists(tok_sorted), lists(tok_sorted), lists(w_sorted), h_tiles, wgu, bgu, wd, bd)


def _combine_kernel(src_ref, cst_ref, nch_ref, y_hbm, x1_ref, qpos_ref, g_ref, *rest, tm, nq, final, n_out):
    out_refs, (yc_buf, sem) = rest[:n_out], rest[n_out:]
    i = pl.program_id(0)
    slot = lax.rem(i, 2)

    def chunk_copy(src_row, dst_row, at_slot):
        return pltpu.make_async_copy(_tile_rows(y_hbm, src_row, FETCH_ROWS),
                                     _tile_rows(yc_buf, at_slot * nq + dst_row, FETCH_ROWS), sem.at[at_slot])

    def fetch_tile(tile, to_slot):
        for e in range(N_EXPERTS):
            src0 = src_ref[tile * N_EXPERTS + e]
            dst0 = cst_ref[tile * N_EXPERTS + e]

            def fetch(j, carry, src0=src0, dst0=dst0):
                chunk_copy(src0 + j * FETCH_ROWS, dst0 + j * FETCH_ROWS, to_slot).start()
                return carry

            lax.fori_loop(0, nch_ref[tile * N_EXPERTS + e], fetch, 0)

    @pl.when(i == 0)
    def _():
        yc_buf[...] = jnp.zeros_like(yc_buf)
        fetch_tile(0, 0)

    @pl.when(i + 1 < pl.num_programs(0))
    def _():
        fetch_tile(i + 1, 1 - slot)

    total = nch_ref[i * N_EXPERTS]
    for e in range(1, N_EXPERTS):
        total = total + nch_ref[i * N_EXPERTS + e]

    def wait(j, carry):
        chunk_copy(0, 0, slot).wait()
        return carry

    lax.fori_loop(0, total, wait, 0)

    q = lax.broadcasted_iota(jnp.int32, (tm, nq), 1)
    qpos = qpos_ref[...]
    hit = q == qpos[:, 0:1]
    for k in range(1, TOP_K):
        hit = hit | (q == qpos[:, k:k + 1])
    p = jnp.where(hit, 1.0, 0.0).astype(BF16)
    base = pl.multiple_of(slot * (nq * TOKEN_TILE_ROWS), nq * TOKEN_TILE_ROWS)
    yc = _load_token_tiles(yc_buf, base, nq).astype(BF16)
    x2 = x1_ref[...] + jnp.dot(p, yc, preferred_element_type=F32)
    if final:
        x2 = _rms(x2, g_ref[...])
    steps_per_out = pl.num_programs(0) // n_out
    for j, out_ref in enumerate(out_refs):
        @pl.when((i >= j * steps_per_out) & (i < (j + 1) * steps_per_out))
        def _(out_ref=out_ref):
            out_ref[...] = x2


def _combine(src0, cstart, nchunk, y_sorted, x1, qpos, g_final, tm, final, n_out):
    t = x1.shape[0]
    steps_per_out = t // tm // n_out
    nq = -(-(tm * TOP_K + N_EXPERTS * (FETCH_ROWS - 1)) // LANES) * LANES
    grid_spec = pltpu.PrefetchScalarGridSpec(
        num_scalar_prefetch=3,
        grid=(t // tm,),
        in_specs=[
            pl.BlockSpec(memory_space=pl.ANY),
            pl.BlockSpec((tm, D_MODEL), lambda i, a, b, c: (i, 0)),
            pl.BlockSpec((tm, TOP_K), lambda i, a, b, c: (i, 0)),
            pl.BlockSpec((1, D_MODEL), lambda i, a, b, c: (0, 0)),
        ],
        out_specs=[pl.BlockSpec((tm, D_MODEL), (lambda j: lambda i, a, b, c: (
            jnp.clip(i - j * steps_per_out, 0, steps_per_out - 1), 0))(j)) for j in range(n_out)],
        scratch_shapes=[pltpu.VMEM((2 * nq * TOKEN_TILE_ROWS, LANES), F32), pltpu.SemaphoreType.DMA((2,))],
    )
    return pl.pallas_call(
        functools.partial(_combine_kernel, tm=tm, nq=nq, final=final, n_out=n_out),
        grid_spec=grid_spec,
        out_shape=[jax.ShapeDtypeStruct((t // n_out, D_MODEL), F32)] * n_out,
        compiler_params=_cparams("arbitrary"),
        name="combine",
    )(src0.reshape(-1), cstart.reshape(-1), nchunk.reshape(-1), y_sorted, x1, qpos, g_final)


def _routing_lists(idx, lrank, wgt, tcnt, tm_tok, tm):
    t = idx.shape[1]
    n_rows = t * TOP_K
    nt = n_rows // tm
    cnt = jnp.sum(tcnt, axis=0)
    group_end = jnp.cumsum(cnt)
    group_start = group_end - cnt
    src0 = group_start[None, :] + jnp.cumsum(tcnt, axis=0) - tcnt
    aligned = -(-tcnt // FETCH_ROWS) * FETCH_ROWS
    cstart = jnp.cumsum(aligned, axis=1) - aligned
    nchunk = aligned // FETCH_ROWS
    experts = jnp.arange(N_EXPERTS, dtype=jnp.int32)
    onehot = idx.reshape(TOP_K, -1, tm_tok)[..., None] == experts
    pick = lambda table: jnp.sum(jnp.where(onehot, table[None, :, None, :], 0), axis=-1).reshape(TOP_K, t)
    pos = lrank + pick(src0)
    qpos = (lrank + pick(cstart)).T
    flat = jnp.arange(n_rows, dtype=jnp.int32)
    _, flat_sorted, w_sorted = lax.sort((pos.reshape(-1).astype(jnp.int32), flat, wgt.reshape(-1)), num_keys=1)
    tok_sorted = flat_sorted % t
    bounds = jnp.sort(jnp.concatenate([jnp.arange(nt, dtype=jnp.int32) * tm, group_start.astype(jnp.int32)]))
    nxt = jnp.concatenate([bounds[1:], jnp.array([n_rows], jnp.int32)])
    tile = jnp.minimum(bounds // tm, nt - 1)
    lo = bounds - tile * tm
    hi = jnp.where(nxt > bounds, nxt - tile * tm, lo)
    expert = jnp.minimum(jnp.sum(group_end[None, :] <= bounds[:, None], axis=-1), N_EXPERTS - 1)
    last = lambda a, v: jnp.concatenate([a, jnp.array([v], a.dtype)])
    items = (last(tile, nt - 1), last(tile, nt), jnp.concatenate([expert, expert[-1:]]), last(lo, 0), last(hi, 0))
    items = tuple(a.astype(jnp.int32) for a in items)
    to_i32 = lambda a: a.astype(jnp.int32)
    return items, tok_sorted, w_sorted, to_i32(src0), to_i32(cstart), to_i32(nchunk), to_i32(qpos)


def _block_flags(n_prompt_seq, seg_len, n_tokens):
    nb = n_tokens // BLOCK
    per_seg = seg_len // BLOCK
    b = jnp.arange(nb)
    prompt_blocks = n_prompt_seq * per_seg
    in_prompt = b < prompt_blocks
    first = jnp.where(in_prompt, b % per_seg == 0, b == prompt_blocks)
    last = jnp.where(in_prompt, b % per_seg == per_seg - 1, b == nb - 1)
    return jnp.stack([~first, ~last]).astype(jnp.int32)


def _trunk(x_parts, n_prompt_seq, seg_len, p, tm, tm_moe, tc, nbq):
    n_parts = len(x_parts)
    x = list(x_parts)
    t = sum(a.shape[0] for a in x)
    depth = p['w_in'].shape[0]
    flags = _block_flags(n_prompt_seq, seg_len, t)
    bias = _band_bias(p['rel_bias'])
    zero_init = jnp.zeros((2, 1, N_SEG, SSM_REAL), F32)
    for layer in range(depth):
        ssm, qk, vt, sgu = _inproj(x, p['norm_mix'][layer][None], p['w_in'][layer].astype(BF16), tm)

        lam_bar, b_blk, a_rows, c_blk = _ssm_params(
            p['ssm_lam_re'][layer], p['ssm_lam_im'][layer], p['ssm_log_step'][layer], p['ssm_b_re'][layer],
            p['ssm_b_im'][layer], p['ssm_c_re'][layer], p['ssm_c_im'][layer])
        _, fin = _ssm_scan(ssm, seg_len, 1, 1, b_blk, a_rows, c_blk, zero_init, tc, False)
        init = jnp.concatenate([zero_init, _chain_init(fin[:, 0], lam_bar ** seg_len)[:, None]], axis=1)
        y, _ = _ssm_scan_pipelined(ssm, seg_len, 2, b_blk, a_rows, c_blk, init, 2 * tc)

        sink = jnp.repeat(p['attn_sink'][layer].astype(F32), BLOCK).reshape(N_KV_HEADS, 1, Q_PER_KV * BLOCK)
        sgb = jnp.repeat(p['sgu_b'][layer].astype(F32).T, SGU_GROUP_WIDTH, axis=1)
        attn, sguo = _attn_sgu(flags, qk, vt, sgu, bias, sink, p['sgu_norm'][layer][None].astype(F32),
                               p['sgu_w'][layer].astype(BF16), sgb, nbq)

        x1, h, idx, rank, wgt, cnt = _outproj_router(
            x, ssm, y, attn, sguo, p['ssm_d'][layer][None].astype(F32), p['out_norm'][layer][None],
            p['w_out'][layer].astype(BF16), p['norm_ffn'][layer][None], p['w_router'][layer].astype(F32),
            p['b_router'][layer][None].astype(F32), tm)

        items, tok_sorted, w_sorted, src0, cstart, nchunk, qpos = _routing_lists(
            idx, rank, wgt, cnt[:, :, 0], tm, tm_moe)
        y_sorted = _experts(items, tok_sorted, w_sorted, h, p['w_gate_up'].astype(F32),
                            p['b_gate_up'][:, :, None].astype(F32), p['w_down'].astype(F32),
                            p['b_down'][:, :, None].astype(F32), layer, tm_moe)
        last = layer == depth - 1
        x = _combine(src0, cstart, nchunk, y_sorted, x1, qpos, p['final_norm'][None].astype(F32), tm,
                     last, n_parts if last else 1)
    return x


def _run(x_prompt, x_sample, p, tm=512, tm_moe=512, tc=64, nbq=4):
    bsz, seg_len, _ = x_prompt.shape
    assert bsz == N_SEG and x_sample.shape[0] == 1 and x_sample.shape[1] == N_SEG * seg_len
    assert seg_len % (nbq * BLOCK) == 0 and seg_len % tc == 0 and seg_len % tm == 0
    parts = [x_prompt.reshape(-1, D_MODEL).astype(F32), x_sample.reshape(-1, D_MODEL).astype(F32)]
    y_prompt, y_sample = _trunk(parts, bsz, seg_len, p, tm, tm_moe, tc, nbq)
    return y_prompt.reshape(x_prompt.shape), y_sample.reshape(x_sample.shape)


def kernel(x_prompt, x_sample, norm_mix, w_in, ssm_lam_re, ssm_lam_im, ssm_log_step, ssm_b_re, ssm_b_im, ssm_c_re, ssm_c_im, ssm_d, attn_sink, rel_bias, sgu_norm, sgu_w, sgu_b, out_norm, w_out, norm_ffn, w_router, b_router, w_gate_up, b_gate_up, w_down, b_down, final_norm):
    p = dict(norm_mix=norm_mix, w_in=w_in, ssm_lam_re=ssm_lam_re, ssm_lam_im=ssm_lam_im, ssm_log_step=ssm_log_step,
             ssm_b_re=ssm_b_re, ssm_b_im=ssm_b_im, ssm_c_re=ssm_c_re, ssm_c_im=ssm_c_im, ssm_d=ssm_d,
             attn_sink=attn_sink, rel_bias=rel_bias, sgu_norm=sgu_norm, sgu_w=sgu_w, sgu_b=sgu_b, out_norm=out_norm,
             w_out=w_out, norm_ffn=norm_ffn, w_router=w_router, b_router=b_router, w_gate_up=w_gate_up,
             b_gate_up=b_gate_up, w_down=w_down, b_down=b_down, final_norm=final_norm)
    return _run(x_prompt, x_sample, p)
```

```python
import functools
import math

import jax
import jax.numpy as jnp
from jax import lax
from jax.experimental import pallas as pl
from jax.experimental.pallas import tpu as pltpu

D_MODEL = 1024
SSM_WIDTH = 256
SSM_GROUP = 16
SSM_GROUPS = 16
SSM_STATE = 64
SSM_COMPLEX = SSM_GROUPS * SSM_STATE
SSM_REAL = 2 * SSM_COMPLEX
ATTN_WIDTH = 512
HEAD_DIM = 64
N_Q_HEADS = 8
N_KV_HEADS = 2
Q_PER_KV = 4
KV_WIDTH = 128
BLOCK = 128
N_BUCKETS = 32
MAX_DISTANCE = 128
SGU_WIDTH = 256
SGU_GROUPS = 4
SGU_GROUP_WIDTH = 64
IN_COLS = 1792
QKV_COLS = ATTN_WIDTH + 2 * KV_WIDTH
N_EXPERTS = 32
TOP_K = 4
D_EXPERT = 1024
SWIGLU_LIMIT = 7.0
SWIGLU_ALPHA = 1.702
EPS = 1e-6
NEG_INF = -1e30

N_SEG = 8
V7X_VMEM_LIMIT = 56 * 1024 * 1024

F32 = jnp.float32
BF16 = jnp.bfloat16


def _rms(xf, g):
    return xf * lax.rsqrt(jnp.mean(xf * xf, axis=-1, keepdims=True) + EPS) * g


LANES = 128
TOKEN_TILE_ROWS = D_MODEL // LANES


def _load_token_tiles(ref, base, n):
    return jnp.concatenate(
        [ref[pl.ds(base + j, n, stride=TOKEN_TILE_ROWS), :] for j in range(TOKEN_TILE_ROWS)], axis=1)


def _store_token_tiles(ref, base, val, n):
    for j in range(TOKEN_TILE_ROWS):
        ref[pl.ds(base + j, n, stride=TOKEN_TILE_ROWS), :] = val[:, j * LANES:(j + 1) * LANES]


def _cparams(*sem):
    return pltpu.CompilerParams(dimension_semantics=sem, vmem_limit_bytes=V7X_VMEM_LIMIT)


def _part_specs(parts, tm):
    per = parts[0].shape[0] // tm
    return [pl.BlockSpec((tm, D_MODEL), (lambda j: lambda i, *_: (jnp.clip(i - j * per, 0, per - 1), 0))(j))
            for j in range(len(parts))]


def _pick_part(x_refs, steps_per_part):
    i = pl.program_id(0)
    x = x_refs[0][...]
    for j in range(1, len(x_refs)):
        x = jnp.where(i >= j * steps_per_part, x_refs[j][...], x)
    return x


def _inproj_kernel(*refs, n_parts, steps_per_part):
    x_refs, (g_ref, w_ref, ssm_ref, qk_ref, vt_ref, sgu_ref) = refs[:n_parts], refs[n_parts:]
    h = _rms(_pick_part(x_refs, steps_per_part), g_ref[...]).astype(BF16)
    p = jnp.dot(h, w_ref[...], preferred_element_type=F32)
    q0, k0, v0 = 2 * SSM_WIDTH, 2 * SSM_WIDTH + ATTN_WIDTH, 2 * SSM_WIDTH + ATTN_WIDTH + KV_WIDTH
    ssm_ref[...] = p[:, :q0].astype(BF16)
    qk_ref[...] = jnp.concatenate([p[:, q0:k0] * (HEAD_DIM ** -0.5), p[:, k0:v0]], axis=1).astype(BF16)
    vt_ref[...] = jnp.transpose(p[:, v0:v0 + KV_WIDTH]).astype(BF16)
    sgu_ref[...] = p[:, v0 + KV_WIDTH:].astype(BF16)


def _inproj(x_parts, g, w, tm):
    t = sum(x.shape[0] for x in x_parts)
    return pl.pallas_call(
        functools.partial(_inproj_kernel, n_parts=len(x_parts), steps_per_part=x_parts[0].shape[0] // tm),
        grid=(t // tm,),
        in_specs=_part_specs(x_parts, tm) + [
            pl.BlockSpec((1, D_MODEL), lambda i: (0, 0)),
            pl.BlockSpec((D_MODEL, IN_COLS), lambda i: (0, 0)),
        ],
        out_specs=[
            pl.BlockSpec((tm, 2 * SSM_WIDTH), lambda i: (i, 0)),
            pl.BlockSpec((tm, ATTN_WIDTH + KV_WIDTH), lambda i: (i, 0)),
            pl.BlockSpec((KV_WIDTH, tm), lambda i: (0, i)),
            pl.BlockSpec((tm, 2 * SGU_WIDTH), lambda i: (i, 0)),
        ],
        out_shape=[
            jax.ShapeDtypeStruct((t, 2 * SSM_WIDTH), BF16),
            jax.ShapeDtypeStruct((t, ATTN_WIDTH + KV_WIDTH), BF16),
            jax.ShapeDtypeStruct((KV_WIDTH, t), BF16),
            jax.ShapeDtypeStruct((t, 2 * SGU_WIDTH), BF16),
        ],
        compiler_params=_cparams("arbitrary"),
        name="inproj",
    )(*x_parts, g, w)


SCAN_COL_BLOCKS = 1


def _ssm_kernel(u_ref, b_ref, a_ref, c_ref, init_ref, *rest, tc, emit_y):
    if emit_y:
        y_ref, fin_ref, bu_scr, st_scr, tm_lo, tm_hi = rest
    else:
        fin_ref, bu_scr, st_scr, tm_lo, tm_hi = rest
    d = pl.program_id(0)
    c = pl.program_id(2)
    halves = ((tm_lo, slice(0, LANES)), (tm_hi, slice(LANES, 2 * LANES)))

    @pl.when(c == 0)
    def _():
        st_scr[...] = init_ref[0, 0]

    for s in range(N_SEG):
        u_s = u_ref[s].astype(F32)
        for scr, cols in halves:
            scr[pl.ds(s, tc, stride=N_SEG), :] = u_s[:, cols]
    u_tm = jnp.concatenate([tm_lo[...], tm_hi[...]], axis=1).astype(BF16)
    bu_scr[...] = jnp.dot(u_tm, b_ref[0], preferred_element_type=F32)
    w = SSM_COMPLEX // SCAN_COL_BLOCKS
    for cb in range(SCAN_COL_BLOCKS):
        re_cols = slice(cb * w, (cb + 1) * w)
        im_cols = slice(SSM_COMPLEX + cb * w, SSM_COMPLEX + (cb + 1) * w)
        a_re = jnp.broadcast_to(a_ref[0, 0:1, re_cols], (N_SEG, w))
        a_im = jnp.broadcast_to(a_ref[0, 1:2, re_cols], (N_SEG, w))

        def body(tt, carry, re_cols=re_cols, im_cols=im_cols, a_re=a_re, a_im=a_im):
            x_re, x_im = carry
            t = jnp.where(d == 0, tt, tc - 1 - tt)
            rows = pl.ds(pl.multiple_of(t * N_SEG, N_SEG), N_SEG)
            n_re = a_re * x_re - a_im * x_im + bu_scr[rows, re_cols]
            n_im = a_re * x_im + a_im * x_re + bu_scr[rows, im_cols]
            bu_scr[rows, re_cols] = n_re
            bu_scr[rows, im_cols] = n_im
            return n_re, n_im

        x_re, x_im = lax.fori_loop(0, tc, body, (st_scr[:, re_cols], st_scr[:, im_cols]), unroll=8)
        st_scr[:, re_cols] = x_re
        st_scr[:, im_cols] = x_im

    if emit_y:
        y = jnp.dot(bu_scr[...].astype(BF16), c_ref[...], preferred_element_type=F32)
        for scr, cols in halves:
            scr[...] = y[:, cols]
        for s in range(N_SEG):
            for scr, cols in halves:
                y_ref[0, s, :, cols] = scr[pl.ds(s, tc, stride=N_SEG), :]

    @pl.when(c == pl.num_programs(2) - 1)
    def _():
        fin_ref[0, 0] = st_scr[...]


def _ssm_pipe_kernel(u_ref, b_ref, a_ref, c_ref, init_ref, y_ref, fin_ref, bu0, bu1, bu2, st_scr, ut_lo, ut_hi,
                     yt_lo, yt_hi, *, tc, nc):
    d = pl.program_id(0)
    s = pl.program_id(2)
    bufs = (bu0, bu1, bu2)
    u_halves = ((ut_lo, slice(0, LANES)), (ut_hi, slice(LANES, 2 * LANES)))
    y_halves = ((yt_lo, slice(0, LANES)), (yt_hi, slice(LANES, 2 * LANES)))

    @pl.when(s == 0)
    def _():
        st_scr[...] = init_ref[0, 0]
        for buf in bufs:
            buf[...] = jnp.zeros_like(buf)

    def projections(b_buf, c_buf):
        for sg in range(N_SEG):
            u_s = u_ref[sg].astype(F32)
            for scr, cols in u_halves:
                scr[pl.ds(sg, tc, stride=N_SEG), :] = u_s[:, cols]
        u_tm = jnp.concatenate([ut_lo[...], ut_hi[...]], axis=1).astype(BF16)
        y = jnp.dot(c_buf[...].astype(BF16), c_ref[...], preferred_element_type=F32)
        b_buf[...] = jnp.dot(u_tm, b_ref[0], preferred_element_type=F32)
        for scr, cols in y_halves:
            scr[...] = y[:, cols]
        for sg in range(N_SEG):
            for scr, cols in y_halves:
                y_ref[0, sg, :, cols] = scr[pl.ds(sg, tc, stride=N_SEG), :]

    def recurrence(buf):
        w = SSM_COMPLEX // SCAN_COL_BLOCKS
        for cb in range(SCAN_COL_BLOCKS):
            re_cols = slice(cb * w, (cb + 1) * w)
            im_cols = slice(SSM_COMPLEX + cb * w, SSM_COMPLEX + (cb + 1) * w)
            a_re = jnp.broadcast_to(a_ref[0, 0:1, re_cols], (N_SEG, w))
            a_im = jnp.broadcast_to(a_ref[0, 1:2, re_cols], (N_SEG, w))

            def body(tt, carry, re_cols=re_cols, im_cols=im_cols, a_re=a_re, a_im=a_im):
                x_re, x_im = carry
                t = jnp.where(d == 0, tt, tc - 1 - tt)
                rows = pl.ds(pl.multiple_of(t * N_SEG, N_SEG), N_SEG)
                n_re = a_re * x_re - a_im * x_im + buf[rows, re_cols]
                n_im = a_re * x_im + a_im * x_re + buf[rows, im_cols]
                buf[rows, re_cols] = n_re
                buf[rows, im_cols] = n_im
                return n_re, n_im

            x_re, x_im = lax.fori_loop(0, tc, body, (st_scr[:, re_cols], st_scr[:, im_cols]), unroll=8)
            st_scr[:, re_cols] = x_re
            st_scr[:, im_cols] = x_im

    turn = lax.rem(s, 3)
    for k in range(3):
        @pl.when(turn == k)
        def _(k=k):
            projections(bufs[k], bufs[(k + 1) % 3])

    for k in range(3):
        @pl.when((turn == k) & (s >= 1) & (s <= nc))
        def _(k=k):
            recurrence(bufs[(k + 2) % 3])

    @pl.when(s == pl.num_programs(2) - 1)
    def _():
        fin_ref[0, 0] = st_scr[...]


def _ssm_scan_pipelined(ssm_proj, seg_len, n_groups, b_blk, a_rows, c_blk, init, tc):
    u = ssm_proj.reshape(-1, seg_len, 2 * SSM_WIDTH)
    nc = seg_len // tc

    def chunk(d, c):
        c = jnp.clip(c, 0, nc - 1)
        return jnp.where(d == 0, c, nc - 1 - c)

    rows = tc * N_SEG
    y, fin = pl.pallas_call(
        functools.partial(_ssm_pipe_kernel, tc=tc, nc=nc),
        grid=(2, n_groups, nc + 2),
        in_specs=[
            pl.BlockSpec((N_SEG, tc, SSM_WIDTH), lambda d, g, s: (g, chunk(d, s), 0)),
            pl.BlockSpec((1, SSM_WIDTH, SSM_REAL), lambda d, g, s: (d, 0, 0)),
            pl.BlockSpec((1, 2, SSM_COMPLEX), lambda d, g, s: (d, 0, 0)),
            pl.BlockSpec((SSM_REAL, SSM_WIDTH), lambda d, g, s: (0, 0)),
            pl.BlockSpec((1, 1, N_SEG, SSM_REAL), lambda d, g, s: (d, g, 0, 0)),
        ],
        out_specs=[
            pl.BlockSpec((1, N_SEG, tc, SSM_WIDTH), lambda d, g, s: (d, g, chunk(d, s - 2), 0)),
            pl.BlockSpec((1, 1, N_SEG, SSM_REAL), lambda d, g, s: (d, g, 0, 0)),
        ],
        out_shape=[
            jax.ShapeDtypeStruct((2, n_groups * N_SEG, seg_len, SSM_WIDTH), F32),
            jax.ShapeDtypeStruct((2, n_groups, N_SEG, SSM_REAL), F32),
        ],
        scratch_shapes=[pltpu.VMEM((rows, SSM_REAL), F32)] * 3 + [pltpu.VMEM((N_SEG, SSM_REAL), F32)]
        + [pltpu.VMEM((rows, LANES), F32)] * 4,
        compiler_params=_cparams("arbitrary", "arbitrary", "arbitrary"),
        name="ssm_scan_y",
    )(u, b_blk, a_rows, c_blk, init)
    return y.reshape(2, -1, SSM_WIDTH), fin


def _ssm_scan(ssm_proj, seg_len, first_group, n_groups, b_blk, a_rows, c_blk, init, tc, emit_y):
    u = ssm_proj.reshape(-1, seg_len, 2 * SSM_WIDTH)
    nc = seg_len // tc

    def chunk(d, c):
        return jnp.where(d == 0, c, nc - 1 - c)

    out_specs = [pl.BlockSpec((1, 1, N_SEG, SSM_REAL), lambda d, g, c: (d, g, 0, 0))]
    out_shape = [jax.ShapeDtypeStruct((2, n_groups, N_SEG, SSM_REAL), F32)]
    if emit_y:
        out_specs = [pl.BlockSpec((1, N_SEG, tc, SSM_WIDTH), lambda d, g, c: (d, g, chunk(d, c), 0))] + out_specs
        out_shape = [jax.ShapeDtypeStruct((2, n_groups * N_SEG, seg_len, SSM_WIDTH), F32)] + out_shape
    res = pl.pallas_call(
        functools.partial(_ssm_kernel, tc=tc, emit_y=emit_y),
        grid=(2, n_groups, nc),
        in_specs=[
            pl.BlockSpec((N_SEG, tc, SSM_WIDTH), lambda d, g, c: (first_group + g, chunk(d, c), 0)),
            pl.BlockSpec((1, SSM_WIDTH, SSM_REAL), lambda d, g, c: (d, 0, 0)),
            pl.BlockSpec((1, 2, SSM_COMPLEX), lambda d, g, c: (d, 0, 0)),
            pl.BlockSpec((SSM_REAL, SSM_WIDTH), lambda d, g, c: (0, 0)),
            pl.BlockSpec((1, 1, N_SEG, SSM_REAL), lambda d, g, c: (d, g, 0, 0)),
        ],
        out_specs=out_specs,
        out_shape=out_shape,
        scratch_shapes=[pltpu.VMEM((tc * N_SEG, SSM_REAL), F32), pltpu.VMEM((N_SEG, SSM_REAL), F32),
                        pltpu.VMEM((tc * N_SEG, LANES), F32), pltpu.VMEM((tc * N_SEG, LANES), F32)],
        compiler_params=_cparams("arbitrary", "arbitrary", "arbitrary"),
        name="ssm_scan_y" if emit_y else "ssm_scan_state",
    )(u, b_blk, a_rows, c_blk, init)
    if emit_y:
        return res[0].reshape(2, -1, SSM_WIDTH), res[1]
    return None, res[0]


def _ssm_params(lam_re, lam_im, log_step, b_re, b_im, c_re, c_im):
    lam = lax.complex(lam_re.astype(F32), lam_im.astype(F32))
    step = jnp.exp(log_step.astype(F32))[..., None]
    lam_bar = jnp.exp(lam * step)
    b = lax.complex(b_re.astype(F32), b_im.astype(F32))
    b_bar = ((lam_bar - 1.0) / lam)[..., None] * b
    eye = jnp.eye(SSM_GROUPS, dtype=F32)
    b_blk_re = jnp.einsum('dgph,gk->dghkp', jnp.real(b_bar), eye).reshape(2, SSM_WIDTH, SSM_COMPLEX)
    b_blk_im = jnp.einsum('dgph,gk->dghkp', jnp.imag(b_bar), eye).reshape(2, SSM_WIDTH, SSM_COMPLEX)
    b_blk = jnp.concatenate([b_blk_re, b_blk_im], axis=-1).astype(BF16)
    c_blk_re = jnp.einsum('ghp,gk->gpkh', c_re.astype(F32), eye).reshape(SSM_COMPLEX, SSM_WIDTH)
    c_blk_im = jnp.einsum('ghp,gk->gpkh', c_im.astype(F32), eye).reshape(SSM_COMPLEX, SSM_WIDTH)
    c_blk = jnp.concatenate([c_blk_re, -c_blk_im], axis=0).astype(BF16)
    a_rows = jnp.stack([jnp.real(lam_bar).reshape(2, SSM_COMPLEX),
                        jnp.imag(lam_bar).reshape(2, SSM_COMPLEX)], axis=1)
    return lam_bar.reshape(2, SSM_COMPLEX), b_blk, a_rows, c_blk


def _chain_init(fin, lam_pow):
    f = lax.complex(fin[..., :SSM_COMPLEX], fin[..., SSM_COMPLEX:])
    zero = jnp.zeros_like(f[0, 0])
    fwd = [zero]
    for s in range(1, N_SEG):
        fwd.append(lam_pow[0] * fwd[-1] + f[0, s - 1])
    bwd = [zero]
    for s in range(N_SEG - 2, -1, -1):
        bwd.append(lam_pow[1] * bwd[-1] + f[1, s + 1])
    init = jnp.stack([jnp.stack(fwd), jnp.stack(bwd[::-1])])
    return jnp.concatenate([jnp.real(init), jnp.imag(init)], axis=-1).astype(F32)


def _attn_sgu_kernel(flags_ref, q_ref, kp_ref, kc_ref, kn_ref, vp_ref, vc_ref, vn_ref, bias_ref, sink_ref,
                     sgu_ref, sgn_ref, sgw_ref, sgb_ref, attn_ref, sguo_ref, *, nbq):
    n = pl.program_id(0)
    no_prev = flags_ref[0, n * nbq] == 0
    no_next = flags_ref[1, n * nbq + nbq - 1] == 0

    def k_block(i):
        if i == 0:
            return kp_ref[...]
        if i == nbq + 1:
            return kn_ref[...]
        return kc_ref[(i - 1) * BLOCK:i * BLOCK, :]

    def vt_block(i):
        if i == 0:
            return vp_ref[...]
        if i == nbq + 1:
            return vn_ref[...]
        return vc_ref[:, (i - 1) * BLOCK:i * BLOCK]

    for b in range(nbq):
        rows = slice(b * BLOCK, (b + 1) * BLOCK)
        q = q_ref[rows, :]
        outs = []
        for j in range(N_KV_HEADS):
            qs = jnp.concatenate(
                [q[:, (j * Q_PER_KV + i) * HEAD_DIM:(j * Q_PER_KV + i + 1) * HEAD_DIM] for i in range(Q_PER_KV)],
                axis=0)
            kv = slice(j * HEAD_DIM, (j + 1) * HEAD_DIM)
            s = []
            for i in range(3):
                si = lax.dot_general(k_block(b + i)[:, kv], qs, (((1,), (1,)), ((), ())),
                                     preferred_element_type=F32)
                si = si + bias_ref[j, i * BLOCK:(i + 1) * BLOCK, :]
                if b == 0 and i == 0:
                    si = jnp.where(no_prev, NEG_INF, si)
                if b == nbq - 1 and i == 2:
                    si = jnp.where(no_next, NEG_INF, si)
                s.append(si)
            sink = sink_ref[j]
            m = jnp.max(jnp.maximum(jnp.maximum(s[0], s[1]), s[2]), axis=0, keepdims=True)
            m = jnp.maximum(m, sink)
            p = [jnp.exp(si - m) for si in s]
            denom = jnp.sum(p[0] + p[1] + p[2], axis=0, keepdims=True) + jnp.exp(sink - m)
            inv = 1.0 / denom
            o_t = sum(jnp.dot(vt_block(b + i)[kv, :], (p[i] * inv).astype(BF16), preferred_element_type=F32)
                      for i in range(3))
            outs += [jnp.transpose(o_t[:, i * BLOCK:(i + 1) * BLOCK]) for i in range(Q_PER_KV)]
        attn_ref[rows, :] = jnp.concatenate(outs, axis=1).astype(BF16)

        gu = jax.nn.gelu(sgu_ref[rows, :SGU_WIDTH].astype(F32))
        gv = _rms(jax.nn.gelu(sgu_ref[rows, SGU_WIDTH:].astype(F32)), sgn_ref[...]).astype(BF16)
        mixed = jnp.concatenate(
            [jnp.dot(sgw_ref[g], gv[:, g * SGU_GROUP_WIDTH:(g + 1) * SGU_GROUP_WIDTH], preferred_element_type=F32)
             for g in range(SGU_GROUPS)], axis=1) + sgb_ref[...]
        sguo_ref[rows, :] = (gu * mixed).astype(BF16)


def _attn_sgu(flags, qk, vt, sgu, bias, sink, sgn, sgw, sgb, nbq):
    t = qk.shape[0]
    nb = t // BLOCK
    tile = nbq * BLOCK
    kcol = ATTN_WIDTH // KV_WIDTH
    before = lambda n: jnp.maximum(n * nbq - 1, 0)
    after = lambda n: jnp.minimum((n + 1) * nbq, nb - 1)
    grid_spec = pltpu.PrefetchScalarGridSpec(
        num_scalar_prefetch=1,
        grid=(nb // nbq,),
        in_specs=[
            pl.BlockSpec((tile, ATTN_WIDTH), lambda n, f: (n, 0)),
            pl.BlockSpec((BLOCK, KV_WIDTH), lambda n, f: (before(n), kcol)),
            pl.BlockSpec((tile, KV_WIDTH), lambda n, f: (n, kcol)),
            pl.BlockSpec((BLOCK, KV_WIDTH), lambda n, f: (after(n), kcol)),
            pl.BlockSpec((KV_WIDTH, BLOCK), lambda n, f: (0, before(n))),
            pl.BlockSpec((KV_WIDTH, tile), lambda n, f: (0, n)),
            pl.BlockSpec((KV_WIDTH, BLOCK), lambda n, f: (0, after(n))),
            pl.BlockSpec((N_KV_HEADS, 3 * BLOCK, Q_PER_KV * BLOCK), lambda n, f: (0, 0, 0)),
            pl.BlockSpec((N_KV_HEADS, 1, Q_PER_KV * BLOCK), lambda n, f: (0, 0, 0)),
            pl.BlockSpec((tile, 2 * SGU_WIDTH), lambda n, f: (n, 0)),
            pl.BlockSpec((1, SGU_WIDTH), lambda n, f: (0, 0)),
            pl.BlockSpec((SGU_GROUPS, BLOCK, BLOCK), lambda n, f: (0, 0, 0)),
            pl.BlockSpec((BLOCK, SGU_WIDTH), lambda n, f: (0, 0)),
        ],
        out_specs=[
            pl.BlockSpec((tile, ATTN_WIDTH), lambda n, f: (n, 0)),
            pl.BlockSpec((tile, SGU_WIDTH), lambda n, f: (n, 0)),
        ],
    )
    return pl.pallas_call(
        functools.partial(_attn_sgu_kernel, nbq=nbq),
        grid_spec=grid_spec,
        out_shape=[jax.ShapeDtypeStruct((t, ATTN_WIDTH), BF16), jax.ShapeDtypeStruct((t, SGU_WIDTH), BF16)],
        compiler_params=_cparams("arbitrary"),
        name="attn_sgu",
    )(flags, qk, qk, qk, qk, vt, vt, vt, bias, sink, sgu, sgn, sgw, sgb)


def _t5_bucket(rel):
    half = N_BUCKETS // 2
    max_exact = half // 2
    ret = jnp.where(rel > 0, half, 0)
    n = jnp.abs(rel)
    nf = jnp.maximum(n, 1).astype(F32)
    large = max_exact + (jnp.log(nf / max_exact) / math.log(MAX_DISTANCE / max_exact)
                         * (half - max_exact)).astype(jnp.int32)
    large = jnp.minimum(large, half - 1)
    return ret + jnp.where(n < max_exact, n, large)


def _band_bias(rel_table):
    q_pos = jnp.arange(BLOCK)[:, None]
    k_pos = jnp.arange(3 * BLOCK)[None, :] - BLOCK
    rel = k_pos - q_pos
    hit = _t5_bucket(rel)[..., None] == jnp.arange(N_BUCKETS)
    bias = jnp.sum(jnp.where(hit[..., None], rel_table.astype(F32), 0.0), axis=2)
    bias = jnp.where((jnp.abs(rel) <= BLOCK)[..., None], bias, NEG_INF)
    bias = bias.reshape(BLOCK, 3 * BLOCK, N_KV_HEADS, Q_PER_KV)
    return bias.transpose(2, 1, 3, 0).reshape(N_KV_HEADS, 3 * BLOCK, Q_PER_KV * BLOCK)


def _outproj_router_kernel(*refs, tm, n_parts, steps_per_part):
    x_refs = refs[:n_parts]
    (ssm_ref, yf_ref, yb_ref, attn_ref, sguo_ref, d_ref, gout_ref, wout_ref, gffn_ref, wrh_ref, wrl_ref, br_ref,
     earlier_ref, x1_ref, h_ref, idx_ref, rank_ref, wgt_ref, cnt_ref) = refs[n_parts:]
    u = ssm_ref[:, :SSM_WIDTH].astype(F32)
    gate = ssm_ref[:, SSM_WIDTH:].astype(F32)
    ya = yf_ref[...] + yb_ref[...] + d_ref[...] * u
    ya = jax.nn.gelu(ya) * jax.nn.sigmoid(gate)
    g = gout_ref[...]
    mixed = jnp.concatenate([
        _rms(ya, g[:, :SSM_WIDTH]),
        _rms(attn_ref[...].astype(F32), g[:, SSM_WIDTH:SSM_WIDTH + ATTN_WIDTH]),
        _rms(sguo_ref[...].astype(F32), g[:, SSM_WIDTH + ATTN_WIDTH:]),
    ], axis=1).astype(BF16)
    x1 = _pick_part(x_refs, steps_per_part) + jnp.dot(mixed, wout_ref[...], preferred_element_type=F32)
    x1_ref[...] = x1
    h = _rms(x1, gffn_ref[...])
    _store_token_tiles(h_ref, 0, h, tm)

    h_hi = h.astype(BF16)
    h_lo = (h - h_hi.astype(F32)).astype(BF16)
    nt_dot = lambda a, b: lax.dot_general(a, b, (((1,), (1,)), ((), ())), preferred_element_type=F32)
    lg = nt_dot(wrh_ref[...], h_hi) + nt_dot(wrl_ref[...], h_hi) + nt_dot(wrh_ref[...], h_lo) + br_ref[...]
    row = lax.broadcasted_iota(jnp.int32, (N_EXPERTS, tm), 0)
    vals, idxs, sels = [], [], []
    for _ in range(TOP_K):
        m = jnp.max(lg, axis=0, keepdims=True)
        idx = jnp.min(jnp.where(lg == m, row, N_EXPERTS), axis=0, keepdims=True)
        sel = row == idx
        lg = jnp.where(sel, -jnp.inf, lg)
        vals.append(m)
        idxs.append(idx)
        sels.append(sel)
    e = [jnp.exp(v - vals[0]) for v in vals]
    tot = e[0] + e[1] + e[2] + e[3]
    wgt_ref[...] = jnp.concatenate([ek / tot for ek in e], axis=0)
    idx_ref[...] = jnp.concatenate(idxs, axis=0)

    picked = (sels[0] | sels[1] | sels[2] | sels[3])
    pm = jnp.where(picked, 1.0, 0.0)
    before = jnp.dot(pm.astype(BF16), earlier_ref[...], preferred_element_type=F32)
    rank_ref[...] = jnp.concatenate(
        [jnp.sum(jnp.where(s, before, 0.0), axis=0, keepdims=True) for s in sels], axis=0).astype(jnp.int32)
    cnt_ref[0] = jnp.sum(pm, axis=1, keepdims=True).astype(jnp.int32)


def _outproj_router(x_parts, ssm, y_dirs, attn, sguo, d_skip, g_out, w_out, g_ffn, w_r, b_r, tm):
    t = ssm.shape[0]
    row = lambda w: pl.BlockSpec((tm, w), lambda i: (i, 0))
    full = lambda a, b: pl.BlockSpec((a, b), lambda i: (0, 0))
    direction = lambda d: pl.BlockSpec((None, tm, SSM_WIDTH), lambda i: (d, i, 0))
    per_choice = pl.BlockSpec((TOP_K, tm), lambda i: (0, i))
    w_rt = w_r.astype(F32).T
    w_hi = w_rt.astype(BF16)
    w_lo = (w_rt - w_hi.astype(F32)).astype(BF16)
    earlier = (jnp.arange(tm)[:, None] < jnp.arange(tm)[None, :]).astype(BF16)
    return pl.pallas_call(
        functools.partial(_outproj_router_kernel, tm=tm, n_parts=len(x_parts),
                          steps_per_part=x_parts[0].shape[0] // tm),
        grid=(t // tm,),
        in_specs=_part_specs(x_parts, tm) + [
            row(2 * SSM_WIDTH), direction(0), direction(1), row(ATTN_WIDTH), row(SGU_WIDTH),
            full(1, SSM_WIDTH), full(1, D_MODEL), full(D_MODEL, D_MODEL), full(1, D_MODEL),
            full(N_EXPERTS, D_MODEL), full(N_EXPERTS, D_MODEL), full(N_EXPERTS, 1), full(tm, tm),
        ],
        out_specs=[row(D_MODEL), pl.BlockSpec((tm * TOKEN_TILE_ROWS, LANES), lambda i: (i, 0)),
                   per_choice, per_choice, per_choice, pl.BlockSpec((1, N_EXPERTS, 1), lambda i: (i, 0, 0))],
        out_shape=[
            jax.ShapeDtypeStruct((t, D_MODEL), F32),
            jax.ShapeDtypeStruct((t * TOKEN_TILE_ROWS, LANES), F32),
            jax.ShapeDtypeStruct((TOP_K, t), jnp.int32),
            jax.ShapeDtypeStruct((TOP_K, t), jnp.int32),
            jax.ShapeDtypeStruct((TOP_K, t), F32),
            jax.ShapeDtypeStruct((t // tm, N_EXPERTS, 1), jnp.int32),
        ],
        compiler_params=_cparams("arbitrary"),
        name="outproj_router",
    )(*x_parts, ssm, y_dirs, y_dirs, attn, sguo, d_skip, g_out, w_out, g_ffn, w_hi, w_lo,
      b_r.astype(F32).reshape(N_EXPERTS, 1), earlier)


ISSUE_UNROLL = 8
FETCH_ROWS = 16


def _for_rows(n, fn):
    def body(c, carry):
        for u in range(ISSUE_UNROLL):
            fn(c * ISSUE_UNROLL + u, u % 2)
        return carry

    lax.fori_loop(0, n // ISSUE_UNROLL, body, 0)


def _tile_rows(ref, first_tile, n_tiles=1):
    start = first_tile * TOKEN_TILE_ROWS
    if not isinstance(start, int):
        start = pl.multiple_of(start, TOKEN_TILE_ROWS)
    return ref.at[pl.ds(start, n_tiles * TOKEN_TILE_ROWS)]


EXPERT_COL_CHUNK = 1024


GATHER_SLOTS = 3


def _experts_kernel(ltile_ref, otile_ref, exp_ref, lo_ref, hi_ref, tok_ref, tok1_ref, tok2_ref, wrow_ref, h_hbm, wgu_ref,
                    bgu_ref, wd_ref, bd_ref, out_ref, x_buf, y_buf, wgu_bf, wd_bf, gsem, *, tm):
    w = pl.program_id(0)
    nw = pl.num_programs(0)
    slot = lax.rem(w, GATHER_SLOTS)
    ahead = lax.rem(w + GATHER_SLOTS - 1, GATHER_SLOTS)
    prev = jnp.maximum(w - 1, 0)

    def gather_row(list_ref, to_slot, r, queue):
        pltpu.make_async_copy(_tile_rows(h_hbm, list_ref[0, 0, r]), _tile_rows(x_buf, to_slot * tm + r),
                              gsem.at[to_slot]).start(priority=queue)

    def wait_gather(of_slot):
        pltpu.make_async_copy(_tile_rows(h_hbm, 0, tm), _tile_rows(x_buf, of_slot * tm, tm), gsem.at[of_slot]).wait()

    @pl.when(w == 0)
    def _():
        _for_rows(tm, lambda r, queue: gather_row(tok_ref, 0, r, queue))
        _for_rows(tm, lambda r, queue: gather_row(tok1_ref, 1, r, queue))

    @pl.when((w == 0) | (exp_ref[w] != exp_ref[prev]))
    def _():
        wgu_bf[...] = wgu_ref[0].astype(BF16)
        wd_bf[...] = wd_ref[0].astype(BF16)

    wait_gather(slot)

    n_pieces = (D_EXPERT + D_MODEL) // EXPERT_COL_CHUNK
    per_piece = tm // n_pieces
    piece = 0

    def issue_next_gather():
        nonlocal piece
        for r in range(piece * per_piece, (piece + 1) * per_piece):
            gather_row(tok2_ref, ahead, r, r % 2)
        piece += 1

    w_rows = jnp.transpose(jnp.broadcast_to(wrow_ref[0], (LANES, tm)))
    w_rows = jnp.concatenate([w_rows] * (EXPERT_COL_CHUNK // LANES), axis=1)

    base = pl.multiple_of(slot * (tm * TOKEN_TILE_ROWS), tm * TOKEN_TILE_ROWS)
    x = _load_token_tiles(x_buf, base, tm).astype(BF16)
    acts = []
    for c in range(D_EXPERT // EXPERT_COL_CHUNK):
        issue_next_gather()
        g_cols = slice(c * EXPERT_COL_CHUNK, (c + 1) * EXPERT_COL_CHUNK)
        u_cols = slice(D_EXPERT + c * EXPERT_COL_CHUNK, D_EXPERT + (c + 1) * EXPERT_COL_CHUNK)
        gate = jnp.dot(x, wgu_bf[:, g_cols], preferred_element_type=F32) + bgu_ref[0, :, g_cols]
        up = jnp.dot(x, wgu_bf[:, u_cols], preferred_element_type=F32) + bgu_ref[0, :, u_cols]
        gate = jnp.minimum(gate, SWIGLU_LIMIT)
        up = jnp.clip(up, -SWIGLU_LIMIT, SWIGLU_LIMIT)
        acts.append(((up + 1.0) * gate * jax.nn.sigmoid(SWIGLU_ALPHA * gate)).astype(BF16))
    act = jnp.concatenate(acts, axis=1)
    tiles_per_chunk = EXPERT_COL_CHUNK // LANES
    for c in range(D_MODEL // EXPERT_COL_CHUNK):
        issue_next_gather()
        cols = slice(c * EXPERT_COL_CHUNK, (c + 1) * EXPERT_COL_CHUNK)
        y = (jnp.dot(act, wd_bf[:, cols], preferred_element_type=F32) + bd_ref[0, :, cols]) * w_rows
        for j in range(tiles_per_chunk):
            y_buf[pl.ds(c * tiles_per_chunk + j, tm, stride=TOKEN_TILE_ROWS), :] = y[:, j * LANES:(j + 1) * LANES]

    first = (w == 0) | (otile_ref[w] != otile_ref[prev])

    @pl.when(first)
    def _():
        out_ref[...] = y_buf[...]

    @pl.when(jnp.logical_not(first))
    def _():
        row = lax.shift_right_logical(lax.broadcasted_iota(jnp.int32, y_buf.shape, 0),
                                      TOKEN_TILE_ROWS.bit_length() - 1)
        mine = (row >= lo_ref[w]) & (row < hi_ref[w])
        out_ref[...] = jnp.where(mine, y_buf[...], out_ref[...])

    @pl.when(w == nw - 1)
    def _():
        for k in range(1, GATHER_SLOTS):
            wait_gather(lax.rem(w + k, GATHER_SLOTS))


def _experts(items, tok_sorted, w_sorted, h_tiles, wgu, bgu, wd, bd, layer, tm):
    nw = items[0].shape[0]
    n_rows = tok_sorted.shape[0]
    nt = n_rows // tm
    lists = lambda a: a.reshape(nt, 1, tm)
    later = lambda k: (lambda i, lt, ot, ex, lo, hi: (lt[jnp.minimum(i + k, nw - 1)], 0, 0))
    by_expert = lambda i, lt, ot, ex, lo, hi: (layer, ex[i], 0, 0)
    grid_spec = pltpu.PrefetchScalarGridSpec(
        num_scalar_prefetch=5,
        grid=(nw,),
        in_specs=[
            pl.BlockSpec((1, 1, tm), later(0), memory_space=pltpu.SMEM),
            pl.BlockSpec((1, 1, tm), later(1), memory_space=pltpu.SMEM),
            pl.BlockSpec((1, 1, tm), later(2), memory_space=pltpu.SMEM),
            pl.BlockSpec((1, 1, tm), later(0)),
            pl.BlockSpec(memory_space=pl.ANY),
            pl.BlockSpec((None, 1, D_MODEL, 2 * D_EXPERT), by_expert),
            pl.BlockSpec((None, 1, 1, 2 * D_EXPERT), by_expert),
            pl.BlockSpec((None, 1, D_EXPERT, D_MODEL), by_expert),
            pl.BlockSpec((None, 1, 1, D_MODEL), by_expert),
        ],
        out_specs=pl.BlockSpec((tm * TOKEN_TILE_ROWS, LANES), lambda i, lt, ot, ex, lo, hi: (ot[i], 0)),
        scratch_shapes=[
            pltpu.VMEM((GATHER_SLOTS * tm * TOKEN_TILE_ROWS, LANES), F32),
            pltpu.VMEM((tm * TOKEN_TILE_ROWS, LANES), F32),
            pltpu.VMEM((D_MODEL, 2 * D_EXPERT), BF16),
            pltpu.VMEM((D_EXPERT, D_MODEL), BF16),
            pltpu.SemaphoreType.DMA((GATHER_SLOTS,)),
        ],
    )
    return pl.pallas_call(
        functools.partial(_experts_kernel, tm=tm),
        grid_spec=grid_spec,
        out_shape=jax.ShapeDtypeStruct(((n_rows + FETCH_ROWS) * TOKEN_TILE_ROWS, LANES), F32),
        compiler_params=_cparams("arbitrary"),
        name="experts",
    )(*items, lists(tok_sorted), lists(tok_sorted), lists(tok_sorted), lists(w_sorted), h_tiles, wgu, bgu, wd, bd)


def _combine_kernel(src_ref, cst_ref, nch_ref, y_hbm, x1_ref, qpos_ref, g_ref, *rest, tm, nq, final, n_out):
    out_refs, (yc_buf, sem) = rest[:n_out], rest[n_out:]
    i = pl.program_id(0)
    slot = lax.rem(i, 2)

    def chunk_copy(src_row, dst_row, at_slot):
        return pltpu.make_async_copy(_tile_rows(y_hbm, src_row, FETCH_ROWS),
                                     _tile_rows(yc_buf, at_slot * nq + dst_row, FETCH_ROWS), sem.at[at_slot])

    def fetch_tile(tile, to_slot):
        for e in range(N_EXPERTS):
            src0 = src_ref[tile * N_EXPERTS + e]
            dst0 = cst_ref[tile * N_EXPERTS + e]

            def fetch(j, carry, src0=src0, dst0=dst0):
                chunk_copy(src0 + j * FETCH_ROWS, dst0 + j * FETCH_ROWS, to_slot).start()
                return carry

            lax.fori_loop(0, nch_ref[tile * N_EXPERTS + e], fetch, 0)

    @pl.when(i == 0)
    def _():
        yc_buf[...] = jnp.zeros_like(yc_buf)
        fetch_tile(0, 0)

    @pl.when(i + 1 < pl.num_programs(0))
    def _():
        fetch_tile(i + 1, 1 - slot)

    total = nch_ref[i * N_EXPERTS]
    for e in range(1, N_EXPERTS):
        total = total + nch_ref[i * N_EXPERTS + e]

    def wait(j, carry):
        chunk_copy(0, 0, slot).wait()
        return carry

    lax.fori_loop(0, total, wait, 0)

    q = lax.broadcasted_iota(jnp.int32, (tm, nq), 1)
    qpos = qpos_ref[...]
    hit = q == qpos[:, 0:1]
    for k in range(1, TOP_K):
        hit = hit | (q == qpos[:, k:k + 1])
    p = jnp.where(hit, 1.0, 0.0).astype(BF16)
    base = pl.multiple_of(slot * (nq * TOKEN_TILE_ROWS), nq * TOKEN_TILE_ROWS)
    yc = _load_token_tiles(yc_buf, base, nq).astype(BF16)
    x2 = x1_ref[...] + jnp.dot(p, yc, preferred_element_type=F32)
    if final:
        x2 = _rms(x2, g_ref[...])
    steps_per_out = pl.num_programs(0) // n_out
    for j, out_ref in enumerate(out_refs):
        @pl.when((i >= j * steps_per_out) & (i < (j + 1) * steps_per_out))
        def _(out_ref=out_ref):
            out_ref[...] = x2


def _combine(src0, cstart, nchunk, y_sorted, x1, qpos, g_final, tm, final, n_out):
    t = x1.shape[0]
    steps_per_out = t // tm // n_out
    nq = -(-(tm * TOP_K + N_EXPERTS * (FETCH_ROWS - 1)) // LANES) * LANES
    grid_spec = pltpu.PrefetchScalarGridSpec(
        num_scalar_prefetch=3,
        grid=(t // tm,),
        in_specs=[
            pl.BlockSpec(memory_space=pl.ANY),
            pl.BlockSpec((tm, D_MODEL), lambda i, a, b, c: (i, 0)),
            pl.BlockSpec((tm, TOP_K), lambda i, a, b, c: (i, 0)),
            pl.BlockSpec((1, D_MODEL), lambda i, a, b, c: (0, 0)),
        ],
        out_specs=[pl.BlockSpec((tm, D_MODEL), (lambda j: lambda i, a, b, c: (
            jnp.clip(i - j * steps_per_out, 0, steps_per_out - 1), 0))(j)) for j in range(n_out)],
        scratch_shapes=[pltpu.VMEM((2 * nq * TOKEN_TILE_ROWS, LANES), F32), pltpu.SemaphoreType.DMA((2,))],
    )
    return pl.pallas_call(
        functools.partial(_combine_kernel, tm=tm, nq=nq, final=final, n_out=n_out),
        grid_spec=grid_spec,
        out_shape=[jax.ShapeDtypeStruct((t // n_out, D_MODEL), F32)] * n_out,
        compiler_params=_cparams("arbitrary"),
        name="combine",
    )(src0.reshape(-1), cstart.reshape(-1), nchunk.reshape(-1), y_sorted, x1, qpos, g_final)


def _routing_lists(idx, lrank, wgt, tcnt, tm_tok, tm):
    t = idx.shape[1]
    n_rows = t * TOP_K
    nt = n_rows // tm
    cnt = jnp.sum(tcnt, axis=0)
    group_end = jnp.cumsum(cnt)
    group_start = group_end - cnt
    src0 = group_start[None, :] + jnp.cumsum(tcnt, axis=0) - tcnt
    aligned = -(-tcnt // FETCH_ROWS) * FETCH_ROWS
    cstart = jnp.cumsum(aligned, axis=1) - aligned
    nchunk = aligned // FETCH_ROWS
    experts = jnp.arange(N_EXPERTS, dtype=jnp.int32)
    onehot = idx.reshape(TOP_K, -1, tm_tok)[..., None] == experts
    pick = lambda table: jnp.sum(jnp.where(onehot, table[None, :, None, :], 0), axis=-1).reshape(TOP_K, t)
    pos = lrank + pick(src0)
    qpos = (lrank + pick(cstart)).T
    flat = jnp.arange(n_rows, dtype=jnp.int32)
    _, flat_sorted, w_sorted = lax.sort((pos.reshape(-1).astype(jnp.int32), flat, wgt.reshape(-1)), num_keys=1)
    tok_sorted = flat_sorted % t
    bounds = jnp.sort(jnp.concatenate([jnp.arange(nt, dtype=jnp.int32) * tm, group_start.astype(jnp.int32)]))
    nxt = jnp.concatenate([bounds[1:], jnp.array([n_rows], jnp.int32)])
    tile = jnp.minimum(bounds // tm, nt - 1)
    lo = bounds - tile * tm
    hi = jnp.where(nxt > bounds, nxt - tile * tm, lo)
    expert = jnp.minimum(jnp.sum(group_end[None, :] <= bounds[:, None], axis=-1), N_EXPERTS - 1)
    last = lambda a, v: jnp.concatenate([a, jnp.array([v], a.dtype)])
    items = (last(tile, nt - 1), last(tile, nt), jnp.concatenate([expert, expert[-1:]]), last(lo, 0), last(hi, 0))
    items = tuple(a.astype(jnp.int32) for a in items)
    to_i32 = lambda a: a.astype(jnp.int32)
    return items, tok_sorted, w_sorted, to_i32(src0), to_i32(cstart), to_i32(nchunk), to_i32(qpos)


def _block_flags(n_prompt_seq, seg_len, n_tokens):
    nb = n_tokens // BLOCK
    per_seg = seg_len // BLOCK
    b = jnp.arange(nb)
    prompt_blocks = n_prompt_seq * per_seg
    in_prompt = b < prompt_blocks
    first = jnp.where(in_prompt, b % per_seg == 0, b == prompt_blocks)
    last = jnp.where(in_prompt, b % per_seg == per_seg - 1, b == nb - 1)
    return jnp.stack([~first, ~last]).astype(jnp.int32)


def _trunk(x_parts, n_prompt_seq, seg_len, p, tm, tm_moe, tc, nbq):
    n_parts = len(x_parts)
    x = list(x_parts)
    t = sum(a.shape[0] for a in x)
    depth = p['w_in'].shape[0]
    flags = _block_flags(n_prompt_seq, seg_len, t)
    bias = _band_bias(p['rel_bias'])
    zero_init = jnp.zeros((2, 1, N_SEG, SSM_REAL), F32)
    for layer in range(depth):
        ssm, qk, vt, sgu = _inproj(x, p['norm_mix'][layer][None], p['w_in'][layer].astype(BF16), 2 * tm)

        lam_bar, b_blk, a_rows, c_blk = _ssm_params(
            p['ssm_lam_re'][layer], p['ssm_lam_im'][layer], p['ssm_log_step'][layer], p['ssm_b_re'][layer],
            p['ssm_b_im'][layer], p['ssm_c_re'][layer], p['ssm_c_im'][layer])
        _, fin = _ssm_scan(ssm, seg_len, 1, 1, b_blk, a_rows, c_blk, zero_init, tc, False)
        init = jnp.concatenate([zero_init, _chain_init(fin[:, 0], lam_bar ** seg_len)[:, None]], axis=1)
        y, _ = _ssm_scan_pipelined(ssm, seg_len, 2, b_blk, a_rows, c_blk, init, 2 * tc)

        sink = jnp.repeat(p['attn_sink'][layer].astype(F32), BLOCK).reshape(N_KV_HEADS, 1, Q_PER_KV * BLOCK)
        sgb = jnp.repeat(p['sgu_b'][layer].astype(F32).T, SGU_GROUP_WIDTH, axis=1)
        attn, sguo = _attn_sgu(flags, qk, vt, sgu, bias, sink, p['sgu_norm'][layer][None].astype(F32),
                               p['sgu_w'][layer].astype(BF16), sgb, nbq)

        x1, h, idx, rank, wgt, cnt = _outproj_router(
            x, ssm, y, attn, sguo, p['ssm_d'][layer][None].astype(F32), p['out_norm'][layer][None],
            p['w_out'][layer].astype(BF16), p['norm_ffn'][layer][None], p['w_router'][layer].astype(F32),
            p['b_router'][layer][None].astype(F32), tm)

        items, tok_sorted, w_sorted, src0, cstart, nchunk, qpos = _routing_lists(
            idx, rank, wgt, cnt[:, :, 0], tm, tm_moe)
        y_sorted = _experts(items, tok_sorted, w_sorted, h, p['w_gate_up'].astype(F32),
                            p['b_gate_up'][:, :, None].astype(F32), p['w_down'].astype(F32),
                            p['b_down'][:, :, None].astype(F32), layer, tm_moe)
        last = layer == depth - 1
        x = _combine(src0, cstart, nchunk, y_sorted, x1, qpos, p['final_norm'][None].astype(F32), tm,
                     last, n_parts if last else 1)
    return x


def _run(x_prompt, x_sample, p, tm=512, tm_moe=512, tc=64, nbq=8):
    bsz, seg_len, _ = x_prompt.shape
    assert bsz == N_SEG and x_sample.shape[0] == 1 and x_sample.shape[1] == N_SEG * seg_len
    assert seg_len % (nbq * BLOCK) == 0 and seg_len % tc == 0 and seg_len % tm == 0
    parts = [x_prompt.reshape(-1, D_MODEL).astype(F32), x_sample.reshape(-1, D_MODEL).astype(F32)]
    y_prompt, y_sample = _trunk(parts, bsz, seg_len, p, tm, tm_moe, tc, nbq)
    return y_prompt.reshape(x_prompt.shape), y_sample.reshape(x_sample.shape)


def kernel(x_prompt, x_sample, norm_mix, w_in, ssm_lam_re, ssm_lam_im, ssm_log_step, ssm_b_re, ssm_b_im, ssm_c_re, ssm_c_im, ssm_d, attn_sink, rel_bias, sgu_norm, sgu_w, sgu_b, out_norm, w_out, norm_ffn, w_router, b_router, w_gate_up, b_gate_up, w_down, b_down, final_norm):
    p = dict(norm_mix=norm_mix, w_in=w_in, ssm_lam_re=ssm_lam_re, ssm_lam_im=ssm_lam_im, ssm_log_step=ssm_log_step,
             ssm_b_re=ssm_b_re, ssm_b_im=ssm_b_im, ssm_c_re=ssm_c_re, ssm_c_im=ssm_c_im, ssm_d=ssm_d,
             attn_sink=attn_sink, rel_bias=rel_bias, sgu_norm=sgu_norm, sgu_w=sgu_w, sgu_b=sgu_b, out_norm=out_norm,
             w_out=w_out, norm_ffn=norm_ffn, w_router=w_router, b_router=b_router, w_gate_up=w_gate_up,
             b_gate_up=b_gate_up, w_down=w_down, b_down=b_down, final_norm=final_norm)
    return _run(x_prompt, x_sample, p)
```

```python
import functools
import math

import jax
import jax.numpy as jnp
from jax import lax
from jax.experimental import pallas as pl
from jax.experimental.pallas import tpu as pltpu

D_MODEL = 1024
SSM_WIDTH = 256
SSM_GROUP = 16
SSM_GROUPS = 16
SSM_STATE = 64
SSM_COMPLEX = SSM_GROUPS * SSM_STATE
SSM_REAL = 2 * SSM_COMPLEX
ATTN_WIDTH = 512
HEAD_DIM = 64
N_Q_HEADS = 8
N_KV_HEADS = 2
Q_PER_KV = 4
KV_WIDTH = 128
BLOCK = 128
N_BUCKETS = 32
MAX_DISTANCE = 128
SGU_WIDTH = 256
SGU_GROUPS = 4
SGU_GROUP_WIDTH = 64
IN_COLS = 1792
QKV_COLS = ATTN_WIDTH + 2 * KV_WIDTH
N_EXPERTS = 32
TOP_K = 4
D_EXPERT = 1024
SWIGLU_LIMIT = 7.0
SWIGLU_ALPHA = 1.702
EPS = 1e-6
NEG_INF = -1e30

N_SEG = 8
V7X_VMEM_LIMIT = 56 * 1024 * 1024

F32 = jnp.float32
BF16 = jnp.bfloat16


def _rms(xf, g):
    return xf * lax.rsqrt(jnp.mean(xf * xf, axis=-1, keepdims=True) + EPS) * g


LANES = 128
TOKEN_TILE_ROWS = D_MODEL // LANES


def _load_token_tiles(ref, base, n):
    return jnp.concatenate(
        [ref[pl.ds(base + j, n, stride=TOKEN_TILE_ROWS), :] for j in range(TOKEN_TILE_ROWS)], axis=1)


def _store_token_tiles(ref, base, val, n):
    for j in range(TOKEN_TILE_ROWS):
        ref[pl.ds(base + j, n, stride=TOKEN_TILE_ROWS), :] = val[:, j * LANES:(j + 1) * LANES]


def _cparams(*sem):
    return pltpu.CompilerParams(dimension_semantics=sem, vmem_limit_bytes=V7X_VMEM_LIMIT)


def _part_specs(parts, tm):
    per = parts[0].shape[0] // tm
    return [pl.BlockSpec((tm, D_MODEL), (lambda j: lambda i, *_: (jnp.clip(i - j * per, 0, per - 1), 0))(j))
            for j in range(len(parts))]


def _pick_part(x_refs, steps_per_part):
    i = pl.program_id(0)
    x = x_refs[0][...]
    for j in range(1, len(x_refs)):
        x = jnp.where(i >= j * steps_per_part, x_refs[j][...], x)
    return x


def _inproj_kernel(*refs, n_parts, steps_per_part):
    x_refs, (g_ref, w_ref, ssm_ref, qk_ref, vt_ref, sgu_ref) = refs[:n_parts], refs[n_parts:]
    h = _rms(_pick_part(x_refs, steps_per_part), g_ref[...]).astype(BF16)
    p = jnp.dot(h, w_ref[...], preferred_element_type=F32)
    q0, k0, v0 = 2 * SSM_WIDTH, 2 * SSM_WIDTH + ATTN_WIDTH, 2 * SSM_WIDTH + ATTN_WIDTH + KV_WIDTH
    ssm_ref[...] = p[:, :q0].astype(BF16)
    qk_ref[...] = jnp.concatenate([p[:, q0:k0] * (HEAD_DIM ** -0.5), p[:, k0:v0]], axis=1).astype(BF16)
    vt_ref[...] = jnp.transpose(p[:, v0:v0 + KV_WIDTH]).astype(BF16)
    sgu_ref[...] = p[:, v0 + KV_WIDTH:].astype(BF16)


def _inproj(x_parts, g, w, tm):
    t = sum(x.shape[0] for x in x_parts)
    return pl.pallas_call(
        functools.partial(_inproj_kernel, n_parts=len(x_parts), steps_per_part=x_parts[0].shape[0] // tm),
        grid=(t // tm,),
        in_specs=_part_specs(x_parts, tm) + [
            pl.BlockSpec((1, D_MODEL), lambda i: (0, 0)),
            pl.BlockSpec((D_MODEL, IN_COLS), lambda i: (0, 0)),
        ],
        out_specs=[
            pl.BlockSpec((tm, 2 * SSM_WIDTH), lambda i: (i, 0)),
            pl.BlockSpec((tm, ATTN_WIDTH + KV_WIDTH), lambda i: (i, 0)),
            pl.BlockSpec((KV_WIDTH, tm), lambda i: (0, i)),
            pl.BlockSpec((tm, 2 * SGU_WIDTH), lambda i: (i, 0)),
        ],
        out_shape=[
            jax.ShapeDtypeStruct((t, 2 * SSM_WIDTH), BF16),
            jax.ShapeDtypeStruct((t, ATTN_WIDTH + KV_WIDTH), BF16),
            jax.ShapeDtypeStruct((KV_WIDTH, t), BF16),
            jax.ShapeDtypeStruct((t, 2 * SGU_WIDTH), BF16),
        ],
        compiler_params=_cparams("arbitrary"),
        name="inproj",
    )(*x_parts, g, w)


SCAN_COL_BLOCKS = 1


def _ssm_kernel(u_ref, b_ref, a_ref, c_ref, init_ref, *rest, tc, emit_y):
    if emit_y:
        y_ref, fin_ref, bu_scr, st_scr, tm_lo, tm_hi = rest
    else:
        fin_ref, bu_scr, st_scr, tm_lo, tm_hi = rest
    d = pl.program_id(0)
    c = pl.program_id(2)
    halves = ((tm_lo, slice(0, LANES)), (tm_hi, slice(LANES, 2 * LANES)))

    @pl.when(c == 0)
    def _():
        st_scr[...] = init_ref[0, 0]

    for s in range(N_SEG):
        u_s = u_ref[s].astype(F32)
        for scr, cols in halves:
            scr[pl.ds(s, tc, stride=N_SEG), :] = u_s[:, cols]
    u_tm = jnp.concatenate([tm_lo[...], tm_hi[...]], axis=1).astype(BF16)
    bu_scr[...] = jnp.dot(u_tm, b_ref[0], preferred_element_type=F32)
    w = SSM_COMPLEX // SCAN_COL_BLOCKS
    for cb in range(SCAN_COL_BLOCKS):
        re_cols = slice(cb * w, (cb + 1) * w)
        im_cols = slice(SSM_COMPLEX + cb * w, SSM_COMPLEX + (cb + 1) * w)
        a_re = jnp.broadcast_to(a_ref[0, 0:1, re_cols], (N_SEG, w))
        a_im = jnp.broadcast_to(a_ref[0, 1:2, re_cols], (N_SEG, w))

        def body(tt, carry, re_cols=re_cols, im_cols=im_cols, a_re=a_re, a_im=a_im):
            x_re, x_im = carry
            t = jnp.where(d == 0, tt, tc - 1 - tt)
            rows = pl.ds(pl.multiple_of(t * N_SEG, N_SEG), N_SEG)
            n_re = a_re * x_re - a_im * x_im + bu_scr[rows, re_cols]
            n_im = a_re * x_im + a_im * x_re + bu_scr[rows, im_cols]
            bu_scr[rows, re_cols] = n_re
            bu_scr[rows, im_cols] = n_im
            return n_re, n_im

        x_re, x_im = lax.fori_loop(0, tc, body, (st_scr[:, re_cols], st_scr[:, im_cols]), unroll=8)
        st_scr[:, re_cols] = x_re
        st_scr[:, im_cols] = x_im

    if emit_y:
        y = jnp.dot(bu_scr[...].astype(BF16), c_ref[...], preferred_element_type=F32)
        for scr, cols in halves:
            scr[...] = y[:, cols]
        for s in range(N_SEG):
            for scr, cols in halves:
                y_ref[0, s, :, cols] = scr[pl.ds(s, tc, stride=N_SEG), :]

    @pl.when(c == pl.num_programs(2) - 1)
    def _():
        fin_ref[0, 0] = st_scr[...]


def _ssm_pipe_kernel(u_ref, b_ref, a_ref, c_ref, init_ref, y_ref, fin_ref, bu0, bu1, bu2, st_scr, ut_lo, ut_hi,
                     yt_lo, yt_hi, *, tc, nc):
    d = pl.program_id(0)
    s = pl.program_id(2)
    bufs = (bu0, bu1, bu2)
    u_halves = ((ut_lo, slice(0, LANES)), (ut_hi, slice(LANES, 2 * LANES)))
    y_halves = ((yt_lo, slice(0, LANES)), (yt_hi, slice(LANES, 2 * LANES)))

    @pl.when(s == 0)
    def _():
        st_scr[...] = init_ref[0, 0]
        for buf in bufs:
            buf[...] = jnp.zeros_like(buf)

    def projections(b_buf, c_buf):
        for sg in range(N_SEG):
            u_s = u_ref[sg].astype(F32)
            for scr, cols in u_halves:
                scr[pl.ds(sg, tc, stride=N_SEG), :] = u_s[:, cols]
        u_tm = jnp.concatenate([ut_lo[...], ut_hi[...]], axis=1).astype(BF16)
        y = jnp.dot(c_buf[...].astype(BF16), c_ref[...], preferred_element_type=F32)
        b_buf[...] = jnp.dot(u_tm, b_ref[0], preferred_element_type=F32)
        for scr, cols in y_halves:
            scr[...] = y[:, cols]
        for sg in range(N_SEG):
            for scr, cols in y_halves:
                y_ref[0, sg, :, cols] = scr[pl.ds(sg, tc, stride=N_SEG), :]

    def recurrence(buf):
        w = SSM_COMPLEX // SCAN_COL_BLOCKS
        for cb in range(SCAN_COL_BLOCKS):
            re_cols = slice(cb * w, (cb + 1) * w)
            im_cols = slice(SSM_COMPLEX + cb * w, SSM_COMPLEX + (cb + 1) * w)
            a_re = jnp.broadcast_to(a_ref[0, 0:1, re_cols], (N_SEG, w))
            a_im = jnp.broadcast_to(a_ref[0, 1:2, re_cols], (N_SEG, w))

            def body(tt, carry, re_cols=re_cols, im_cols=im_cols, a_re=a_re, a_im=a_im):
                x_re, x_im = carry
                t = jnp.where(d == 0, tt, tc - 1 - tt)
                rows = pl.ds(pl.multiple_of(t * N_SEG, N_SEG), N_SEG)
                n_re = a_re * x_re - a_im * x_im + buf[rows, re_cols]
                n_im = a_re * x_im + a_im * x_re + buf[rows, im_cols]
                buf[rows, re_cols] = n_re
                buf[rows, im_cols] = n_im
                return n_re, n_im

            x_re, x_im = lax.fori_loop(0, tc, body, (st_scr[:, re_cols], st_scr[:, im_cols]), unroll=8)
            st_scr[:, re_cols] = x_re
            st_scr[:, im_cols] = x_im

    turn = lax.rem(s, 3)
    for k in range(3):
        @pl.when(turn == k)
        def _(k=k):
            projections(bufs[k], bufs[(k + 1) % 3])

    for k in range(3):
        @pl.when((turn == k) & (s >= 1) & (s <= nc))
        def _(k=k):
            recurrence(bufs[(k + 2) % 3])

    @pl.when(s == pl.num_programs(2) - 1)
    def _():
        fin_ref[0, 0] = st_scr[...]


def _ssm_scan_pipelined(ssm_proj, seg_len, n_groups, b_blk, a_rows, c_blk, init, tc):
    u = ssm_proj.reshape(-1, seg_len, 2 * SSM_WIDTH)
    nc = seg_len // tc

    def chunk(d, c):
        c = jnp.clip(c, 0, nc - 1)
        return jnp.where(d == 0, c, nc - 1 - c)

    rows = tc * N_SEG
    y, fin = pl.pallas_call(
        functools.partial(_ssm_pipe_kernel, tc=tc, nc=nc),
        grid=(2, n_groups, nc + 2),
        in_specs=[
            pl.BlockSpec((N_SEG, tc, SSM_WIDTH), lambda d, g, s: (g, chunk(d, s), 0)),
            pl.BlockSpec((1, SSM_WIDTH, SSM_REAL), lambda d, g, s: (d, 0, 0)),
            pl.BlockSpec((1, 2, SSM_COMPLEX), lambda d, g, s: (d, 0, 0)),
            pl.BlockSpec((SSM_REAL, SSM_WIDTH), lambda d, g, s: (0, 0)),
            pl.BlockSpec((1, 1, N_SEG, SSM_REAL), lambda d, g, s: (d, g, 0, 0)),
        ],
        out_specs=[
            pl.BlockSpec((1, N_SEG, tc, SSM_WIDTH), lambda d, g, s: (d, g, chunk(d, s - 2), 0)),
            pl.BlockSpec((1, 1, N_SEG, SSM_REAL), lambda d, g, s: (d, g, 0, 0)),
        ],
        out_shape=[
            jax.ShapeDtypeStruct((2, n_groups * N_SEG, seg_len, SSM_WIDTH), F32),
            jax.ShapeDtypeStruct((2, n_groups, N_SEG, SSM_REAL), F32),
        ],
        scratch_shapes=[pltpu.VMEM((rows, SSM_REAL), F32)] * 3 + [pltpu.VMEM((N_SEG, SSM_REAL), F32)]
        + [pltpu.VMEM((rows, LANES), F32)] * 4,
        compiler_params=_cparams("arbitrary", "arbitrary", "arbitrary"),
        name="ssm_scan_y",
    )(u, b_blk, a_rows, c_blk, init)
    return y.reshape(2, -1, SSM_WIDTH), fin


def _ssm_scan(ssm_proj, seg_len, first_group, n_groups, b_blk, a_rows, c_blk, init, tc, emit_y):
    u = ssm_proj.reshape(-1, seg_len, 2 * SSM_WIDTH)
    nc = seg_len // tc

    def chunk(d, c):
        return jnp.where(d == 0, c, nc - 1 - c)

    out_specs = [pl.BlockSpec((1, 1, N_SEG, SSM_REAL), lambda d, g, c: (d, g, 0, 0))]
    out_shape = [jax.ShapeDtypeStruct((2, n_groups, N_SEG, SSM_REAL), F32)]
    if emit_y:
        out_specs = [pl.BlockSpec((1, N_SEG, tc, SSM_WIDTH), lambda d, g, c: (d, g, chunk(d, c), 0))] + out_specs
        out_shape = [jax.ShapeDtypeStruct((2, n_groups * N_SEG, seg_len, SSM_WIDTH), F32)] + out_shape
    res = pl.pallas_call(
        functools.partial(_ssm_kernel, tc=tc, emit_y=emit_y),
        grid=(2, n_groups, nc),
        in_specs=[
            pl.BlockSpec((N_SEG, tc, SSM_WIDTH), lambda d, g, c: (first_group + g, chunk(d, c), 0)),
            pl.BlockSpec((1, SSM_WIDTH, SSM_REAL), lambda d, g, c: (d, 0, 0)),
            pl.BlockSpec((1, 2, SSM_COMPLEX), lambda d, g, c: (d, 0, 0)),
            pl.BlockSpec((SSM_REAL, SSM_WIDTH), lambda d, g, c: (0, 0)),
            pl.BlockSpec((1, 1, N_SEG, SSM_REAL), lambda d, g, c: (d, g, 0, 0)),
        ],
        out_specs=out_specs,
        out_shape=out_shape,
        scratch_shapes=[pltpu.VMEM((tc * N_SEG, SSM_REAL), F32), pltpu.VMEM((N_SEG, SSM_REAL), F32),
                        pltpu.VMEM((tc * N_SEG, LANES), F32), pltpu.VMEM((tc * N_SEG, LANES), F32)],
        compiler_params=_cparams("arbitrary", "arbitrary", "arbitrary"),
        name="ssm_scan_y" if emit_y else "ssm_scan_state",
    )(u, b_blk, a_rows, c_blk, init)
    if emit_y:
        return res[0].reshape(2, -1, SSM_WIDTH), res[1]
    return None, res[0]


def _ssm_params(lam_re, lam_im, log_step, b_re, b_im, c_re, c_im):
    lam = lax.complex(lam_re.astype(F32), lam_im.astype(F32))
    step = jnp.exp(log_step.astype(F32))[..., None]
    lam_bar = jnp.exp(lam * step)
    b = lax.complex(b_re.astype(F32), b_im.astype(F32))
    b_bar = ((lam_bar - 1.0) / lam)[..., None] * b
    eye = jnp.eye(SSM_GROUPS, dtype=F32)
    b_blk_re = jnp.einsum('dgph,gk->dghkp', jnp.real(b_bar), eye).reshape(2, SSM_WIDTH, SSM_COMPLEX)
    b_blk_im = jnp.einsum('dgph,gk->dghkp', jnp.imag(b_bar), eye).reshape(2, SSM_WIDTH, SSM_COMPLEX)
    b_blk = jnp.concatenate([b_blk_re, b_blk_im], axis=-1).astype(BF16)
    c_blk_re = jnp.einsum('ghp,gk->gpkh', c_re.astype(F32), eye).reshape(SSM_COMPLEX, SSM_WIDTH)
    c_blk_im = jnp.einsum('ghp,gk->gpkh', c_im.astype(F32), eye).reshape(SSM_COMPLEX, SSM_WIDTH)
    c_blk = jnp.concatenate([c_blk_re, -c_blk_im], axis=0).astype(BF16)
    a_rows = jnp.stack([jnp.real(lam_bar).reshape(2, SSM_COMPLEX),
                        jnp.imag(lam_bar).reshape(2, SSM_COMPLEX)], axis=1)
    return lam_bar.reshape(2, SSM_COMPLEX), b_blk, a_rows, c_blk


def _chain_init(fin, lam_pow):
    f = lax.complex(fin[..., :SSM_COMPLEX], fin[..., SSM_COMPLEX:])
    zero = jnp.zeros_like(f[0, 0])
    fwd = [zero]
    for s in range(1, N_SEG):
        fwd.append(lam_pow[0] * fwd[-1] + f[0, s - 1])
    bwd = [zero]
    for s in range(N_SEG - 2, -1, -1):
        bwd.append(lam_pow[1] * bwd[-1] + f[1, s + 1])
    init = jnp.stack([jnp.stack(fwd), jnp.stack(bwd[::-1])])
    return jnp.concatenate([jnp.real(init), jnp.imag(init)], axis=-1).astype(F32)


def _attn_sgu_kernel(flags_ref, q_ref, kp_ref, kc_ref, kn_ref, vp_ref, vc_ref, vn_ref, bias_ref, sink_ref,
                     sgu_ref, sgn_ref, sgw_ref, sgb_ref, attn_ref, sguo_ref, *, nbq):
    n = pl.program_id(0)
    no_prev = flags_ref[0, n * nbq] == 0
    no_next = flags_ref[1, n * nbq + nbq - 1] == 0

    def k_block(i):
        if i == 0:
            return kp_ref[...]
        if i == nbq + 1:
            return kn_ref[...]
        return kc_ref[(i - 1) * BLOCK:i * BLOCK, :]

    def vt_block(i):
        if i == 0:
            return vp_ref[...]
        if i == nbq + 1:
            return vn_ref[...]
        return vc_ref[:, (i - 1) * BLOCK:i * BLOCK]

    for b in range(nbq):
        rows = slice(b * BLOCK, (b + 1) * BLOCK)
        q = q_ref[rows, :]
        outs = []
        for j in range(N_KV_HEADS):
            qs = jnp.concatenate(
                [q[:, (j * Q_PER_KV + i) * HEAD_DIM:(j * Q_PER_KV + i + 1) * HEAD_DIM] for i in range(Q_PER_KV)],
                axis=0)
            kv = slice(j * HEAD_DIM, (j + 1) * HEAD_DIM)
            s = []
            for i in range(3):
                si = lax.dot_general(k_block(b + i)[:, kv], qs, (((1,), (1,)), ((), ())),
                                     preferred_element_type=F32)
                si = si + bias_ref[j, i * BLOCK:(i + 1) * BLOCK, :]
                if b == 0 and i == 0:
                    si = jnp.where(no_prev, NEG_INF, si)
                if b == nbq - 1 and i == 2:
                    si = jnp.where(no_next, NEG_INF, si)
                s.append(si)
            sink = sink_ref[j]
            m = jnp.max(jnp.maximum(jnp.maximum(s[0], s[1]), s[2]), axis=0, keepdims=True)
            m = jnp.maximum(m, sink)
            p = [jnp.exp(si - m) for si in s]
            denom = jnp.sum(p[0] + p[1] + p[2], axis=0, keepdims=True) + jnp.exp(sink - m)
            inv = 1.0 / denom
            o_t = sum(jnp.dot(vt_block(b + i)[kv, :], p[i].astype(BF16), preferred_element_type=F32)
                      for i in range(3)) * inv
            outs += [jnp.transpose(o_t[:, i * BLOCK:(i + 1) * BLOCK]) for i in range(Q_PER_KV)]
        attn_ref[rows, :] = jnp.concatenate(outs, axis=1).astype(BF16)

        gu = jax.nn.gelu(sgu_ref[rows, :SGU_WIDTH].astype(F32))
        gv = _rms(jax.nn.gelu(sgu_ref[rows, SGU_WIDTH:].astype(F32)), sgn_ref[...]).astype(BF16)
        mixed = jnp.concatenate(
            [jnp.dot(sgw_ref[g], gv[:, g * SGU_GROUP_WIDTH:(g + 1) * SGU_GROUP_WIDTH], preferred_element_type=F32)
             for g in range(SGU_GROUPS)], axis=1) + sgb_ref[...]
        sguo_ref[rows, :] = (gu * mixed).astype(BF16)


def _attn_sgu(flags, qk, vt, sgu, bias, sink, sgn, sgw, sgb, nbq):
    t = qk.shape[0]
    nb = t // BLOCK
    tile = nbq * BLOCK
    kcol = ATTN_WIDTH // KV_WIDTH
    before = lambda n: jnp.maximum(n * nbq - 1, 0)
    after = lambda n: jnp.minimum((n + 1) * nbq, nb - 1)
    grid_spec = pltpu.PrefetchScalarGridSpec(
        num_scalar_prefetch=1,
        grid=(nb // nbq,),
        in_specs=[
            pl.BlockSpec((tile, ATTN_WIDTH), lambda n, f: (n, 0)),
            pl.BlockSpec((BLOCK, KV_WIDTH), lambda n, f: (before(n), kcol)),
            pl.BlockSpec((tile, KV_WIDTH), lambda n, f: (n, kcol)),
            pl.BlockSpec((BLOCK, KV_WIDTH), lambda n, f: (after(n), kcol)),
            pl.BlockSpec((KV_WIDTH, BLOCK), lambda n, f: (0, before(n))),
            pl.BlockSpec((KV_WIDTH, tile), lambda n, f: (0, n)),
            pl.BlockSpec((KV_WIDTH, BLOCK), lambda n, f: (0, after(n))),
            pl.BlockSpec((N_KV_HEADS, 3 * BLOCK, Q_PER_KV * BLOCK), lambda n, f: (0, 0, 0)),
            pl.BlockSpec((N_KV_HEADS, 1, Q_PER_KV * BLOCK), lambda n, f: (0, 0, 0)),
            pl.BlockSpec((tile, 2 * SGU_WIDTH), lambda n, f: (n, 0)),
            pl.BlockSpec((1, SGU_WIDTH), lambda n, f: (0, 0)),
            pl.BlockSpec((SGU_GROUPS, BLOCK, BLOCK), lambda n, f: (0, 0, 0)),
            pl.BlockSpec((BLOCK, SGU_WIDTH), lambda n, f: (0, 0)),
        ],
        out_specs=[
            pl.BlockSpec((tile, ATTN_WIDTH), lambda n, f: (n, 0)),
            pl.BlockSpec((tile, SGU_WIDTH), lambda n, f: (n, 0)),
        ],
    )
    return pl.pallas_call(
        functools.partial(_attn_sgu_kernel, nbq=nbq),
        grid_spec=grid_spec,
        out_shape=[jax.ShapeDtypeStruct((t, ATTN_WIDTH), BF16), jax.ShapeDtypeStruct((t, SGU_WIDTH), BF16)],
        compiler_params=_cparams("arbitrary"),
        name="attn_sgu",
    )(flags, qk, qk, qk, qk, vt, vt, vt, bias, sink, sgu, sgn, sgw, sgb)


def _t5_bucket(rel):
    half = N_BUCKETS // 2
    max_exact = half // 2
    ret = jnp.where(rel > 0, half, 0)
    n = jnp.abs(rel)
    nf = jnp.maximum(n, 1).astype(F32)
    large = max_exact + (jnp.log(nf / max_exact) / math.log(MAX_DISTANCE / max_exact)
                         * (half - max_exact)).astype(jnp.int32)
    large = jnp.minimum(large, half - 1)
    return ret + jnp.where(n < max_exact, n, large)


def _band_bias(rel_table):
    q_pos = jnp.arange(BLOCK)[:, None]
    k_pos = jnp.arange(3 * BLOCK)[None, :] - BLOCK
    rel = k_pos - q_pos
    hit = _t5_bucket(rel)[..., None] == jnp.arange(N_BUCKETS)
    bias = jnp.sum(jnp.where(hit[..., None], rel_table.astype(F32), 0.0), axis=2)
    bias = jnp.where((jnp.abs(rel) <= BLOCK)[..., None], bias, NEG_INF)
    bias = bias.reshape(BLOCK, 3 * BLOCK, N_KV_HEADS, Q_PER_KV)
    return bias.transpose(2, 1, 3, 0).reshape(N_KV_HEADS, 3 * BLOCK, Q_PER_KV * BLOCK)


def _outproj_router_kernel(*refs, tm, n_parts, steps_per_part):
    x_refs = refs[:n_parts]
    (ssm_ref, yf_ref, yb_ref, attn_ref, sguo_ref, d_ref, gout_ref, wout_ref, gffn_ref, wrh_ref, wrl_ref, br_ref,
     earlier_ref, x1_ref, h_ref, idx_ref, rank_ref, wgt_ref, cnt_ref) = refs[n_parts:]
    u = ssm_ref[:, :SSM_WIDTH].astype(F32)
    gate = ssm_ref[:, SSM_WIDTH:].astype(F32)
    ya = yf_ref[...] + yb_ref[...] + d_ref[...] * u
    ya = jax.nn.gelu(ya) * jax.nn.sigmoid(gate)
    g = gout_ref[...]
    mixed = jnp.concatenate([
        _rms(ya, g[:, :SSM_WIDTH]),
        _rms(attn_ref[...].astype(F32), g[:, SSM_WIDTH:SSM_WIDTH + ATTN_WIDTH]),
        _rms(sguo_ref[...].astype(F32), g[:, SSM_WIDTH + ATTN_WIDTH:]),
    ], axis=1).astype(BF16)
    x1 = _pick_part(x_refs, steps_per_part) + jnp.dot(mixed, wout_ref[...], preferred_element_type=F32)
    x1_ref[...] = x1
    h = _rms(x1, gffn_ref[...])
    _store_token_tiles(h_ref, 0, h, tm)

    h_hi = h.astype(BF16)
    h_lo = (h - h_hi.astype(F32)).astype(BF16)
    nt_dot = lambda a, b: lax.dot_general(a, b, (((1,), (1,)), ((), ())), preferred_element_type=F32)
    lg = nt_dot(wrh_ref[...], h_hi) + nt_dot(wrl_ref[...], h_hi) + nt_dot(wrh_ref[...], h_lo) + br_ref[...]
    row = lax.broadcasted_iota(jnp.int32, (N_EXPERTS, tm), 0)
    vals, idxs, sels = [], [], []
    for _ in range(TOP_K):
        m = jnp.max(lg, axis=0, keepdims=True)
        idx = jnp.min(jnp.where(lg == m, row, N_EXPERTS), axis=0, keepdims=True)
        sel = row == idx
        lg = jnp.where(sel, -jnp.inf, lg)
        vals.append(m)
        idxs.append(idx)
        sels.append(sel)
    e = [jnp.exp(v - vals[0]) for v in vals]
    tot = e[0] + e[1] + e[2] + e[3]
    wgt_ref[...] = jnp.concatenate([ek / tot for ek in e], axis=0)
    idx_ref[...] = jnp.concatenate(idxs, axis=0)

    picked = (sels[0] | sels[1] | sels[2] | sels[3])
    pm = jnp.where(picked, 1.0, 0.0)
    before = jnp.dot(pm.astype(BF16), earlier_ref[...], preferred_element_type=F32)
    rank_ref[...] = jnp.concatenate(
        [jnp.sum(jnp.where(s, before, 0.0), axis=0, keepdims=True) for s in sels], axis=0).astype(jnp.int32)
    cnt_ref[0] = jnp.sum(pm, axis=1, keepdims=True).astype(jnp.int32)


def _outproj_router(x_parts, ssm, y_dirs, attn, sguo, d_skip, g_out, w_out, g_ffn, w_r, b_r, tm):
    t = ssm.shape[0]
    row = lambda w: pl.BlockSpec((tm, w), lambda i: (i, 0))
    full = lambda a, b: pl.BlockSpec((a, b), lambda i: (0, 0))
    direction = lambda d: pl.BlockSpec((None, tm, SSM_WIDTH), lambda i: (d, i, 0))
    per_choice = pl.BlockSpec((TOP_K, tm), lambda i: (0, i))
    w_rt = w_r.astype(F32).T
    w_hi = w_rt.astype(BF16)
    w_lo = (w_rt - w_hi.astype(F32)).astype(BF16)
    earlier = (jnp.arange(tm)[:, None] < jnp.arange(tm)[None, :]).astype(BF16)
    return pl.pallas_call(
        functools.partial(_outproj_router_kernel, tm=tm, n_parts=len(x_parts),
                          steps_per_part=x_parts[0].shape[0] // tm),
        grid=(t // tm,),
        in_specs=_part_specs(x_parts, tm) + [
            row(2 * SSM_WIDTH), direction(0), direction(1), row(ATTN_WIDTH), row(SGU_WIDTH),
            full(1, SSM_WIDTH), full(1, D_MODEL), full(D_MODEL, D_MODEL), full(1, D_MODEL),
            full(N_EXPERTS, D_MODEL), full(N_EXPERTS, D_MODEL), full(N_EXPERTS, 1), full(tm, tm),
        ],
        out_specs=[row(D_MODEL), pl.BlockSpec((tm * TOKEN_TILE_ROWS, LANES), lambda i: (i, 0)),
                   per_choice, per_choice, per_choice, pl.BlockSpec((1, N_EXPERTS, 1), lambda i: (i, 0, 0))],
        out_shape=[
            jax.ShapeDtypeStruct((t, D_MODEL), F32),
            jax.ShapeDtypeStruct((t * TOKEN_TILE_ROWS, LANES), F32),
            jax.ShapeDtypeStruct((TOP_K, t), jnp.int32),
            jax.ShapeDtypeStruct((TOP_K, t), jnp.int32),
            jax.ShapeDtypeStruct((TOP_K, t), F32),
            jax.ShapeDtypeStruct((t // tm, N_EXPERTS, 1), jnp.int32),
        ],
        compiler_params=_cparams("arbitrary"),
        name="outproj_router",
    )(*x_parts, ssm, y_dirs, y_dirs, attn, sguo, d_skip, g_out, w_out, g_ffn, w_hi, w_lo,
      b_r.astype(F32).reshape(N_EXPERTS, 1), earlier)


ISSUE_UNROLL = 8
FETCH_ROWS = 16


def _for_rows(n, fn):
    def body(c, carry):
        for u in range(ISSUE_UNROLL):
            fn(c * ISSUE_UNROLL + u, u % 2)
        return carry

    lax.fori_loop(0, n // ISSUE_UNROLL, body, 0)


def _tile_rows(ref, first_tile, n_tiles=1):
    start = first_tile * TOKEN_TILE_ROWS
    if not isinstance(start, int):
        start = pl.multiple_of(start, TOKEN_TILE_ROWS)
    return ref.at[pl.ds(start, n_tiles * TOKEN_TILE_ROWS)]


EXPERT_COL_CHUNK = 1024


GATHER_SLOTS = 3


def _experts_kernel(ltile_ref, otile_ref, exp_ref, lo_ref, hi_ref, tok_ref, tok1_ref, tok2_ref, wrow_ref, h_hbm, wgu_ref,
                    bgu_ref, wd_ref, bd_ref, out_ref, x_buf, y_buf, wgu_bf, wd_bf, gsem, *, tm):
    w = pl.program_id(0)
    nw = pl.num_programs(0)
    slot = lax.rem(w, GATHER_SLOTS)
    ahead = lax.rem(w + GATHER_SLOTS - 1, GATHER_SLOTS)
    prev = jnp.maximum(w - 1, 0)

    def gather_row(list_ref, to_slot, r, queue):
        pltpu.make_async_copy(_tile_rows(h_hbm, list_ref[0, 0, r]), _tile_rows(x_buf, to_slot * tm + r),
                              gsem.at[to_slot]).start(priority=queue)

    def wait_gather(of_slot):
        pltpu.make_async_copy(_tile_rows(h_hbm, 0, tm), _tile_rows(x_buf, of_slot * tm, tm), gsem.at[of_slot]).wait()

    @pl.when(w == 0)
    def _():
        _for_rows(tm, lambda r, queue: gather_row(tok_ref, 0, r, queue))
        _for_rows(tm, lambda r, queue: gather_row(tok1_ref, 1, r, queue))

    @pl.when((w == 0) | (exp_ref[w] != exp_ref[prev]))
    def _():
        wgu_bf[...] = wgu_ref[0].astype(BF16)
        wd_bf[...] = wd_ref[0].astype(BF16)

    wait_gather(slot)

    n_pieces = (D_EXPERT + D_MODEL) // EXPERT_COL_CHUNK
    per_piece = tm // n_pieces
    piece = 0

    def issue_next_gather():
        nonlocal piece
        for r in range(piece * per_piece, (piece + 1) * per_piece):
            gather_row(tok2_ref, ahead, r, r % 2)
        piece += 1

    w_rows = jnp.transpose(jnp.broadcast_to(wrow_ref[0], (LANES, tm)))
    w_rows = jnp.concatenate([w_rows] * (EXPERT_COL_CHUNK // LANES), axis=1)

    base = pl.multiple_of(slot * (tm * TOKEN_TILE_ROWS), tm * TOKEN_TILE_ROWS)
    x = _load_token_tiles(x_buf, base, tm).astype(BF16)
    acts = []
    for c in range(D_EXPERT // EXPERT_COL_CHUNK):
        issue_next_gather()
        g_cols = slice(c * EXPERT_COL_CHUNK, (c + 1) * EXPERT_COL_CHUNK)
        u_cols = slice(D_EXPERT + c * EXPERT_COL_CHUNK, D_EXPERT + (c + 1) * EXPERT_COL_CHUNK)
        gate = jnp.dot(x, wgu_bf[:, g_cols], preferred_element_type=F32) + bgu_ref[0, :, g_cols]
        up = jnp.dot(x, wgu_bf[:, u_cols], preferred_element_type=F32) + bgu_ref[0, :, u_cols]
        gate = jnp.minimum(gate, SWIGLU_LIMIT)
        up = jnp.clip(up, -SWIGLU_LIMIT, SWIGLU_LIMIT)
        acts.append(((up + 1.0) * gate * jax.nn.sigmoid(SWIGLU_ALPHA * gate)).astype(BF16))
    act = jnp.concatenate(acts, axis=1)
    tiles_per_chunk = EXPERT_COL_CHUNK // LANES
    for c in range(D_MODEL // EXPERT_COL_CHUNK):
        issue_next_gather()
        cols = slice(c * EXPERT_COL_CHUNK, (c + 1) * EXPERT_COL_CHUNK)
        y = (jnp.dot(act, wd_bf[:, cols], preferred_element_type=F32) + bd_ref[0, :, cols]) * w_rows
        for j in range(tiles_per_chunk):
            y_buf[pl.ds(c * tiles_per_chunk + j, tm, stride=TOKEN_TILE_ROWS), :] = y[:, j * LANES:(j + 1) * LANES]

    first = (w == 0) | (otile_ref[w] != otile_ref[prev])

    @pl.when(first)
    def _():
        out_ref[...] = y_buf[...]

    @pl.when(jnp.logical_not(first))
    def _():
        row = lax.shift_right_logical(lax.broadcasted_iota(jnp.int32, y_buf.shape, 0),
                                      TOKEN_TILE_ROWS.bit_length() - 1)
        mine = (row >= lo_ref[w]) & (row < hi_ref[w])
        out_ref[...] = jnp.where(mine, y_buf[...], out_ref[...])

    @pl.when(w == nw - 1)
    def _():
        for k in range(1, GATHER_SLOTS):
            wait_gather(lax.rem(w + k, GATHER_SLOTS))


def _experts(items, tok_sorted, w_sorted, h_tiles, wgu, bgu, wd, bd, layer, tm):
    nw = items[0].shape[0]
    n_rows = tok_sorted.shape[0]
    nt = n_rows // tm
    lists = lambda a: a.reshape(nt, 1, tm)
    later = lambda k: (lambda i, lt, ot, ex, lo, hi: (lt[jnp.minimum(i + k, nw - 1)], 0, 0))
    by_expert = lambda i, lt, ot, ex, lo, hi: (layer, ex[i], 0, 0)
    grid_spec = pltpu.PrefetchScalarGridSpec(
        num_scalar_prefetch=5,
        grid=(nw,),
        in_specs=[
            pl.BlockSpec((1, 1, tm), later(0), memory_space=pltpu.SMEM),
            pl.BlockSpec((1, 1, tm), later(1), memory_space=pltpu.SMEM),
            pl.BlockSpec((1, 1, tm), later(2), memory_space=pltpu.SMEM),
            pl.BlockSpec((1, 1, tm), later(0)),
            pl.BlockSpec(memory_space=pl.ANY),
            pl.BlockSpec((None, 1, D_MODEL, 2 * D_EXPERT), by_expert),
            pl.BlockSpec((None, 1, 1, 2 * D_EXPERT), by_expert),
            pl.BlockSpec((None, 1, D_EXPERT, D_MODEL), by_expert),
            pl.BlockSpec((None, 1, 1, D_MODEL), by_expert),
        ],
        out_specs=pl.BlockSpec((tm * TOKEN_TILE_ROWS, LANES), lambda i, lt, ot, ex, lo, hi: (ot[i], 0)),
        scratch_shapes=[
            pltpu.VMEM((GATHER_SLOTS * tm * TOKEN_TILE_ROWS, LANES), F32),
            pltpu.VMEM((tm * TOKEN_TILE_ROWS, LANES), F32),
            pltpu.VMEM((D_MODEL, 2 * D_EXPERT), BF16),
            pltpu.VMEM((D_EXPERT, D_MODEL), BF16),
            pltpu.SemaphoreType.DMA((GATHER_SLOTS,)),
        ],
    )
    return pl.pallas_call(
        functools.partial(_experts_kernel, tm=tm),
        grid_spec=grid_spec,
        out_shape=jax.ShapeDtypeStruct(((n_rows + FETCH_ROWS) * TOKEN_TILE_ROWS, LANES), F32),
        compiler_params=_cparams("arbitrary"),
        name="experts",
    )(*items, lists(tok_sorted), lists(tok_sorted), lists(tok_sorted), lists(w_sorted), h_tiles, wgu, bgu, wd, bd)


def _combine_kernel(src_ref, cst_ref, nch_ref, y_hbm, x1_ref, qpos_ref, g_ref, *rest, tm, nq, final, n_out):
    out_refs, (yc_buf, sem) = rest[:n_out], rest[n_out:]
    i = pl.program_id(0)
    slot = lax.rem(i, 2)

    def chunk_copy(src_row, dst_row, at_slot):
        return pltpu.make_async_copy(_tile_rows(y_hbm, src_row, FETCH_ROWS),
                                     _tile_rows(yc_buf, at_slot * nq + dst_row, FETCH_ROWS), sem.at[at_slot])

    def fetch_tile(tile, to_slot):
        for e in range(N_EXPERTS):
            src0 = src_ref[tile * N_EXPERTS + e]
            dst0 = cst_ref[tile * N_EXPERTS + e]

            def fetch(j, carry, src0=src0, dst0=dst0):
                chunk_copy(src0 + j * FETCH_ROWS, dst0 + j * FETCH_ROWS, to_slot).start()
                return carry

            lax.fori_loop(0, nch_ref[tile * N_EXPERTS + e], fetch, 0)

    @pl.when(i == 0)
    def _():
        yc_buf[...] = jnp.zeros_like(yc_buf)
        fetch_tile(0, 0)

    @pl.when(i + 1 < pl.num_programs(0))
    def _():
        fetch_tile(i + 1, 1 - slot)

    total = nch_ref[i * N_EXPERTS]
    for e in range(1, N_EXPERTS):
        total = total + nch_ref[i * N_EXPERTS + e]

    def wait(j, carry):
        chunk_copy(0, 0, slot).wait()
        return carry

    lax.fori_loop(0, total, wait, 0)

    q = lax.broadcasted_iota(jnp.int32, (tm, nq), 1)
    qpos = qpos_ref[...]
    hit = q == qpos[:, 0:1]
    for k in range(1, TOP_K):
        hit = hit | (q == qpos[:, k:k + 1])
    p = jnp.where(hit, 1.0, 0.0).astype(BF16)
    base = pl.multiple_of(slot * (nq * TOKEN_TILE_ROWS), nq * TOKEN_TILE_ROWS)
    yc = _load_token_tiles(yc_buf, base, nq).astype(BF16)
    x2 = x1_ref[...] + jnp.dot(p, yc, preferred_element_type=F32)
    if final:
        x2 = _rms(x2, g_ref[...])
    steps_per_out = pl.num_programs(0) // n_out
    for j, out_ref in enumerate(out_refs):
        @pl.when((i >= j * steps_per_out) & (i < (j + 1) * steps_per_out))
        def _(out_ref=out_ref):
            out_ref[...] = x2


def _combine(src0, cstart, nchunk, y_sorted, x1, qpos, g_final, tm, final, n_out):
    t = x1.shape[0]
    steps_per_out = t // tm // n_out
    nq = -(-(tm * TOP_K + N_EXPERTS * (FETCH_ROWS - 1)) // LANES) * LANES
    grid_spec = pltpu.PrefetchScalarGridSpec(
        num_scalar_prefetch=3,
        grid=(t // tm,),
        in_specs=[
            pl.BlockSpec(memory_space=pl.ANY),
            pl.BlockSpec((tm, D_MODEL), lambda i, a, b, c: (i, 0)),
            pl.BlockSpec((tm, TOP_K), lambda i, a, b, c: (i, 0)),
            pl.BlockSpec((1, D_MODEL), lambda i, a, b, c: (0, 0)),
        ],
        out_specs=[pl.BlockSpec((tm, D_MODEL), (lambda j: lambda i, a, b, c: (
            jnp.clip(i - j * steps_per_out, 0, steps_per_out - 1), 0))(j)) for j in range(n_out)],
        scratch_shapes=[pltpu.VMEM((2 * nq * TOKEN_TILE_ROWS, LANES), F32), pltpu.SemaphoreType.DMA((2,))],
    )
    return pl.pallas_call(
        functools.partial(_combine_kernel, tm=tm, nq=nq, final=final, n_out=n_out),
        grid_spec=grid_spec,
        out_shape=[jax.ShapeDtypeStruct((t // n_out, D_MODEL), F32)] * n_out,
        compiler_params=_cparams("arbitrary"),
        name="combine",
    )(src0.reshape(-1), cstart.reshape(-1), nchunk.reshape(-1), y_sorted, x1, qpos, g_final)


def _routing_lists(idx, lrank, wgt, tcnt, tm_tok, tm):
    t = idx.shape[1]
    n_rows = t * TOP_K
    nt = n_rows // tm
    cnt = jnp.sum(tcnt, axis=0)
    group_end = jnp.cumsum(cnt)
    group_start = group_end - cnt
    src0 = group_start[None, :] + jnp.cumsum(tcnt, axis=0) - tcnt
    aligned = -(-tcnt // FETCH_ROWS) * FETCH_ROWS
    cstart = jnp.cumsum(aligned, axis=1) - aligned
    nchunk = aligned // FETCH_ROWS
    experts = jnp.arange(N_EXPERTS, dtype=jnp.int32)
    onehot = idx.reshape(TOP_K, -1, tm_tok)[..., None] == experts
    pick = lambda table: jnp.sum(jnp.where(onehot, table[None, :, None, :], 0), axis=-1).reshape(TOP_K, t)
    pos = lrank + pick(src0)
    qpos = (lrank + pick(cstart)).T
    flat = jnp.arange(n_rows, dtype=jnp.int32)
    _, flat_sorted, w_sorted = lax.sort((pos.reshape(-1).astype(jnp.int32), flat, wgt.reshape(-1)), num_keys=1,
                                        is_stable=False)
    tok_sorted = flat_sorted % t
    bounds = jnp.sort(jnp.concatenate([jnp.arange(nt, dtype=jnp.int32) * tm, group_start.astype(jnp.int32)]))
    nxt = jnp.concatenate([bounds[1:], jnp.array([n_rows], jnp.int32)])
    tile = jnp.minimum(bounds // tm, nt - 1)
    lo = bounds - tile * tm
    hi = jnp.where(nxt > bounds, nxt - tile * tm, lo)
    expert = jnp.minimum(jnp.sum(group_end[None, :] <= bounds[:, None], axis=-1), N_EXPERTS - 1)
    last = lambda a, v: jnp.concatenate([a, jnp.array([v], a.dtype)])
    items = (last(tile, nt - 1), last(tile, nt), jnp.concatenate([expert, expert[-1:]]), last(lo, 0), last(hi, 0))
    items = tuple(a.astype(jnp.int32) for a in items)
    to_i32 = lambda a: a.astype(jnp.int32)
    return items, tok_sorted, w_sorted, to_i32(src0), to_i32(cstart), to_i32(nchunk), to_i32(qpos)


def _block_flags(n_prompt_seq, seg_len, n_tokens):
    nb = n_tokens // BLOCK
    per_seg = seg_len // BLOCK
    b = jnp.arange(nb)
    prompt_blocks = n_prompt_seq * per_seg
    in_prompt = b < prompt_blocks
    first = jnp.where(in_prompt, b % per_seg == 0, b == prompt_blocks)
    last = jnp.where(in_prompt, b % per_seg == per_seg - 1, b == nb - 1)
    return jnp.stack([~first, ~last]).astype(jnp.int32)


def _trunk(x_parts, n_prompt_seq, seg_len, p, tm, tm_moe, tc, nbq):
    n_parts = len(x_parts)
    x = list(x_parts)
    t = sum(a.shape[0] for a in x)
    depth = p['w_in'].shape[0]
    flags = _block_flags(n_prompt_seq, seg_len, t)
    bias = _band_bias(p['rel_bias'])
    zero_init = jnp.zeros((2, 1, N_SEG, SSM_REAL), F32)
    for layer in range(depth):
        ssm, qk, vt, sgu = _inproj(x, p['norm_mix'][layer][None], p['w_in'][layer].astype(BF16), 2 * tm)

        lam_bar, b_blk, a_rows, c_blk = _ssm_params(
            p['ssm_lam_re'][layer], p['ssm_lam_im'][layer], p['ssm_log_step'][layer], p['ssm_b_re'][layer],
            p['ssm_b_im'][layer], p['ssm_c_re'][layer], p['ssm_c_im'][layer])
        _, fin = _ssm_scan(ssm, seg_len, 1, 1, b_blk, a_rows, c_blk, zero_init, tc, False)
        init = jnp.concatenate([zero_init, _chain_init(fin[:, 0], lam_bar ** seg_len)[:, None]], axis=1)
        y, _ = _ssm_scan_pipelined(ssm, seg_len, 2, b_blk, a_rows, c_blk, init, 2 * tc)

        sink = jnp.repeat(p['attn_sink'][layer].astype(F32), BLOCK).reshape(N_KV_HEADS, 1, Q_PER_KV * BLOCK)
        sgb = jnp.repeat(p['sgu_b'][layer].astype(F32).T, SGU_GROUP_WIDTH, axis=1)
        attn, sguo = _attn_sgu(flags, qk, vt, sgu, bias, sink, p['sgu_norm'][layer][None].astype(F32),
                               p['sgu_w'][layer].astype(BF16), sgb, nbq)

        x1, h, idx, rank, wgt, cnt = _outproj_router(
            x, ssm, y, attn, sguo, p['ssm_d'][layer][None].astype(F32), p['out_norm'][layer][None],
            p['w_out'][layer].astype(BF16), p['norm_ffn'][layer][None], p['w_router'][layer].astype(F32),
            p['b_router'][layer][None].astype(F32), tm)

        items, tok_sorted, w_sorted, src0, cstart, nchunk, qpos = _routing_lists(
            idx, rank, wgt, cnt[:, :, 0], tm, tm_moe)
        y_sorted = _experts(items, tok_sorted, w_sorted, h, p['w_gate_up'].astype(F32),
                            p['b_gate_up'][:, :, None].astype(F32), p['w_down'].astype(F32),
                            p['b_down'][:, :, None].astype(F32), layer, tm_moe)
        last = layer == depth - 1
        x = _combine(src0, cstart, nchunk, y_sorted, x1, qpos, p['final_norm'][None].astype(F32), tm,
                     last, n_parts if last else 1)
    return x


def _run(x_prompt, x_sample, p, tm=512, tm_moe=512, tc=64, nbq=8):
    bsz, seg_len, _ = x_prompt.shape
    assert bsz == N_SEG and x_sample.shape[0] == 1 and x_sample.shape[1] == N_SEG * seg_len
    assert seg_len % (nbq * BLOCK) == 0 and seg_len % tc == 0 and seg_len % tm == 0
    parts = [x_prompt.reshape(-1, D_MODEL).astype(F32), x_sample.reshape(-1, D_MODEL).astype(F32)]
    y_prompt, y_sample = _trunk(parts, bsz, seg_len, p, tm, tm_moe, tc, nbq)
    return y_prompt.reshape(x_prompt.shape), y_sample.reshape(x_sample.shape)


def kernel(x_prompt, x_sample, norm_mix, w_in, ssm_lam_re, ssm_lam_im, ssm_log_step, ssm_b_re, ssm_b_im, ssm_c_re, ssm_c_im, ssm_d, attn_sink, rel_bias, sgu_norm, sgu_w, sgu_b, out_norm, w_out, norm_ffn, w_router, b_router, w_gate_up, b_gate_up, w_down, b_down, final_norm):
    p = dict(norm_mix=norm_mix, w_in=w_in, ssm_lam_re=ssm_lam_re, ssm_lam_im=ssm_lam_im, ssm_log_step=ssm_log_step,
             ssm_b_re=ssm_b_re, ssm_b_im=ssm_b_im, ssm_c_re=ssm_c_re, ssm_c_im=ssm_c_im, ssm_d=ssm_d,
             attn_sink=attn_sink, rel_bias=rel_bias, sgu_norm=sgu_norm, sgu_w=sgu_w, sgu_b=sgu_b, out_norm=out_norm,
             w_out=w_out, norm_ffn=norm_ffn, w_router=w_router, b_router=b_router, w_gate_up=w_gate_up,
             b_gate_up=b_gate_up, w_down=w_down, b_down=b_down, final_norm=final_norm)
    return _run(x_prompt, x_sample, p)
```
